```python
import math
import jax, jax.numpy as jnp
from jax import lax
import numpy as np

D_MODEL = 2048
BATCH = 8
SEQ = 4096
DEPTH = 2

CTX_LEN = 256
GRID_W = 64
HEAD_DIM = 128
N_GROUPS = 4
GROUP_WIDTH = D_MODEL // N_GROUPS
MIX_WIDTH = N_GROUPS * GROUP_WIDTH
Q_BLOCK = 128
WINDOW = 128
ROPE_THETA = 10000.0
EPS = 1e-6
NEG_INF = -1e30

A_HEADS = GROUP_WIDTH // HEAD_DIM
A_KV_HEADS = A_HEADS // 2
B_HEADS = GROUP_WIDTH // HEAD_DIM
B_V_DIM = HEAD_DIM
B_QK_DIM = B_V_DIM // 2
C_HEADS = GROUP_WIDTH // HEAD_DIM
C_Q_LORA = 448
C_KV_LORA = 128
C_NOPE = 128
C_ROPE = 64
C_V = GROUP_WIDTH // C_HEADS
D_HEADS = GROUP_WIDTH // HEAD_DIM
D_KV_HEADS = 2

IN_SIZES = (A_HEADS * HEAD_DIM, A_KV_HEADS * HEAD_DIM, A_KV_HEADS * HEAD_DIM, GROUP_WIDTH,
            2 * B_HEADS * B_QK_DIM, 2 * B_HEADS * B_QK_DIM, B_HEADS * B_V_DIM, GROUP_WIDTH,
            C_Q_LORA, C_KV_LORA, C_ROPE, GROUP_WIDTH,
            D_HEADS * HEAD_DIM, D_KV_HEADS * HEAD_DIM, D_KV_HEADS * HEAD_DIM, GROUP_WIDTH)
IN_WIDTH = sum(IN_SIZES)

kernel_name = 'hybrid_parallel_heads_dit_ctx_prefix'


def rms_norm(x, g, eps=EPS):
    xf = x.astype(jnp.float32)
    y = xf * lax.rsqrt(jnp.mean(xf * xf, axis=-1, keepdims=True) + eps)
    return (y * g.astype(jnp.float32)).astype(x.dtype)


def axial_rope_tables(rows, rot_dim):
    row = jnp.broadcast_to(jnp.arange(rows)[:, None], (rows, GRID_W)).reshape(-1).astype(jnp.float32)
    col = jnp.broadcast_to(jnp.arange(GRID_W)[None, :], (rows, GRID_W)).reshape(-1).astype(jnp.float32)
    axis_dim = rot_dim // 2
    inv_freq = ROPE_THETA ** (-jnp.arange(0, axis_dim, 2, dtype=jnp.float32) / axis_dim)
    ang_r = row[:, None] * inv_freq[None, :]
    ang_c = col[:, None] * inv_freq[None, :]
    ang = jnp.concatenate([ang_r, ang_r, ang_c, ang_c], axis=-1)
    return jnp.cos(ang), jnp.sin(ang)


def apply_rope(x, rope):
    cos, sin = rope
    x1, x2, x3, x4 = jnp.split(x, 4, axis=-1)
    rot = jnp.concatenate([-x2, x1, -x4, x3], axis=-1)
    return (x * cos[:, None, :] + rot * sin[:, None, :]).astype(x.dtype)


def split_cols(y, sizes):
    cuts = [int(v) for v in np.cumsum(sizes)[:-1]]
    return jnp.split(y, cuts, axis=-1)


def branch_inputs(h, w_in, cq_g, ckv_g, w_uq, w_ukv, dq_g, dk_g, ropes):
    B, T, _ = h.shape
    (a_q, a_k, a_v, a_z, b_q, b_k, b_v, b_z,
     c_cq, c_ckv, c_kr, c_z, d_q, d_k, d_v, d_z) = split_cols(h @ w_in, IN_SIZES)
    a_q = a_q.reshape(B, T, A_HEADS, HEAD_DIM)
    a_k = a_k.reshape(B, T, A_KV_HEADS, HEAD_DIM)
    a_v = a_v.reshape(B, T, A_KV_HEADS, HEAD_DIM)
    b_q = b_q.reshape(B, T, 2 * B_HEADS, B_QK_DIM)
    b_k = b_k.reshape(B, T, 2 * B_HEADS, B_QK_DIM)
    b_v = b_v.reshape(B, T, B_HEADS, B_V_DIM)
    c_qh = (rms_norm(c_cq, cq_g) @ w_uq).reshape(B, T, C_HEADS, C_NOPE + C_ROPE)
    c_q_nope, c_q_rope = c_qh[..., :C_NOPE], c_qh[..., C_NOPE:]
    c_kvh = (rms_norm(c_ckv, ckv_g) @ w_ukv).reshape(B, T, C_HEADS, C_NOPE + C_V)
    c_k_nope, c_v = c_kvh[..., :C_NOPE], c_kvh[..., C_NOPE:]
    c_kr = c_kr.reshape(B, T, 1, C_ROPE)
    d_q = rms_norm(d_q.reshape(B, T, D_HEADS, HEAD_DIM), dq_g)
    d_k = rms_norm(d_k.reshape(B, T, D_KV_HEADS, HEAD_DIM), dk_g)
    d_v = d_v.reshape(B, T, D_KV_HEADS, HEAD_DIM)
    if ropes is not None:
        rope_h, rope_b, rope_c = ropes
        a_q, a_k = apply_rope(a_q, rope_h), apply_rope(a_k, rope_h)
        b_q, b_k = apply_rope(b_q, rope_b), apply_rope(b_k, rope_b)
        c_q_rope, c_kr = apply_rope(c_q_rope, rope_c), apply_rope(c_kr, rope_c)
        d_q, d_k = apply_rope(d_q, rope_h), apply_rope(d_k, rope_h)
    c_q = jnp.concatenate([c_q_nope, c_q_rope], axis=-1)[:, :, :, None, :]
    c_k = jnp.concatenate([c_k_nope, jnp.broadcast_to(c_kr, (B, T, C_HEADS, C_ROPE))], axis=-1)
    return {
        'a_q': a_q.reshape(B, T, A_KV_HEADS, A_HEADS // A_KV_HEADS, HEAD_DIM), 'a_k': a_k, 'a_v': a_v, 'a_z': a_z,
        'b_q': b_q.reshape(B, T, B_HEADS, 2, B_QK_DIM), 'b_k': b_k.reshape(B, T, B_HEADS, 2, B_QK_DIM),
        'b_v': b_v, 'b_z': b_z,
        'c_q': c_q, 'c_k': c_k, 'c_v': c_v, 'c_z': c_z,
        'd_q': d_q.reshape(B, T, D_KV_HEADS, D_HEADS // D_KV_HEADS, HEAD_DIM), 'd_k': d_k, 'd_v': d_v, 'd_z': d_z,
    }


def block_attention(q, k, v, sink=None):
    B, S, Hkv, G, dk = q.shape
    nb = S // Q_BLOCK
    scale = 1.0 / math.sqrt(dk)
    qb = jnp.moveaxis(q.reshape(B, nb, Q_BLOCK, Hkv, G, dk), 1, 0)
    n_keys = k.shape[1]

    def one(qi):
        s = jnp.einsum('bqhgd,bkhd->bhgqk', qi, k, preferred_element_type=jnp.float32) * scale
        if sink is not None:
            s_sink = jnp.broadcast_to(sink.astype(jnp.float32)[None, :, :, None, None], s.shape[:-1] + (1,))
            s = jnp.concatenate([s, s_sink], axis=-1)
        p = jax.nn.softmax(s, axis=-1)[..., :n_keys]
        return jnp.einsum('bhgqk,bkhd->bqhgd', p.astype(v.dtype), v)

    o = lax.map(one, qb)
    return jnp.moveaxis(o, 0, 1).reshape(B, S, Hkv, G, v.shape[-1])


def windowed_attention(q, k, v, k_ctx, v_ctx, sink):
    B, S, Hkv, G, d = q.shape
    nb = S // Q_BLOCK
    n_ctx = k_ctx.shape[1]
    scale = 1.0 / math.sqrt(d)
    qb = q.reshape(B, nb, Q_BLOCK, Hkv, G, d)
    pad = [(0, 0), (Q_BLOCK, Q_BLOCK), (0, 0), (0, 0)]
    kb = jnp.pad(k, pad).reshape(B, nb + 2, Q_BLOCK, Hkv, d)
    vb = jnp.pad(v, pad).reshape(B, nb + 2, Q_BLOCK, Hkv, d)
    k_band = jnp.concatenate([kb[:, :-2], kb[:, 1:-1], kb[:, 2:]], axis=2)
    v_band = jnp.concatenate([vb[:, :-2], vb[:, 1:-1], vb[:, 2:]], axis=2)
    blk = jnp.arange(nb)[:, None]
    qpos = blk * Q_BLOCK + jnp.arange(Q_BLOCK)[None, :]
    kpos = (blk - 1) * Q_BLOCK + jnp.arange(3 * Q_BLOCK)[None, :]
    valid = ((jnp.abs(qpos[:, :, None] - kpos[:, None, :]) <= WINDOW)
             & (kpos >= 0)[:, None, :] & (kpos < S)[:, None, :])
    s_band = jnp.einsum('bnqhgd,bnkhd->bnhgqk', qb, k_band, preferred_element_type=jnp.float32) * scale
    s_band = jnp.where(valid[None, :, None, None], s_band, NEG_INF)
    s_ctx = jnp.einsum('bnqhgd,bkhd->bnhgqk', qb, k_ctx, preferred_element_type=jnp.float32) * scale
    s_sink = jnp.broadcast_to(sink.astype(jnp.float32)[None, None, :, :, None, None], s_ctx.shape[:-1] + (1,))
    p = jax.nn.softmax(jnp.concatenate([s_ctx, s_band, s_sink], axis=-1), axis=-1)
    p_ctx = p[..., :n_ctx].astype(v.dtype)
    p_band = p[..., n_ctx:n_ctx + 3 * Q_BLOCK].astype(v.dtype)
    o = (jnp.einsum('bnhgqk,bkhd->bnqhgd', p_ctx, v_ctx)
         + jnp.einsum('bnhgqk,bnkhd->bnqhgd', p_band, v_band))
    return o.reshape(B, S, Hkv, G, d)


def diff_block_attention(q, k, v, lam):
    B, S, H, _, dk = q.shape
    nb = S // Q_BLOCK
    scale = 1.0 / math.sqrt(dk)
    qb = jnp.moveaxis(q.reshape(B, nb, Q_BLOCK, H, 2, dk), 1, 0)

    def one(qi):
        s = jnp.einsum('bqhcd,bkhcd->bhcqk', qi, k, preferred_element_type=jnp.float32) * scale
        p = jax.nn.softmax(s, axis=-1)
        a = p[:, :, 0] - lam * p[:, :, 1]
        return jnp.einsum('bhqk,bkhd->bqhd', a.astype(v.dtype), v)

    o = lax.map(one, qb)
    return jnp.moveaxis(o, 0, 1).reshape(B, S, H, v.shape[-1])


def mixers(p, pc, sink, lam, lam_init, subln_g, latent):
    B, T = p['a_z'].shape[:2]
    if latent:
        a_o = windowed_attention(p['a_q'], p['a_k'], p['a_v'], pc['a_k'], pc['a_v'], sink)
        keys = lambda name: jnp.concatenate([pc[name], p[name]], axis=1)
    else:
        a_o = block_attention(p['a_q'], p['a_k'], p['a_v'], sink=sink)
        keys = lambda name: pc[name]
    b_o = diff_block_attention(p['b_q'], keys('b_k'), keys('b_v'), lam)
    b_o = rms_norm(b_o, subln_g) * (1.0 - lam_init)
    c_o = block_attention(p['c_q'], keys('c_k'), keys('c_v'))
    d_o = block_attention(p['d_q'], keys('d_k'), keys('d_v'))
    return jnp.concatenate([
        a_o.reshape(B, T, GROUP_WIDTH) * jax.nn.silu(p['a_z']),
        b_o.reshape(B, T, GROUP_WIDTH) * jax.nn.silu(p['b_z']),
        c_o.reshape(B, T, GROUP_WIDTH) * jax.nn.silu(p['c_z']),
        d_o.reshape(B, T, GROUP_WIDTH) * jax.nn.silu(p['d_z']),
    ], axis=-1)


def setup_inputs(seed: int = 0) -> dict:
    key = jax.random.key(seed)
    ks = jax.random.split(key, 19)
    f32 = jnp.float32

    def nrm(k, shape, scale):
        return jax.random.normal(k, shape, f32) * scale

    def gain(k, shape):
        return 1.0 + 0.05 * jax.random.normal(k, shape, f32)

    return {
        'x': nrm(ks[0], (BATCH, SEQ, D_MODEL), 1.0),
        'c': nrm(ks[1], (BATCH, D_MODEL), 1.0),
        'ctx': nrm(ks[2], (BATCH, CTX_LEN, D_MODEL), 1.0),
        'c_ctx': nrm(ks[3], (D_MODEL,), 1.0),
        'w_mod': nrm(ks[4], (DEPTH, D_MODEL, 3 * D_MODEL), 0.2 * D_MODEL ** -0.5),
        'b_mod': nrm(ks[5], (DEPTH, 3 * D_MODEL), 0.02),
        'norm_g': gain(ks[6], (DEPTH, D_MODEL)),
        'w_in': nrm(ks[7], (DEPTH, D_MODEL, IN_WIDTH), D_MODEL ** -0.5),
        'c_q_norm_g': gain(ks[8], (DEPTH, C_Q_LORA)),
        'c_kv_norm_g': gain(ks[9], (DEPTH, C_KV_LORA)),
        'c_w_uq': nrm(ks[10], (DEPTH, C_Q_LORA, C_HEADS * (C_NOPE + C_ROPE)), C_Q_LORA ** -0.5),
        'c_w_ukv': nrm(ks[11], (DEPTH, C_KV_LORA, C_HEADS * (C_NOPE + C_V)), C_KV_LORA ** -0.5),
        'd_q_norm_g': gain(ks[12], (DEPTH, HEAD_DIM)),
        'd_k_norm_g': gain(ks[13], (DEPTH, HEAD_DIM)),
        'a_sink': nrm(ks[14], (DEPTH, A_HEADS), 0.5),
        'b_lambda': nrm(ks[15], (DEPTH, 4, B_QK_DIM), 0.1),
        'b_subln_g': gain(ks[16], (DEPTH, B_V_DIM)),
        'w_out': nrm(ks[17], (DEPTH, MIX_WIDTH, D_MODEL), MIX_WIDTH ** -0.5),
        'final_norm_g': gain(ks[18], (D_MODEL,)),
    }


def reference(x, c, ctx, c_ctx, w_mod, b_mod, norm_g, w_in, c_q_norm_g, c_kv_norm_g,
              c_w_uq, c_w_ukv, d_q_norm_g, d_k_norm_g, a_sink, b_lambda, b_subln_g,
              w_out, final_norm_g):
    B, S, _ = x.shape
    ROWS = S // GRID_W
    rope_h = axial_rope_tables(ROWS, HEAD_DIM)
    rope_b = axial_rope_tables(ROWS, B_QK_DIM)
    rope_c = axial_rope_tables(ROWS, C_ROPE)
    for l in range(DEPTH):
        last = l == DEPTH - 1
        shift_x, scale_x, gate_x = jnp.split(jax.nn.silu(c) @ w_mod[l] + b_mod[l], 3, axis=-1)
        shift_c, scale_c, gate_c = jnp.split(jax.nn.silu(c_ctx) @ w_mod[l] + b_mod[l], 3, axis=-1)
        hx = rms_norm(x, norm_g[l]) * (1.0 + scale_x[:, None, :]) + shift_x[:, None, :]
        hc = rms_norm(ctx, norm_g[l]) * (1.0 + scale_c) + shift_c
        lw = (w_in[l], c_q_norm_g[l], c_kv_norm_g[l], c_w_uq[l], c_w_ukv[l], d_q_norm_g[l], d_k_norm_g[l])
        px = branch_inputs(hx, *lw, ropes=(rope_h, rope_b, rope_c))
        pc = branch_inputs(hc, *lw, ropes=None)
        lam_init = 0.8 - 0.6 * math.exp(-0.3 * l)
        lq1, lk1, lq2, lk2 = b_lambda[l].astype(jnp.float32)
        lam = jnp.exp(jnp.sum(lq1 * lk1)) - jnp.exp(jnp.sum(lq2 * lk2)) + lam_init
        sink = a_sink[l].reshape(A_KV_HEADS, A_HEADS // A_KV_HEADS)
        ux = mixers(px, pc, sink, lam, lam_init, b_subln_g[l], latent=True)
        if not last:
            uc = mixers(pc, pc, sink, lam, lam_init, b_subln_g[l], latent=False)
            ctx = ctx + gate_c * (uc @ w_out[l])
        x = x + gate_x[:, None, :] * (ux @ w_out[l])
    return rms_norm(x, final_norm_g)
```

```python
import functools
import math

import jax
import jax.numpy as jnp
from jax import lax
from jax.experimental import pallas as pl
from jax.experimental.pallas import tpu as pltpu

F32 = jnp.float32
BF16 = jnp.bfloat16

D_MODEL = 2048
HEAD_DIM = 128
GRID_W = 64
GROUP_WIDTH = 512
WINDOW = 128
ROPE_THETA = 10000.0
EPS = 1e-6
NEG_INF = -1e30
C_Q_LORA = 448
C_ROPE = 64
C_QK_DIM = 192
B_QK_DIM = 64
LANES = 128
MOD_ROWS = 16
VMEM_LIMIT = 56 * 1024 * 1024

TM_LATENT = 512
KV_CHUNK = 512
TQ_DENSE = 512
TQ_BAND = 256
MOD_TN = 512


def _dotf(a, b):
    return jnp.dot(a, b, preferred_element_type=F32)


def _dot_nt(a, b):
    return lax.dot_general(a, b, (((1,), (1,)), ((), ())), preferred_element_type=F32)


def _silu(z):
    return z / (1.0 + jnp.exp(-z))


def _rms(y, inv_n):
    return y * lax.rsqrt(jnp.sum(y * y, axis=-1, keepdims=True) * inv_n + EPS)


def _rope(x, cos, sin_signed, shift):
    lane = lax.broadcasted_iota(jnp.int32, x.shape, 1)
    even = (lane & shift) == 0
    x_up = pltpu.roll(x, LANES - shift, 1)
    x_dn = pltpu.roll(x, shift, 1)
    return x * cos + jnp.where(even, x_up, x_dn) * sin_signed


def _params(n_axes):
    return pltpu.CompilerParams(dimension_semantics=("parallel",) * n_axes,
                                vmem_limit_bytes=VMEM_LIMIT)


def _mod_kernel(c_ref, w_ref, b_ref, o_ref):
    a = _silu(c_ref[...]).astype(BF16)
    o_ref[0] = _dotf(a, w_ref[0].astype(BF16)) + b_ref[0]


def _modulation(cc, w_mod, b_mod):
    depth, d, n = w_mod.shape
    return pl.pallas_call(
        _mod_kernel,
        grid=(depth, n // MOD_TN),
        in_specs=[pl.BlockSpec((MOD_ROWS, d), lambda l, j: (0, 0)),
                  pl.BlockSpec((1, d, MOD_TN), lambda l, j: (l, 0, j)),
                  pl.BlockSpec((1, 1, MOD_TN), lambda l, j: (l, 0, j))],
        out_specs=pl.BlockSpec((1, MOD_ROWS, MOD_TN), lambda l, j: (l, 0, j)),
        out_shape=jax.ShapeDtypeStruct((depth, MOD_ROWS, n), F32),
        compiler_params=_params(2),
        name="modulation",
    )(cc, w_mod, b_mod.reshape(depth, 1, n))


def _prenorm_kernel(x_ref, g_ref, shift_ref, scale_ref, h_ref):
    x = x_ref[0]
    y = _rms(x, 1.0 / D_MODEL) * g_ref[...]
    h_ref[0] = (y * (1.0 + scale_ref[0, 0]) + shift_ref[0, 0]).astype(BF16)


def _mod_spec(layer, row_fn, which):
    return pl.BlockSpec((1, 1, 1, D_MODEL), lambda b, i: (layer, row_fn(b), 0, which))


def _prenorm(x, g, mods, layer, row_fn, tm):
    bsz, t, d = x.shape
    return pl.pallas_call(
        _prenorm_kernel,
        grid=(bsz, t // tm),
        in_specs=[pl.BlockSpec((1, tm, d), lambda b, i: (b, i, 0)),
                  pl.BlockSpec((1, d), lambda b, i: (0, 0)),
                  _mod_spec(layer, row_fn, 0), _mod_spec(layer, row_fn, 1)],
        out_specs=pl.BlockSpec((1, tm, d), lambda b, i: (b, i, 0)),
        out_shape=jax.ShapeDtypeStruct((bsz, t, d), BF16),
        compiler_params=_params(2),
        name="prenorm",
    )(x, g, mods, mods)


def _store_vt(vt_ref, head, y):
    vt_ref[0, head, 0] = y.T.astype(BF16)


def _proj_a_kernel(h_ref, w_ref, cos_ref, sin_ref, q_ref, k_ref, vt_ref, gz_ref):
    h = h_ref[0]
    cos, sin = cos_ref[...], sin_ref[...]
    scale = 1.0 / math.sqrt(HEAD_DIM)
    yq = _dotf(h, w_ref[:, 0:512])
    for hh in range(4):
        q_ref[0, hh] = (_rope(yq[:, hh * 128:(hh + 1) * 128], cos, sin, 32) * scale).astype(BF16)
    ykv = _dotf(h, w_ref[:, 512:1024])
    for hh in range(2):
        k_ref[0, hh] = _rope(ykv[:, hh * 128:(hh + 1) * 128], cos, sin, 32).astype(BF16)
        _store_vt(vt_ref, hh, ykv[:, 256 + hh * 128:256 + (hh + 1) * 128])
    gz_ref[0] = _silu(_dotf(h, w_ref[:, 1024:1536])).astype(BF16)


def _proj_b_kernel(h_ref, w_ref, cos_ref, sin_ref, q_ref, k_ref, vt_ref, gz_ref):
    h = h_ref[0]
    cos, sin = cos_ref[...], sin_ref[...]
    scale = 1.0 / math.sqrt(B_QK_DIM)
    yq = _dotf(h, w_ref[:, 0:512])
    lane = lax.broadcasted_iota(jnp.int32, (h.shape[0], LANES), 1)
    first = lane < B_QK_DIM
    for hh in range(4):
        y = _rope(yq[:, hh * 128:(hh + 1) * 128], cos, sin, 16) * scale
        q_ref[0, 2 * hh] = jnp.where(first, y, 0.0).astype(BF16)
        q_ref[0, 2 * hh + 1] = jnp.where(first, 0.0, y).astype(BF16)
    yk = _dotf(h, w_ref[:, 512:1024])
    for hh in range(4):
        k_ref[0, hh] = _rope(yk[:, hh * 128:(hh + 1) * 128], cos, sin, 16).astype(BF16)
    yv = _dotf(h, w_ref[:, 1024:1536])
    for hh in range(4):
        _store_vt(vt_ref, hh, yv[:, hh * 128:(hh + 1) * 128])
    gz_ref[0] = _silu(_dotf(h, w_ref[:, 1536:2048])).astype(BF16)


def _proj_c_kernel(h_ref, w_ref, wuq_ref, wukv_ref, gq_ref, gkv_ref, cos_ref, sin_ref,
                   q_ref, k_ref, vt_ref, gz_ref):
    h = h_ref[0]
    cos, sin = cos_ref[...], sin_ref[...]
    scale = 1.0 / math.sqrt(C_QK_DIM)
    ycq = _dotf(h, w_ref[:, 0:512])
    cqn = (_rms(ycq, 1.0 / C_Q_LORA) * gq_ref[...]).astype(BF16)
    yq = _dotf(cqn, wuq_ref[...])
    yckv = _dotf(h, w_ref[:, 512:768])
    ckvn = (_rms(yckv[:, 0:128], 1.0 / 128) * gkv_ref[...]).astype(BF16)
    ykv = _dotf(ckvn, wukv_ref[...])
    kr_even = _rope(yckv[:, 128:256], cos, sin, 16)
    kr_odd = pltpu.roll(kr_even, C_ROPE, 1)
    qr = [_rope(yq[:, 512 + p * 128:512 + (p + 1) * 128], cos, sin, 16) * scale for p in range(2)]
    for hh in range(4):
        q_ref[0, hh, :, 0:128] = (yq[:, hh * 128:(hh + 1) * 128] * scale).astype(BF16)
        q_ref[0, hh, :, 128:256] = qr[hh // 2].astype(BF16)
        k_ref[0, hh, :, 0:128] = ykv[:, hh * 256:hh * 256 + 128].astype(BF16)
        k_ref[0, hh, :, 128:256] = (kr_even if hh % 2 == 0 else kr_odd).astype(BF16)
        _store_vt(vt_ref, hh, ykv[:, hh * 256 + 128:(hh + 1) * 256])
    gz_ref[0] = _silu(_dotf(h, w_ref[:, 768:1280])).astype(BF16)


def _proj_d_kernel(h_ref, w_ref, gq_ref, gk_ref, cos_ref, sin_ref, q_ref, k_ref, vt_ref, gz_ref):
    h = h_ref[0]
    cos, sin = cos_ref[...], sin_ref[...]
    scale = 1.0 / math.sqrt(HEAD_DIM)
    yq = _dotf(h, w_ref[:, 0:512])
    for hh in range(4):
        yn = _rms(yq[:, hh * 128:(hh + 1) * 128], 1.0 / HEAD_DIM) * gq_ref[...]
        q_ref[0, hh] = (_rope(yn, cos, sin, 32) * scale).astype(BF16)
    ykv = _dotf(h, w_ref[:, 512:1024])
    for hh in range(2):
        yn = _rms(ykv[:, hh * 128:(hh + 1) * 128], 1.0 / HEAD_DIM) * gk_ref[...]
        k_ref[0, hh] = _rope(yn, cos, sin, 32).astype(BF16)
        _store_vt(vt_ref, hh, ykv[:, 256 + hh * 128:256 + (hh + 1) * 128])
    gz_ref[0] = _silu(_dotf(h, w_ref[:, 1024:1536])).astype(BF16)


def _full_spec(a):
    nd = a.ndim
    return pl.BlockSpec(a.shape, lambda b, i: (0,) * nd)


def _project(kern, name, h, weights, tables, tm, n_q, dq, n_kv, dk):
    bsz, t, d = h.shape
    nt = t // tm
    tab_specs = [pl.BlockSpec((tm, LANES), (lambda b, i: (i, 0)) if tab.shape[0] == t
                              else (lambda b, i: (0, 0))) for tab in tables]
    return pl.pallas_call(
        kern,
        grid=(bsz, nt),
        in_specs=[pl.BlockSpec((1, tm, d), lambda b, i: (b, i, 0))]
        + [_full_spec(w) for w in weights] + tab_specs,
        out_specs=[pl.BlockSpec((1, n_q, tm, dq), lambda b, i: (b, 0, i, 0)),
                   pl.BlockSpec((1, n_kv, tm, dk), lambda b, i: (b, 0, i, 0)),
                   pl.BlockSpec((1, n_kv, 1, HEAD_DIM, tm), lambda b, i: (b, 0, i, 0, 0)),
                   pl.BlockSpec((1, tm, GROUP_WIDTH), lambda b, i: (b, i, 0))],
        out_shape=[jax.ShapeDtypeStruct((bsz, n_q, t, dq), BF16),
                   jax.ShapeDtypeStruct((bsz, n_kv, t, dk), BF16),
                   jax.ShapeDtypeStruct((bsz, n_kv, nt, HEAD_DIM, tm), BF16),
                   jax.ShapeDtypeStruct((bsz, t, GROUP_WIDTH), BF16)],
        compiler_params=_params(2),
        name=name,
    )(h, *weights, *tables)


def _finish_plain(acc, l, gate_ref, out_ref, n_sets, tq):
    o_t = acc * (1.0 / l)
    for g in range(n_sets):
        o = o_t[:, g * tq:(g + 1) * tq].T
        gate = gate_ref[0, :, g * 128:(g + 1) * 128].astype(F32)
        out_ref[0, :, g * 128:(g + 1) * 128] = (o * gate).astype(BF16)


def _finish_diff(acc, l, gate_ref, out_ref, lam_ref, subg_ref, lam_init, tq):
    o_t = acc * (1.0 / l)
    lp = lam_ref[...]
    s1 = jnp.sum(lp[0:1] * lp[1:2], axis=1, keepdims=True)
    s2 = jnp.sum(lp[2:3] * lp[3:4], axis=1, keepdims=True)
    lam = jnp.exp(s1) - jnp.exp(s2) + lam_init
    d = o_t[:, 0:tq] - lam * o_t[:, tq:2 * tq]
    dn = d * lax.rsqrt(jnp.mean(d * d, axis=0, keepdims=True) + EPS)
    o = dn.T * subg_ref[...] * (1.0 - lam_init)
    out_ref[0] = (o * gate_ref[0].astype(F32)).astype(BF16)


def _dense_attn_kernel(*refs, n_sets, tq, n_chunks, has_sink, diff, lam_init):
    refs = list(refs)
    q_ref, kc_ref, vtc_ref = refs[:3]
    pos = 3
    if n_chunks:
        kl_ref, vtl_ref = refs[pos:pos + 2]
        pos += 2
    gate_ref = refs[pos]
    pos += 1
    if has_sink:
        sink_ref = refs[pos]
        pos += 1
    if diff:
        lam_ref, subg_ref = refs[pos:pos + 2]
        pos += 2
    out_ref, acc_ref = refs[pos:pos + 2]

    n = n_sets * tq
    q = q_ref[0].reshape(n, q_ref.shape[-1])
    if has_sink:
        m0 = sink_ref[0]
        l0 = jnp.ones((1, n), F32)
    else:
        m0 = jnp.full((1, n), NEG_INF, F32)
        l0 = jnp.zeros((1, n), F32)

    def step(k, vt, m, l, first):
        s = _dot_nt(k, q)
        m_new = jnp.maximum(m, jnp.max(s, axis=0, keepdims=True))
        alpha = jnp.exp(m - m_new)
        p = jnp.exp(s - m_new)
        l_new = alpha * l + jnp.sum(p, axis=0, keepdims=True)
        pv = _dotf(vt, p.astype(BF16))
        if first:
            acc_ref[...] = pv
        else:
            acc_ref[...] = alpha * acc_ref[...] + pv
        return m_new, l_new

    m, l = step(kc_ref[0, 0], vtc_ref[0, 0, 0], m0, l0, True)
    if n_chunks:
        def body(j, carry):
            start = pl.multiple_of(j * KV_CHUNK, KV_CHUNK)
            return step(kl_ref[0, 0, pl.ds(start, KV_CHUNK), :], vtl_ref[0, 0, j], *carry, False)
        m, l = lax.fori_loop(0, n_chunks, body, (m, l))

    if diff:
        _finish_diff(acc_ref[...], l, gate_ref, out_ref, lam_ref, subg_ref, lam_init, tq)
    else:
        _finish_plain(acc_ref[...], l, gate_ref, out_ref, n_sets, tq)


def _dense_attention(name, q, kc, vtc, kl, vtl, gate, n_sets, tq, sink=None, diff=None):
    bsz, _, t, dk = q.shape
    hkv, tc = kc.shape[1], kc.shape[2]
    dv = vtc.shape[3]
    n_chunks = 0 if kl is None else kl.shape[2] // KV_CHUNK
    out_w = LANES if diff else n_sets * LANES
    n = n_sets * tq
    args = [q, kc, vtc]
    specs = [pl.BlockSpec((1, n_sets, tq, dk), lambda b, h, i: (b, h, i, 0)),
             pl.BlockSpec((1, 1, tc, dk), lambda b, h, i: (b, h, 0, 0)),
             pl.BlockSpec((1, 1, 1, dv, tc), lambda b, h, i: (b, h, 0, 0, 0))]
    if n_chunks:
        args += [kl, vtl]
        specs += [pl.BlockSpec((1, 1, kl.shape[2], dk), lambda b, h, i: (b, h, 0, 0)),
                  pl.BlockSpec((1, 1, n_chunks, dv, KV_CHUNK), lambda b, h, i: (b, h, 0, 0, 0))]
    args.append(gate)
    specs.append(pl.BlockSpec((1, tq, out_w), lambda b, h, i: (b, i, h)))
    if sink is not None:
        args.append(sink)
        specs.append(pl.BlockSpec((1, 1, n), lambda b, h, i: (h, 0, 0)))
    lam_init = 0.0
    if diff is not None:
        lam_par, subg, lam_init = diff
        args += [lam_par, subg]
        specs += [pl.BlockSpec(lam_par.shape, lambda b, h, i: (0, 0)),
                  pl.BlockSpec(subg.shape, lambda b, h, i: (0, 0))]
    kern = functools.partial(_dense_attn_kernel, n_sets=n_sets, tq=tq, n_chunks=n_chunks,
                             has_sink=sink is not None, diff=diff is not None, lam_init=lam_init)
    return pl.pallas_call(
        kern,
        grid=(bsz, hkv, t // tq),
        in_specs=specs,
        out_specs=pl.BlockSpec((1, tq, out_w), lambda b, h, i: (b, i, h)),
        out_shape=jax.ShapeDtypeStruct((bsz, t, GROUP_WIDTH), BF16),
        scratch_shapes=[pltpu.VMEM((dv, n), F32)],
        compiler_params=_params(3),
        name=name,
    )(*args)


def _band_attn_kernel(q_ref, kc_ref, vtc_ref, kp_ref, km_ref, kn_ref, vp_ref, vm_ref, vn_ref,
                      gate_ref, sink_ref, out_ref, *, seq_len):
    tq = TQ_BAND
    n = 2 * tq
    q = q_ref[0].reshape(n, HEAD_DIM)
    q_start = pl.program_id(2) * tq
    q_rel = lax.broadcasted_iota(jnp.int32, (1, n), 1) & (tq - 1)

    def band_scores(k_ref, rel0):
        rows = k_ref.shape[2]
        s = _dot_nt(k_ref[0, 0], q)
        k_rel = rel0 + lax.broadcasted_iota(jnp.int32, (rows, 1), 0)
        k_pos = q_start + k_rel
        valid = (jnp.abs(q_rel - k_rel) <= WINDOW) & (k_pos >= 0) & (k_pos < seq_len)
        return jnp.where(valid, s, NEG_INF)

    scores = [_dot_nt(kc_ref[0, 0], q), band_scores(kp_ref, -WINDOW), band_scores(km_ref, 0),
              band_scores(kn_ref, tq)]
    vts = [vtc_ref[0, 0, 0], vp_ref[0, 0, 0], vm_ref[0, 0, 0], vn_ref[0, 0, 0]]
    m = sink_ref[0]
    for s in scores:
        m = jnp.maximum(m, jnp.max(s, axis=0, keepdims=True))
    l = jnp.exp(sink_ref[0] - m)
    acc = None
    for s, vt in zip(scores, vts):
        p = jnp.exp(s - m)
        l = l + jnp.sum(p, axis=0, keepdims=True)
        pv = _dotf(vt, p.astype(BF16))
        acc = pv if acc is None else acc + pv
    _finish_plain(acc, l, gate_ref, out_ref, 2, tq)


def _band_attention(q, kc, vtc, kl, vtl, gate, sink):
    bsz, _, s, dk = q.shape
    tq = TQ_BAND
    tc = kc.shape[2]
    n_blk = s // WINDOW
    per_chunk = TM_LATENT // WINDOW

    def prev_blk(i):
        return jnp.maximum(2 * i - 1, 0)

    def next_blk(i):
        return jnp.minimum(2 * i + 2, n_blk - 1)

    half = TM_LATENT // tq
    specs = [
        pl.BlockSpec((1, 2, tq, dk), lambda b, h, i: (b, h, i, 0)),
        pl.BlockSpec((1, 1, tc, dk), lambda b, h, i: (b, h, 0, 0)),
        pl.BlockSpec((1, 1, 1, HEAD_DIM, tc), lambda b, h, i: (b, h, 0, 0, 0)),
        pl.BlockSpec((1, 1, WINDOW, dk), lambda b, h, i: (b, h, prev_blk(i), 0)),
        pl.BlockSpec((1, 1, tq, dk), lambda b, h, i: (b, h, i, 0)),
        pl.BlockSpec((1, 1, WINDOW, dk), lambda b, h, i: (b, h, next_blk(i), 0)),
        pl.BlockSpec((1, 1, 1, HEAD_DIM, WINDOW),
                     lambda b, h, i: (b, h, prev_blk(i) // per_chunk, 0, prev_blk(i) % per_chunk)),
        pl.BlockSpec((1, 1, 1, HEAD_DIM, tq), lambda b, h, i: (b, h, i // half, 0, i % half)),
        pl.BlockSpec((1, 1, 1, HEAD_DIM, WINDOW),
                     lambda b, h, i: (b, h, next_blk(i) // per_chunk, 0, next_blk(i) % per_chunk)),
        pl.BlockSpec((1, tq, 2 * LANES), lambda b, h, i: (b, i, h)),
        pl.BlockSpec((1, 1, 2 * tq), lambda b, h, i: (h, 0, 0)),
    ]
    return pl.pallas_call(
        functools.partial(_band_attn_kernel, seq_len=s),
        grid=(bsz, 2, s // tq),
        in_specs=specs,
        out_specs=pl.BlockSpec((1, tq, 2 * LANES), lambda b, h, i: (b, i, h)),
        out_shape=jax.ShapeDtypeStruct((bsz, s, GROUP_WIDTH), BF16),
        compiler_params=_params(3),
        name="attn_a_band",
    )(q, kc, vtc, kl, kl, kl, vtl, vtl, vtl, gate, sink)


def _out_kernel(*refs, final):
    ua, ub, uc, ud, w_ref, x_ref, gate_ref = refs[:7]
    out_ref = refs[-1]
    acc = (_dotf(ua[0], w_ref[0]) + _dotf(ub[0], w_ref[1])
           + _dotf(uc[0], w_ref[2]) + _dotf(ud[0], w_ref[3]))
    xn = x_ref[0] + gate_ref[0, 0] * acc
    if final:
        xn = _rms(xn, 1.0 / D_MODEL) * refs[7][...]
    out_ref[0] = xn


def _out_project(us, w_out, x, mods, layer, row_fn, tm, final_g=None):
    bsz, t, d = x.shape
    u_spec = pl.BlockSpec((1, tm, GROUP_WIDTH), lambda b, i: (b, i, 0))
    specs = [u_spec] * 4 + [_full_spec(w_out), pl.BlockSpec((1, tm, d), lambda b, i: (b, i, 0)),
                            _mod_spec(layer, row_fn, 2)]
    args = list(us) + [w_out, x, mods]
    if final_g is not None:
        specs.append(pl.BlockSpec((1, d), lambda b, i: (0, 0)))
        args.append(final_g)
    return pl.pallas_call(
        functools.partial(_out_kernel, final=final_g is not None),
        grid=(bsz, t // tm),
        in_specs=specs,
        out_specs=pl.BlockSpec((1, tm, d), lambda b, i: (b, i, 0)),
        out_shape=jax.ShapeDtypeStruct((bsz, t, d), F32),
        compiler_params=_params(2),
        name="out_proj",
    )(*args)


def _rope_tables(rows, rot_dim):
    row = jnp.broadcast_to(jnp.arange(rows)[:, None], (rows, GRID_W)).reshape(-1).astype(F32)
    col = jnp.broadcast_to(jnp.arange(GRID_W)[None, :], (rows, GRID_W)).reshape(-1).astype(F32)
    axis_dim = rot_dim // 2
    inv_freq = ROPE_THETA ** (-jnp.arange(0, axis_dim, 2, dtype=F32) / axis_dim)
    ang_r = row[:, None] * inv_freq[None, :]
    ang_c = col[:, None] * inv_freq[None, :]
    ang = jnp.concatenate([ang_r, ang_r, ang_c, ang_c], axis=-1)
    quarter = rot_dim // 4
    sign = jnp.where((jnp.arange(rot_dim) // quarter) % 2 == 0, -1.0, 1.0).astype(F32)
    cos, sin = jnp.cos(ang), jnp.sin(ang) * sign
    reps = LANES // rot_dim
    return jnp.tile(cos, (1, reps)), jnp.tile(sin, (1, reps))


def _layer_weights(w_in, w_uq, w_ukv, gq, gkv):
    wa = w_in[:, 0:1536]
    wb = w_in[:, 1536:3584]
    zeros64 = jnp.zeros((w_in.shape[0], 64), w_in.dtype)
    wc = jnp.concatenate([w_in[:, 3584:4032], zeros64,
                          w_in[:, 4032:4160],
                          w_in[:, 4160:4224], zeros64,
                          w_in[:, 4224:4736]], axis=1)
    wd = w_in[:, 4736:6272]
    uq = w_uq.reshape(C_Q_LORA, 4, C_QK_DIM)
    uq = jnp.concatenate([uq[:, :, :128].reshape(C_Q_LORA, 512),
                          uq[:, :, 128:].reshape(C_Q_LORA, 256)], axis=1)
    uq = jnp.concatenate([uq, jnp.zeros((64, 768), uq.dtype)], axis=0)
    gq_pad = jnp.concatenate([gq, jnp.zeros((64,), gq.dtype)])[None, :]
    cast = lambda a: a.astype(BF16)
    return cast(wa), cast(wb), cast(wc), cast(wd), cast(uq), cast(w_ukv), gq_pad, gkv[None, :]


def kernel(x, c, ctx, c_ctx, w_mod, b_mod, norm_g, w_in, c_q_norm_g, c_kv_norm_g, c_w_uq, c_w_ukv,
           d_q_norm_g, d_k_norm_g, a_sink, b_lambda, b_subln_g, w_out, final_norm_g):
    bsz, s, d = x.shape
    tc = ctx.shape[1]
    depth = w_mod.shape[0]
    assert d == D_MODEL and s % TM_LATENT == 0 and s % GRID_W == 0 and bsz + 1 <= MOD_ROWS
    assert tc % LANES == 0 and TM_LATENT == KV_CHUNK

    cc = jnp.concatenate([c, c_ctx[None, :], jnp.zeros((MOD_ROWS - bsz - 1, d), c.dtype)], axis=0)
    mods = _modulation(cc, w_mod, b_mod).reshape(depth, MOD_ROWS, 1, 3 * d)
    lat_row = lambda b: b
    ctx_row = lambda b: bsz

    rows = s // GRID_W
    cos_h, sin_h = _rope_tables(rows, HEAD_DIM)
    cos_b, sin_b = _rope_tables(rows, B_QK_DIM)
    one = jnp.ones((tc, LANES), F32)
    zero = jnp.zeros((tc, LANES), F32)
    tab_lat = {"h": (cos_h, sin_h), "b": (cos_b, sin_b)}
    tab_ctx = {"h": (one, zero), "b": (one, zero)}

    tq_lat = min(TQ_DENSE, s)
    for l in range(depth):
        last = l == depth - 1
        wa, wb, wc, wd, uq, ukv, gq_pad, gkv = _layer_weights(
            w_in[l], c_w_uq[l], c_w_ukv[l], c_q_norm_g[l], c_kv_norm_g[l])
        dq_g, dk_g = d_q_norm_g[l][None, :], d_k_norm_g[l][None, :]
        wo = w_out[l].astype(BF16).reshape(4, GROUP_WIDTH, d)
        lam_init = 0.8 - 0.6 * math.exp(-0.3 * l)
        diff = (b_lambda[l], b_subln_g[l][None, :], lam_init)
        sink2 = a_sink[l].reshape(2, 2)

        def project(h, tabs, tm):
            pa = _project(_proj_a_kernel, "proj_a", h, [wa], tabs["h"], tm, 4, 128, 2, 128)
            pb = _project(_proj_b_kernel, "proj_b", h, [wb], tabs["b"], tm, 8, 128, 4, 128)
            pc = _project(_proj_c_kernel, "proj_c", h, [wc, uq, ukv, gq_pad, gkv], tabs["b"], tm,
                          4, 256, 4, 256)
            pd = _project(_proj_d_kernel, "proj_d", h, [wd, dq_g, dk_g], tabs["h"], tm, 4, 128, 2, 128)
            return pa, pb, pc, pd

        def sink_rows(tq):
            return jnp.repeat(sink2, tq, axis=1).reshape(2, 1, 2 * tq)

        hx = _prenorm(x, norm_g[l][None, :], mods, l, lat_row, TM_LATENT)
        hc = _prenorm(ctx, norm_g[l][None, :], mods, l, ctx_row, tc)
        (qa, ka, vta, gza), (qb, kb, vtb, gzb), (qc, kc, vtc, gzc), (qd, kd, vtd, gzd) = project(
            hx, tab_lat, TM_LATENT)
        (qa_c, ka_c, vta_c, gza_c), (qb_c, kb_c, vtb_c, gzb_c), (qc_c, kc_c, vtc_c, gzc_c), \
            (qd_c, kd_c, vtd_c, gzd_c) = project(hc, tab_ctx, tc)

        ua = _band_attention(qa, ka_c, vta_c, ka, vta, gza, sink_rows(TQ_BAND))
        ub = _dense_attention("attn_b", qb, kb_c, vtb_c, kb, vtb, gzb, 2, tq_lat, diff=diff)
        uc = _dense_attention("attn_c", qc, kc_c, vtc_c, kc, vtc, gzc, 1, tq_lat)
        ud = _dense_attention("attn_d", qd, kd_c, vtd_c, kd, vtd, gzd, 2, tq_lat)
        if not last:
            ua_c = _dense_attention("attn_a_ctx", qa_c, ka_c, vta_c, None, None, gza_c, 2, tc,
                                    sink=sink_rows(tc))
            ub_c = _dense_attention("attn_b_ctx", qb_c, kb_c, vtb_c, None, None, gzb_c, 2, tc, diff=diff)
            uc_c = _dense_attention("attn_c_ctx", qc_c, kc_c, vtc_c, None, None, gzc_c, 1, tc)
            ud_c = _dense_attention("attn_d_ctx", qd_c, kd_c, vtd_c, None, None, gzd_c, 2, tc)
            ctx = _out_project((ua_c, ub_c, uc_c, ud_c), wo, ctx, mods, l, ctx_row, tc)
        x = _out_project((ua, ub, uc, ud), wo, x, mods, l, lat_row, TM_LATENT,
                         final_g=final_norm_g[None, :] if last else None)
    return x
```

```python
import functools
import math

import jax
import jax.numpy as jnp
from jax import lax
from jax.experimental import pallas as pl
from jax.experimental.pallas import tpu as pltpu

F32 = jnp.float32
BF16 = jnp.bfloat16

D_MODEL = 2048
HEAD_DIM = 128
GRID_W = 64
GROUP_WIDTH = 512
WINDOW = 128
ROPE_THETA = 10000.0
EPS = 1e-6
NEG_INF = -1e30
C_Q_LORA = 448
C_ROPE = 64
C_QK_DIM = 192
B_QK_DIM = 64
LOG2E = math.log2(math.e)
LANES = 128
ONES_ROWS = 16
DV_AUG = HEAD_DIM + ONES_ROWS
MOD_ROWS = 16
VMEM_LIMIT = 56 * 1024 * 1024

TM_LATENT = 512
KV_CHUNK = 512
TQ_DENSE = 512
TQ_BAND = 256
MOD_TN = 512
COL_TILE = 256


def _dotf(a, b):
    return jnp.dot(a, b, preferred_element_type=F32)


def _dot_nt(a, b):
    return lax.dot_general(a, b, (((1,), (1,)), ((), ())), preferred_element_type=F32)


def _silu(z):
    return z / (1.0 + jnp.exp(-z))


def _rms(y, inv_n):
    return y * lax.rsqrt(jnp.sum(y * y, axis=-1, keepdims=True) * inv_n + EPS)


def _rope(x, cos, sin_signed, shift):
    lane = lax.broadcasted_iota(jnp.int32, x.shape, 1)
    even = (lane & shift) == 0
    x_up = pltpu.roll(x, LANES - shift, 1)
    x_dn = pltpu.roll(x, shift, 1)
    return x * cos + jnp.where(even, x_up, x_dn) * sin_signed


def _params(n_axes):
    return pltpu.CompilerParams(dimension_semantics=("parallel",) * n_axes,
                                vmem_limit_bytes=VMEM_LIMIT)


def _mod_kernel(c_ref, w_ref, b_ref, o_ref):
    a = _silu(c_ref[...]).astype(BF16)
    o_ref[0] = _dotf(a, w_ref[0].astype(BF16)) + b_ref[0]


def _modulation(cc, w_mod, b_mod):
    depth, d, n = w_mod.shape
    return pl.pallas_call(
        _mod_kernel,
        grid=(depth, n // MOD_TN),
        in_specs=[pl.BlockSpec((MOD_ROWS, d), lambda l, j: (0, 0)),
                  pl.BlockSpec((1, d, MOD_TN), lambda l, j: (l, 0, j)),
                  pl.BlockSpec((1, 1, MOD_TN), lambda l, j: (l, 0, j))],
        out_specs=pl.BlockSpec((1, MOD_ROWS, MOD_TN), lambda l, j: (l, 0, j)),
        out_shape=jax.ShapeDtypeStruct((depth, MOD_ROWS, n), F32),
        compiler_params=_params(2),
        name="modulation",
    )(cc, w_mod, b_mod.reshape(depth, 1, n))


def _prenorm_kernel(x_ref, g_ref, shift_ref, scale_ref, h_ref):
    x = x_ref[0]
    y = _rms(x, 1.0 / D_MODEL) * g_ref[...]
    h_ref[0] = (y * (1.0 + scale_ref[0, 0]) + shift_ref[0, 0]).astype(BF16)


def _mod_spec(layer, row_fn, which):
    return pl.BlockSpec((1, 1, 1, D_MODEL), lambda b, i: (layer, row_fn(b), 0, which))


def _prenorm(x, g, mods, layer, row_fn, tm):
    bsz, t, d = x.shape
    return pl.pallas_call(
        _prenorm_kernel,
        grid=(bsz, t // tm),
        in_specs=[pl.BlockSpec((1, tm, d), lambda b, i: (b, i, 0)),
                  pl.BlockSpec((1, d), lambda b, i: (0, 0)),
                  _mod_spec(layer, row_fn, 0), _mod_spec(layer, row_fn, 1)],
        out_specs=pl.BlockSpec((1, tm, d), lambda b, i: (b, i, 0)),
        out_shape=jax.ShapeDtypeStruct((bsz, t, d), BF16),
        compiler_params=_params(2),
        name="prenorm",
    )(x, g, mods, mods)


def _store_vt(vt_ref, head, y):
    vt_ref[0, head, 0, 0:HEAD_DIM, :] = y.T.astype(BF16)
    vt_ref[0, head, 0, HEAD_DIM:DV_AUG, :] = jnp.ones((ONES_ROWS, y.shape[0]), BF16)


def _proj_a_kernel(h_ref, w_ref, cos_ref, sin_ref, q_ref, k_ref, vt_ref, gz_ref):
    h = h_ref[0]
    cos, sin = cos_ref[...], sin_ref[...]
    scale = LOG2E / math.sqrt(HEAD_DIM)
    yq = _dotf(h, w_ref[:, 0:512])
    for hh in range(4):
        q_ref[0, hh] = (_rope(yq[:, hh * 128:(hh + 1) * 128], cos, sin, 32) * scale).astype(BF16)
    ykv = _dotf(h, w_ref[:, 512:1024])
    for hh in range(2):
        k_ref[0, hh] = _rope(ykv[:, hh * 128:(hh + 1) * 128], cos, sin, 32).astype(BF16)
        _store_vt(vt_ref, hh, ykv[:, 256 + hh * 128:256 + (hh + 1) * 128])
    gz_ref[0] = _silu(_dotf(h, w_ref[:, 1024:1536])).astype(BF16)


def _proj_b_kernel(h_ref, w_ref, cos_ref, sin_ref, q_ref, k_ref, vt_ref, gz_ref):
    h = h_ref[0]
    cos, sin = cos_ref[...], sin_ref[...]
    scale = LOG2E / math.sqrt(B_QK_DIM)
    yq = _dotf(h, w_ref[:, 0:512])
    lane = lax.broadcasted_iota(jnp.int32, (h.shape[0], LANES), 1)
    first = lane < B_QK_DIM
    for hh in range(4):
        y = _rope(yq[:, hh * 128:(hh + 1) * 128], cos, sin, 16) * scale
        q_ref[0, 2 * hh] = jnp.where(first, y, 0.0).astype(BF16)
        q_ref[0, 2 * hh + 1] = jnp.where(first, 0.0, y).astype(BF16)
    yk = _dotf(h, w_ref[:, 512:1024])
    for hh in range(4):
        k_ref[0, hh] = _rope(yk[:, hh * 128:(hh + 1) * 128], cos, sin, 16).astype(BF16)
    yv = _dotf(h, w_ref[:, 1024:1536])
    for hh in range(4):
        _store_vt(vt_ref, hh, yv[:, hh * 128:(hh + 1) * 128])
    gz_ref[0] = _silu(_dotf(h, w_ref[:, 1536:2048])).astype(BF16)


def _proj_c_kernel(h_ref, w_ref, wuq_ref, wukv_ref, gq_ref, gkv_ref, cos_ref, sin_ref,
                   q_ref, k_ref, vt_ref, gz_ref):
    h = h_ref[0]
    cos, sin = cos_ref[...], sin_ref[...]
    scale = LOG2E / math.sqrt(C_QK_DIM)
    ycq = _dotf(h, w_ref[:, 0:512])
    cqn = (_rms(ycq, 1.0 / C_Q_LORA) * gq_ref[...]).astype(BF16)
    yq = _dotf(cqn, wuq_ref[...])
    yckv = _dotf(h, w_ref[:, 512:768])
    ckvn = (_rms(yckv[:, 0:128], 1.0 / 128) * gkv_ref[...]).astype(BF16)
    ykv = _dotf(ckvn, wukv_ref[...])
    kr_even = _rope(yckv[:, 128:256], cos, sin, 16)
    kr_odd = pltpu.roll(kr_even, C_ROPE, 1)
    qr = [_rope(yq[:, 512 + p * 128:512 + (p + 1) * 128], cos, sin, 16) * scale for p in range(2)]
    for hh in range(4):
        q_ref[0, hh, :, 0:128] = (yq[:, hh * 128:(hh + 1) * 128] * scale).astype(BF16)
        q_ref[0, hh, :, 128:256] = qr[hh // 2].astype(BF16)
        k_ref[0, hh, :, 0:128] = ykv[:, hh * 256:hh * 256 + 128].astype(BF16)
        k_ref[0, hh, :, 128:256] = (kr_even if hh % 2 == 0 else kr_odd).astype(BF16)
        _store_vt(vt_ref, hh, ykv[:, hh * 256 + 128:(hh + 1) * 256])
    gz_ref[0] = _silu(_dotf(h, w_ref[:, 768:1280])).astype(BF16)


def _proj_d_kernel(h_ref, w_ref, gq_ref, gk_ref, cos_ref, sin_ref, q_ref, k_ref, vt_ref, gz_ref):
    h = h_ref[0]
    cos, sin = cos_ref[...], sin_ref[...]
    scale = LOG2E / math.sqrt(HEAD_DIM)
    yq = _dotf(h, w_ref[:, 0:512])
    for hh in range(4):
        yn = _rms(yq[:, hh * 128:(hh + 1) * 128], 1.0 / HEAD_DIM) * gq_ref[...]
        q_ref[0, hh] = (_rope(yn, cos, sin, 32) * scale).astype(BF16)
    ykv = _dotf(h, w_ref[:, 512:1024])
    for hh in range(2):
        yn = _rms(ykv[:, hh * 128:(hh + 1) * 128], 1.0 / HEAD_DIM) * gk_ref[...]
        k_ref[0, hh] = _rope(yn, cos, sin, 32).astype(BF16)
        _store_vt(vt_ref, hh, ykv[:, 256 + hh * 128:256 + (hh + 1) * 128])
    gz_ref[0] = _silu(_dotf(h, w_ref[:, 1024:1536])).astype(BF16)


def _full_spec(a):
    nd = a.ndim
    return pl.BlockSpec(a.shape, lambda b, i: (0,) * nd)


def _project(kern, name, h, weights, tables, tm, n_q, dq, n_kv, dk):
    bsz, t, d = h.shape
    nt = t // tm
    tab_specs = [pl.BlockSpec((tm, LANES), (lambda b, i: (i, 0)) if tab.shape[0] == t
                              else (lambda b, i: (0, 0))) for tab in tables]
    return pl.pallas_call(
        kern,
        grid=(bsz, nt),
        in_specs=[pl.BlockSpec((1, tm, d), lambda b, i: (b, i, 0))]
        + [_full_spec(w) for w in weights] + tab_specs,
        out_specs=[pl.BlockSpec((1, n_q, tm, dq), lambda b, i: (b, 0, i, 0)),
                   pl.BlockSpec((1, n_kv, tm, dk), lambda b, i: (b, 0, i, 0)),
                   pl.BlockSpec((1, n_kv, 1, DV_AUG, tm), lambda b, i: (b, 0, i, 0, 0)),
                   pl.BlockSpec((1, tm, GROUP_WIDTH), lambda b, i: (b, i, 0))],
        out_shape=[jax.ShapeDtypeStruct((bsz, n_q, t, dq), BF16),
                   jax.ShapeDtypeStruct((bsz, n_kv, t, dk), BF16),
                   jax.ShapeDtypeStruct((bsz, n_kv, nt, DV_AUG, tm), BF16),
                   jax.ShapeDtypeStruct((bsz, t, GROUP_WIDTH), BF16)],
        compiler_params=_params(2),
        name=name,
    )(h, *weights, *tables)


def _finish_plain(acc, l, gate_ref, out_ref, n_sets, tq):
    o_t = acc * (1.0 / l)
    for g in range(n_sets):
        o = o_t[:, g * tq:(g + 1) * tq].T
        gate = gate_ref[0, :, g * 128:(g + 1) * 128].astype(F32)
        out_ref[0, :, g * 128:(g + 1) * 128] = (o * gate).astype(BF16)


def _finish_diff(acc, l, gate_ref, out_ref, lam_ref, subg_ref, lam_init, tq):
    o_t = acc * (1.0 / l)
    lp = lam_ref[...]
    s1 = jnp.sum(lp[0:1] * lp[1:2], axis=1, keepdims=True)
    s2 = jnp.sum(lp[2:3] * lp[3:4], axis=1, keepdims=True)
    lam = jnp.exp(s1) - jnp.exp(s2) + lam_init
    d = o_t[:, 0:tq] - lam * o_t[:, tq:2 * tq]
    dn = d * lax.rsqrt(jnp.mean(d * d, axis=0, keepdims=True) + EPS)
    o = dn.T * subg_ref[...] * (1.0 - lam_init)
    out_ref[0] = (o * gate_ref[0].astype(F32)).astype(BF16)


def _dense_attn_kernel(*refs, n_sets, tq, n_chunks, has_sink, diff, lam_init):
    refs = list(refs)
    q_ref, kc_ref, vtc_ref = refs[:3]
    pos = 3
    if n_chunks:
        kl_ref, vtl_ref = refs[pos:pos + 2]
        pos += 2
    gate_ref = refs[pos]
    pos += 1
    if has_sink:
        sink_ref = refs[pos]
        pos += 1
    if diff:
        lam_ref, subg_ref = refs[pos:pos + 2]
        pos += 2
    out_ref, acc_ref = refs[pos:pos + 2]
    s_refs = refs[pos + 2:pos + 4]

    n = n_sets * tq
    n_tiles = n // COL_TILE
    per_set = tq // COL_TILE
    tc = kc_ref.shape[2]
    acc_ref[0:HEAD_DIM, :] = jnp.zeros((HEAD_DIM, n), F32)
    if has_sink:
        m0 = sink_ref[0] * LOG2E
        acc_ref[HEAD_DIM:DV_AUG, :] = jnp.ones((ONES_ROWS, n), F32)
    else:
        m0 = jnp.full((1, n), NEG_INF, F32)
        acc_ref[HEAD_DIM:DV_AUG, :] = jnp.zeros((ONES_ROWS, n), F32)

    def cols(c):
        return slice(c * COL_TILE, (c + 1) * COL_TILE)

    def scores(k, c, s_ref, rows):
        qt = q_ref[0, c // per_set, (c % per_set) * COL_TILE:(c % per_set + 1) * COL_TILE, :]
        s = _dot_nt(k, qt)
        s_ref[0:rows, cols(c)] = s
        return jnp.max(s, axis=0, keepdims=True)

    def softmax_pv(s_ref, rows, cmax, vt, m, c):
        m_new = jnp.maximum(m, cmax)
        alpha = jnp.exp2(m - m_new)
        p = jnp.exp2(s_ref[0:rows, cols(c)] - m_new)
        pv = _dotf(vt, p.astype(BF16))
        acc_ref[:, cols(c)] = alpha * acc_ref[:, cols(c)] + pv
        return m_new

    def stage(k_next, s_next, s_cur, rows_cur, cmax_cur, vt_cur, m):
        ms, cm = [], []
        for c in range(n_tiles):
            if k_next is not None:
                cm.append(scores(k_next, c, s_next, KV_CHUNK))
            if s_cur is not None:
                ms.append(softmax_pv(s_cur, rows_cur, cmax_cur[:, cols(c)], vt_cur, m[:, cols(c)], c))
        cat = lambda xs: jnp.concatenate(xs, axis=1) if xs else None
        return cat(cm), cat(ms)

    s_a, s_b = s_refs
    k_ctx = kc_ref[0, 0]
    cmax = jnp.concatenate([scores(k_ctx, c, s_a, tc) for c in range(n_tiles)], axis=1)
    if not n_chunks:
        _, m = stage(None, None, s_a, tc, cmax, vtc_ref[0, 0, 0], m0)
    else:
        def k_chunk(j):
            start = pl.multiple_of(j * KV_CHUNK, KV_CHUNK)
            return kl_ref[0, 0, pl.ds(start, KV_CHUNK), :]

        cmax, m = stage(k_chunk(0), s_b, s_a, tc, cmax, vtc_ref[0, 0, 0], m0)

        def body(i, carry):
            cmax, m = carry
            j = 2 * i
            cmax, m = stage(k_chunk(j + 1), s_a, s_b, KV_CHUNK, cmax, vtl_ref[0, 0, j], m)
            return stage(k_chunk(j + 2), s_b, s_a, KV_CHUNK, cmax, vtl_ref[0, 0, j + 1], m)

        cmax, m = lax.fori_loop(0, n_chunks // 2 - 1, body, (cmax, m))
        j = n_chunks - 2
        cmax, m = stage(k_chunk(j + 1), s_a, s_b, KV_CHUNK, cmax, vtl_ref[0, 0, j], m)
        _, m = stage(None, None, s_a, KV_CHUNK, cmax, vtl_ref[0, 0, j + 1], m)

    acc = acc_ref[0:HEAD_DIM, :]
    l = acc_ref[HEAD_DIM:HEAD_DIM + 1, :]
    if diff:
        _finish_diff(acc, l, gate_ref, out_ref, lam_ref, subg_ref, lam_init, tq)
    else:
        _finish_plain(acc, l, gate_ref, out_ref, n_sets, tq)


def _dense_attention(name, q, kc, vtc, kl, vtl, gate, n_sets, tq, sink=None, diff=None):
    bsz, _, t, dk = q.shape
    hkv, tc = kc.shape[1], kc.shape[2]
    dv = vtc.shape[3]
    n_chunks = 0 if kl is None else kl.shape[2] // KV_CHUNK
    assert n_chunks % 2 == 0 and tq % COL_TILE == 0
    out_w = LANES if diff else n_sets * LANES
    n = n_sets * tq
    args = [q, kc, vtc]
    specs = [pl.BlockSpec((1, n_sets, tq, dk), lambda b, h, i: (b, h, i, 0)),
             pl.BlockSpec((1, 1, tc, dk), lambda b, h, i: (b, h, 0, 0)),
             pl.BlockSpec((1, 1, 1, dv, tc), lambda b, h, i: (b, h, 0, 0, 0))]
    if n_chunks:
        args += [kl, vtl]
        specs += [pl.BlockSpec((1, 1, kl.shape[2], dk), lambda b, h, i: (b, h, 0, 0)),
                  pl.BlockSpec((1, 1, n_chunks, dv, KV_CHUNK), lambda b, h, i: (b, h, 0, 0, 0))]
    args.append(gate)
    specs.append(pl.BlockSpec((1, tq, out_w), lambda b, h, i: (b, i, h)))
    if sink is not None:
        args.append(sink)
        specs.append(pl.BlockSpec((1, 1, n), lambda b, h, i: (h, 0, 0)))
    lam_init = 0.0
    if diff is not None:
        lam_par, subg, lam_init = diff
        args += [lam_par, subg]
        specs += [pl.BlockSpec(lam_par.shape, lambda b, h, i: (0, 0)),
                  pl.BlockSpec(subg.shape, lambda b, h, i: (0, 0))]
    kern = functools.partial(_dense_attn_kernel, n_sets=n_sets, tq=tq, n_chunks=n_chunks,
                             has_sink=sink is not None, diff=diff is not None, lam_init=lam_init)
    return pl.pallas_call(
        kern,
        grid=(bsz, hkv, t // tq),
        in_specs=specs,
        out_specs=pl.BlockSpec((1, tq, out_w), lambda b, h, i: (b, i, h)),
        out_shape=jax.ShapeDtypeStruct((bsz, t, GROUP_WIDTH), BF16),
        scratch_shapes=[pltpu.VMEM((dv, n), F32)] + [pltpu.VMEM((max(KV_CHUNK, tc), n), F32)] * 2,
        compiler_params=_params(3),
        name=name,
    )(*args)


def _band_attn_kernel(q_ref, kc_ref, vtc_ref, kp_ref, km_ref, kn_ref, vp_ref, vm_ref, vn_ref,
                      gate_ref, sink_ref, out_ref, *, seq_len):
    tq = TQ_BAND
    n = 2 * tq
    q = q_ref[0].reshape(n, HEAD_DIM)
    q_start = pl.program_id(2) * tq
    q_rel = lax.broadcasted_iota(jnp.int32, (1, n), 1) & (tq - 1)

    def band_scores(k_ref, rel0):
        rows = k_ref.shape[2]
        s = _dot_nt(k_ref[0, 0], q)
        k_rel = rel0 + lax.broadcasted_iota(jnp.int32, (rows, 1), 0)
        k_pos = q_start + k_rel
        valid = (jnp.abs(q_rel - k_rel) <= WINDOW) & (k_pos >= 0) & (k_pos < seq_len)
        return jnp.where(valid, s, NEG_INF)

    scores = [_dot_nt(kc_ref[0, 0], q), band_scores(kp_ref, -WINDOW), band_scores(km_ref, 0),
              band_scores(kn_ref, tq)]
    vts = [vtc_ref[0, 0, 0], vp_ref[0, 0, 0], vm_ref[0, 0, 0], vn_ref[0, 0, 0]]
    sink = sink_ref[0] * LOG2E
    m = sink
    for s in scores:
        m = jnp.maximum(m, jnp.max(s, axis=0, keepdims=True))
    acc = None
    for s, vt in zip(scores, vts):
        pv = _dotf(vt, jnp.exp2(s - m).astype(BF16))
        acc = pv if acc is None else acc + pv
    l = acc[HEAD_DIM:HEAD_DIM + 1, :] + jnp.exp2(sink - m)
    _finish_plain(acc[0:HEAD_DIM, :], l, gate_ref, out_ref, 2, tq)


def _band_attention(q, kc, vtc, kl, vtl, gate, sink):
    bsz, _, s, dk = q.shape
    tq = TQ_BAND
    tc = kc.shape[2]
    n_blk = s // WINDOW
    per_chunk = TM_LATENT // WINDOW

    def prev_blk(i):
        return jnp.maximum(2 * i - 1, 0)

    def next_blk(i):
        return jnp.minimum(2 * i + 2, n_blk - 1)

    half = TM_LATENT // tq
    specs = [
        pl.BlockSpec((1, 2, tq, dk), lambda b, h, i: (b, h, i, 0)),
        pl.BlockSpec((1, 1, tc, dk), lambda b, h, i: (b, h, 0, 0)),
        pl.BlockSpec((1, 1, 1, DV_AUG, tc), lambda b, h, i: (b, h, 0, 0, 0)),
        pl.BlockSpec((1, 1, WINDOW, dk), lambda b, h, i: (b, h, prev_blk(i), 0)),
        pl.BlockSpec((1, 1, tq, dk), lambda b, h, i: (b, h, i, 0)),
        pl.BlockSpec((1, 1, WINDOW, dk), lambda b, h, i: (b, h, next_blk(i), 0)),
        pl.BlockSpec((1, 1, 1, DV_AUG, WINDOW),
                     lambda b, h, i: (b, h, prev_blk(i) // per_chunk, 0, prev_blk(i) % per_chunk)),
        pl.BlockSpec((1, 1, 1, DV_AUG, tq), lambda b, h, i: (b, h, i // half, 0, i % half)),
        pl.BlockSpec((1, 1, 1, DV_AUG, WINDOW),
                     lambda b, h, i: (b, h, next_blk(i) // per_chunk, 0, next_blk(i) % per_chunk)),
        pl.BlockSpec((1, tq, 2 * LANES), lambda b, h, i: (b, i, h)),
        pl.BlockSpec((1, 1, 2 * tq), lambda b, h, i: (h, 0, 0)),
    ]
    return pl.pallas_call(
        functools.partial(_band_attn_kernel, seq_len=s),
        grid=(bsz, 2, s // tq),
        in_specs=specs,
        out_specs=pl.BlockSpec((1, tq, 2 * LANES), lambda b, h, i: (b, i, h)),
        out_shape=jax.ShapeDtypeStruct((bsz, s, GROUP_WIDTH), BF16),
        compiler_params=_params(3),
        name="attn_a_band",
    )(q, kc, vtc, kl, kl, kl, vtl, vtl, vtl, gate, sink)


def _out_kernel(*refs, final):
    ua, ub, uc, ud, w_ref, x_ref, gate_ref = refs[:7]
    out_ref = refs[-1]
    acc = (_dotf(ua[0], w_ref[0]) + _dotf(ub[0], w_ref[1])
           + _dotf(uc[0], w_ref[2]) + _dotf(ud[0], w_ref[3]))
    xn = x_ref[0] + gate_ref[0, 0] * acc
    if final:
        xn = _rms(xn, 1.0 / D_MODEL) * refs[7][...]
    out_ref[0] = xn


def _out_project(us, w_out, x, mods, layer, row_fn, tm, final_g=None):
    bsz, t, d = x.shape
    u_spec = pl.BlockSpec((1, tm, GROUP_WIDTH), lambda b, i: (b, i, 0))
    specs = [u_spec] * 4 + [_full_spec(w_out), pl.BlockSpec((1, tm, d), lambda b, i: (b, i, 0)),
                            _mod_spec(layer, row_fn, 2)]
    args = list(us) + [w_out, x, mods]
    if final_g is not None:
        specs.append(pl.BlockSpec((1, d), lambda b, i: (0, 0)))
        args.append(final_g)
    return pl.pallas_call(
        functools.partial(_out_kernel, final=final_g is not None),
        grid=(bsz, t // tm),
        in_specs=specs,
        out_specs=pl.BlockSpec((1, tm, d), lambda b, i: (b, i, 0)),
        out_shape=jax.ShapeDtypeStruct((bsz, t, d), F32),
        compiler_params=_params(2),
        name="out_proj",
    )(*args)


def _rope_tables(rows, rot_dim):
    row = jnp.broadcast_to(jnp.arange(rows)[:, None], (rows, GRID_W)).reshape(-1).astype(F32)
    col = jnp.broadcast_to(jnp.arange(GRID_W)[None, :], (rows, GRID_W)).reshape(-1).astype(F32)
    axis_dim = rot_dim // 2
    inv_freq = ROPE_THETA ** (-jnp.arange(0, axis_dim, 2, dtype=F32) / axis_dim)
    ang_r = row[:, None] * inv_freq[None, :]
    ang_c = col[:, None] * inv_freq[None, :]
    ang = jnp.concatenate([ang_r, ang_r, ang_c, ang_c], axis=-1)
    quarter = rot_dim // 4
    sign = jnp.where((jnp.arange(rot_dim) // quarter) % 2 == 0, -1.0, 1.0).astype(F32)
    cos, sin = jnp.cos(ang), jnp.sin(ang) * sign
    reps = LANES // rot_dim
    return jnp.tile(cos, (1, reps)), jnp.tile(sin, (1, reps))


def _layer_weights(w_in, w_uq, w_ukv, gq, gkv):
    wa = w_in[:, 0:1536]
    wb = w_in[:, 1536:3584]
    zeros64 = jnp.zeros((w_in.shape[0], 64), w_in.dtype)
    wc = jnp.concatenate([w_in[:, 3584:4032], zeros64,
                          w_in[:, 4032:4160],
                          w_in[:, 4160:4224], zeros64,
                          w_in[:, 4224:4736]], axis=1)
    wd = w_in[:, 4736:6272]
    uq = w_uq.reshape(C_Q_LORA, 4, C_QK_DIM)
    uq = jnp.concatenate([uq[:, :, :128].reshape(C_Q_LORA, 512),
                          uq[:, :, 128:].reshape(C_Q_LORA, 256)], axis=1)
    uq = jnp.concatenate([uq, jnp.zeros((64, 768), uq.dtype)], axis=0)
    gq_pad = jnp.concatenate([gq, jnp.zeros((64,), gq.dtype)])[None, :]
    cast = lambda a: a.astype(BF16)
    return cast(wa), cast(wb), cast(wc), cast(wd), cast(uq), cast(w_ukv), gq_pad, gkv[None, :]


def kernel(x, c, ctx, c_ctx, w_mod, b_mod, norm_g, w_in, c_q_norm_g, c_kv_norm_g, c_w_uq, c_w_ukv,
           d_q_norm_g, d_k_norm_g, a_sink, b_lambda, b_subln_g, w_out, final_norm_g):
    bsz, s, d = x.shape
    tc = ctx.shape[1]
    depth = w_mod.shape[0]
    assert d == D_MODEL and s % TM_LATENT == 0 and s % GRID_W == 0 and bsz + 1 <= MOD_ROWS
    assert tc % LANES == 0 and TM_LATENT == KV_CHUNK

    cc = jnp.concatenate([c, c_ctx[None, :], jnp.zeros((MOD_ROWS - bsz - 1, d), c.dtype)], axis=0)
    mods = _modulation(cc, w_mod, b_mod).reshape(depth, MOD_ROWS, 1, 3 * d)
    lat_row = lambda b: b
    ctx_row = lambda b: bsz

    rows = s // GRID_W
    cos_h, sin_h = _rope_tables(rows, HEAD_DIM)
    cos_b, sin_b = _rope_tables(rows, B_QK_DIM)
    one = jnp.ones((tc, LANES), F32)
    zero = jnp.zeros((tc, LANES), F32)
    tab_lat = {"h": (cos_h, sin_h), "b": (cos_b, sin_b)}
    tab_ctx = {"h": (one, zero), "b": (one, zero)}

    tq_lat = min(TQ_DENSE, s)
    for l in range(depth):
        last = l == depth - 1
        wa, wb, wc, wd, uq, ukv, gq_pad, gkv = _layer_weights(
            w_in[l], c_w_uq[l], c_w_ukv[l], c_q_norm_g[l], c_kv_norm_g[l])
        dq_g, dk_g = d_q_norm_g[l][None, :], d_k_norm_g[l][None, :]
        wo = w_out[l].astype(BF16).reshape(4, GROUP_WIDTH, d)
        lam_init = 0.8 - 0.6 * math.exp(-0.3 * l)
        diff = (b_lambda[l], b_subln_g[l][None, :], lam_init)
        sink2 = a_sink[l].reshape(2, 2)

        def project(h, tabs, tm):
            pa = _project(_proj_a_kernel, "proj_a", h, [wa], tabs["h"], tm, 4, 128, 2, 128)
            pb = _project(_proj_b_kernel, "proj_b", h, [wb], tabs["b"], tm, 8, 128, 4, 128)
            pc = _project(_proj_c_kernel, "proj_c", h, [wc, uq, ukv, gq_pad, gkv], tabs["b"], tm,
                          4, 256, 4, 256)
            pd = _project(_proj_d_kernel, "proj_d", h, [wd, dq_g, dk_g], tabs["h"], tm, 4, 128, 2, 128)
            return pa, pb, pc, pd

        def sink_rows(tq):
            return jnp.repeat(sink2, tq, axis=1).reshape(2, 1, 2 * tq)

        hx = _prenorm(x, norm_g[l][None, :], mods, l, lat_row, TM_LATENT)
        hc = _prenorm(ctx, norm_g[l][None, :], mods, l, ctx_row, tc)
        (qa, ka, vta, gza), (qb, kb, vtb, gzb), (qc, kc, vtc, gzc), (qd, kd, vtd, gzd) = project(
            hx, tab_lat, TM_LATENT)
        (qa_c, ka_c, vta_c, gza_c), (qb_c, kb_c, vtb_c, gzb_c), (qc_c, kc_c, vtc_c, gzc_c), \
            (qd_c, kd_c, vtd_c, gzd_c) = project(hc, tab_ctx, tc)

        ua = _band_attention(qa, ka_c, vta_c, ka, vta, gza, sink_rows(TQ_BAND))
        ub = _dense_attention("attn_b", qb, kb_c, vtb_c, kb, vtb, gzb, 2, tq_lat, diff=diff)
        uc = _dense_attention("attn_c", qc, kc_c, vtc_c, kc, vtc, gzc, 1, tq_lat)
        ud = _dense_attention("attn_d", qd, kd_c, vtd_c, kd, vtd, gzd, 2, tq_lat)
        if not last:
            ua_c = _dense_attention("attn_a_ctx", qa_c, ka_c, vta_c, None, None, gza_c, 2, tc,
                                    sink=sink_rows(tc))
            ub_c = _dense_attention("attn_b_ctx", qb_c, kb_c, vtb_c, None, None, gzb_c, 2, tc, diff=diff)
            uc_c = _dense_attention("attn_c_ctx", qc_c, kc_c, vtc_c, None, None, gzc_c, 1, tc)
            ud_c = _dense_attention("attn_d_ctx", qd_c, kd_c, vtd_c, None, None, gzd_c, 2, tc)
            ctx = _out_project((ua_c, ub_c, uc_c, ud_c), wo, ctx, mods, l, ctx_row, tc)
        x = _out_project((ua, ub, uc, ud), wo, x, mods, l, lat_row, TM_LATENT,
                         final_g=final_norm_g[None, :] if last else None)
    return x
```

```python
import functools
import math

import jax
import jax.numpy as jnp
from jax import lax
from jax.experimental import pallas as pl
from jax.experimental.pallas import tpu as pltpu

F32 = jnp.float32
BF16 = jnp.bfloat16

D_MODEL = 2048
HEAD_DIM = 128
GRID_W = 64
GROUP_WIDTH = 512
WINDOW = 128
ROPE_THETA = 10000.0
EPS = 1e-6
NEG_INF = -1e30
C_Q_LORA = 448
C_ROPE = 64
C_QK_DIM = 192
B_QK_DIM = 64
LOG2E = math.log2(math.e)
LANES = 128
ONES_ROWS = 16
DV_AUG = HEAD_DIM + ONES_ROWS
MOD_ROWS = 16
VMEM_LIMIT = 56 * 1024 * 1024

TM_LATENT = 512
KV_CHUNK = 512
N_DENSE = 2048
TQ_BAND = 1024
MOD_TN = 512
COL_TILE = 256


def _dotf(a, b):
    return jnp.dot(a, b, preferred_element_type=F32)


def _dot_nt(a, b):
    return lax.dot_general(a, b, (((1,), (1,)), ((), ())), preferred_element_type=F32)


def _silu(z):
    return z / (1.0 + jnp.exp(-z))


def _rms(y, inv_n):
    return y * lax.rsqrt(jnp.sum(y * y, axis=-1, keepdims=True) * inv_n + EPS)


def _rope(x, cos, sin_signed, shift):
    lane = lax.broadcasted_iota(jnp.int32, x.shape, 1)
    even = (lane & shift) == 0
    x_up = pltpu.roll(x, LANES - shift, 1)
    x_dn = pltpu.roll(x, shift, 1)
    return x * cos + jnp.where(even, x_up, x_dn) * sin_signed


def _params(n_axes):
    return pltpu.CompilerParams(dimension_semantics=("parallel",) * n_axes,
                                vmem_limit_bytes=VMEM_LIMIT)


def _mod_kernel(c_ref, w_ref, b_ref, o_ref):
    a = _silu(c_ref[...]).astype(BF16)
    o_ref[0] = _dotf(a, w_ref[0].astype(BF16)) + b_ref[0]


def _modulation(cc, w_mod, b_mod):
    depth, d, n = w_mod.shape
    return pl.pallas_call(
        _mod_kernel,
        grid=(depth, n // MOD_TN),
        in_specs=[pl.BlockSpec((MOD_ROWS, d), lambda l, j: (0, 0)),
                  pl.BlockSpec((1, d, MOD_TN), lambda l, j: (l, 0, j)),
                  pl.BlockSpec((1, 1, MOD_TN), lambda l, j: (l, 0, j))],
        out_specs=pl.BlockSpec((1, MOD_ROWS, MOD_TN), lambda l, j: (l, 0, j)),
        out_shape=jax.ShapeDtypeStruct((depth, MOD_ROWS, n), F32),
        compiler_params=_params(2),
        name="modulation",
    )(cc, w_mod, b_mod.reshape(depth, 1, n))


def _prenorm(x_ref, g_ref, shift_ref, scale_ref):
    y = _rms(x_ref[0], 1.0 / D_MODEL) * g_ref[...]
    return (y * (1.0 + scale_ref[0, 0]) + shift_ref[0, 0]).astype(BF16)


def _mod_spec(layer, row_fn, which):
    return pl.BlockSpec((1, 1, 1, D_MODEL), lambda b, i: (layer, row_fn(b), 0, which))


def _store_vt(vt_ref, head, y):
    vt_ref[0, head, 0, 0:HEAD_DIM, :] = y.T.astype(BF16)
    vt_ref[0, head, 0, HEAD_DIM:DV_AUG, :] = jnp.ones((ONES_ROWS, y.shape[0]), BF16)


def _proj_a_kernel(x_ref, g_ref, shift_ref, scale_ref, w_ref, cos_ref, sin_ref,
                   q_ref, k_ref, vt_ref, gz_ref):
    h = _prenorm(x_ref, g_ref, shift_ref, scale_ref)
    cos, sin = cos_ref[...], sin_ref[...]
    scale = LOG2E / math.sqrt(HEAD_DIM)
    yq = _dotf(h, w_ref[:, 0:512])
    for hh in range(4):
        q_ref[0, hh] = (_rope(yq[:, hh * 128:(hh + 1) * 128], cos, sin, 32) * scale).astype(BF16)
    ykv = _dotf(h, w_ref[:, 512:1024])
    for hh in range(2):
        k_ref[0, hh] = _rope(ykv[:, hh * 128:(hh + 1) * 128], cos, sin, 32).astype(BF16)
        _store_vt(vt_ref, hh, ykv[:, 256 + hh * 128:256 + (hh + 1) * 128])
    gz_ref[0] = _silu(_dotf(h, w_ref[:, 1024:1536])).astype(BF16)


def _proj_b_kernel(x_ref, g_ref, shift_ref, scale_ref, w_ref, cos_ref, sin_ref,
                   q_ref, k_ref, vt_ref, gz_ref):
    h = _prenorm(x_ref, g_ref, shift_ref, scale_ref)
    cos, sin = cos_ref[...], sin_ref[...]
    scale = LOG2E / math.sqrt(B_QK_DIM)
    yq = _dotf(h, w_ref[:, 0:512])
    lane = lax.broadcasted_iota(jnp.int32, (h.shape[0], LANES), 1)
    first = lane < B_QK_DIM
    for hh in range(4):
        y = _rope(yq[:, hh * 128:(hh + 1) * 128], cos, sin, 16) * scale
        q_ref[0, 2 * hh] = jnp.where(first, y, 0.0).astype(BF16)
        q_ref[0, 2 * hh + 1] = jnp.where(first, 0.0, y).astype(BF16)
    yk = _dotf(h, w_ref[:, 512:1024])
    for hh in range(4):
        k_ref[0, hh] = _rope(yk[:, hh * 128:(hh + 1) * 128], cos, sin, 16).astype(BF16)
    yv = _dotf(h, w_ref[:, 1024:1536])
    for hh in range(4):
        _store_vt(vt_ref, hh, yv[:, hh * 128:(hh + 1) * 128])
    gz_ref[0] = _silu(_dotf(h, w_ref[:, 1536:2048])).astype(BF16)


def _proj_c_kernel(x_ref, g_ref, shift_ref, scale_ref, w_ref, wuq_ref, wukv_ref, gq_ref, gkv_ref,
                   cos_ref, sin_ref, q_ref, k_ref, vt_ref, gz_ref):
    h = _prenorm(x_ref, g_ref, shift_ref, scale_ref)
    cos, sin = cos_ref[...], sin_ref[...]
    scale = LOG2E / math.sqrt(C_QK_DIM)
    ycq = _dotf(h, w_ref[:, 0:512])
    cqn = (_rms(ycq, 1.0 / C_Q_LORA) * gq_ref[...]).astype(BF16)
    yq = _dotf(cqn, wuq_ref[...])
    yckv = _dotf(h, w_ref[:, 512:768])
    ckvn = (_rms(yckv[:, 0:128], 1.0 / 128) * gkv_ref[...]).astype(BF16)
    ykv = _dotf(ckvn, wukv_ref[...])
    kr_even = _rope(yckv[:, 128:256], cos, sin, 16)
    kr_odd = pltpu.roll(kr_even, C_ROPE, 1)
    qr = [_rope(yq[:, 512 + p * 128:512 + (p + 1) * 128], cos, sin, 16) * scale for p in range(2)]
    for hh in range(4):
        q_ref[0, hh, :, 0:128] = (yq[:, hh * 128:(hh + 1) * 128] * scale).astype(BF16)
        q_ref[0, hh, :, 128:256] = qr[hh // 2].astype(BF16)
        k_ref[0, hh, :, 0:128] = ykv[:, hh * 256:hh * 256 + 128].astype(BF16)
        k_ref[0, hh, :, 128:256] = (kr_even if hh % 2 == 0 else kr_odd).astype(BF16)
        _store_vt(vt_ref, hh, ykv[:, hh * 256 + 128:(hh + 1) * 256])
    gz_ref[0] = _silu(_dotf(h, w_ref[:, 768:1280])).astype(BF16)


def _proj_d_kernel(x_ref, g_ref, shift_ref, scale_ref, w_ref, gq_ref, gk_ref, cos_ref, sin_ref,
                   q_ref, k_ref, vt_ref, gz_ref):
    h = _prenorm(x_ref, g_ref, shift_ref, scale_ref)
    cos, sin = cos_ref[...], sin_ref[...]
    scale = LOG2E / math.sqrt(HEAD_DIM)
    yq = _dotf(h, w_ref[:, 0:512])
    for hh in range(4):
        yn = _rms(yq[:, hh * 128:(hh + 1) * 128], 1.0 / HEAD_DIM) * gq_ref[...]
        q_ref[0, hh] = (_rope(yn, cos, sin, 32) * scale).astype(BF16)
    ykv = _dotf(h, w_ref[:, 512:1024])
    for hh in range(2):
        yn = _rms(ykv[:, hh * 128:(hh + 1) * 128], 1.0 / HEAD_DIM) * gk_ref[...]
        k_ref[0, hh] = _rope(yn, cos, sin, 32).astype(BF16)
        _store_vt(vt_ref, hh, ykv[:, 256 + hh * 128:256 + (hh + 1) * 128])
    gz_ref[0] = _silu(_dotf(h, w_ref[:, 1024:1536])).astype(BF16)


def _full_spec(a):
    nd = a.ndim
    return pl.BlockSpec(a.shape, lambda b, i: (0,) * nd)


def _project(kern, name, x, norm, weights, tables, tm, n_q, dq, n_kv, dk):
    bsz, t, d = x.shape
    nt = t // tm
    gain, mods, layer, row_fn = norm
    tab_specs = [pl.BlockSpec((tm, LANES), (lambda b, i: (i, 0)) if tab.shape[0] == t
                              else (lambda b, i: (0, 0))) for tab in tables]
    return pl.pallas_call(
        kern,
        grid=(bsz, nt),
        in_specs=[pl.BlockSpec((1, tm, d), lambda b, i: (b, i, 0)),
                  pl.BlockSpec((1, d), lambda b, i: (0, 0)),
                  _mod_spec(layer, row_fn, 0), _mod_spec(layer, row_fn, 1)]
        + [_full_spec(w) for w in weights] + tab_specs,
        out_specs=[pl.BlockSpec((1, n_q, tm, dq), lambda b, i: (b, 0, i, 0)),
                   pl.BlockSpec((1, n_kv, tm, dk), lambda b, i: (b, 0, i, 0)),
                   pl.BlockSpec((1, n_kv, 1, DV_AUG, tm), lambda b, i: (b, 0, i, 0, 0)),
                   pl.BlockSpec((1, tm, GROUP_WIDTH), lambda b, i: (b, i, 0))],
        out_shape=[jax.ShapeDtypeStruct((bsz, n_q, t, dq), BF16),
                   jax.ShapeDtypeStruct((bsz, n_kv, t, dk), BF16),
                   jax.ShapeDtypeStruct((bsz, n_kv, nt, DV_AUG, tm), BF16),
                   jax.ShapeDtypeStruct((bsz, t, GROUP_WIDTH), BF16)],
        compiler_params=_params(2),
        name=name,
    )(x, gain, mods, mods, *weights, *tables)


def _finish_plain(acc, l, gate_ref, out_ref, n_sets, tq):
    o_t = acc * (1.0 / l)
    for g in range(n_sets):
        o = o_t[:, g * tq:(g + 1) * tq].T
        gate = gate_ref[0, :, g * 128:(g + 1) * 128].astype(F32)
        out_ref[0, :, g * 128:(g + 1) * 128] = (o * gate).astype(BF16)


def _finish_diff(acc, l, gate_ref, out_ref, lam_ref, subg_ref, lam_init, tq):
    o_t = acc * (1.0 / l)
    lp = lam_ref[...]
    s1 = jnp.sum(lp[0:1] * lp[1:2], axis=1, keepdims=True)
    s2 = jnp.sum(lp[2:3] * lp[3:4], axis=1, keepdims=True)
    lam = jnp.exp(s1) - jnp.exp(s2) + lam_init
    d = o_t[:, 0:tq] - lam * o_t[:, tq:2 * tq]
    dn = d * lax.rsqrt(jnp.mean(d * d, axis=0, keepdims=True) + EPS)
    o = dn.T * subg_ref[...] * (1.0 - lam_init)
    out_ref[0] = (o * gate_ref[0].astype(F32)).astype(BF16)


def _dense_attn_kernel(*refs, n_sets, tq, n_chunks, has_sink, diff, lam_init):
    refs = list(refs)
    q_ref, kc_ref, vtc_ref = refs[:3]
    pos = 3
    if n_chunks:
        kl_ref, vtl_ref = refs[pos:pos + 2]
        pos += 2
    gate_ref = refs[pos]
    pos += 1
    if has_sink:
        sink_ref = refs[pos]
        pos += 1
    if diff:
        lam_ref, subg_ref = refs[pos:pos + 2]
        pos += 2
    out_ref, acc_ref = refs[pos:pos + 2]
    s_refs = refs[pos + 2:pos + 4]

    n = n_sets * tq
    n_tiles = n // COL_TILE
    per_set = tq // COL_TILE
    tc = kc_ref.shape[2]
    acc_ref[0:HEAD_DIM, :] = jnp.zeros((HEAD_DIM, n), F32)
    if has_sink:
        m0 = sink_ref[0] * LOG2E
        acc_ref[HEAD_DIM:DV_AUG, :] = jnp.ones((ONES_ROWS, n), F32)
    else:
        m0 = jnp.full((1, n), NEG_INF, F32)
        acc_ref[HEAD_DIM:DV_AUG, :] = jnp.zeros((ONES_ROWS, n), F32)

    def cols(c):
        return slice(c * COL_TILE, (c + 1) * COL_TILE)

    def scores(k, c, s_ref, rows):
        qt = q_ref[0, c // per_set, (c % per_set) * COL_TILE:(c % per_set + 1) * COL_TILE, :]
        s = _dot_nt(k, qt)
        s_ref[0:rows, cols(c)] = s
        return jnp.max(s, axis=0, keepdims=True)

    def softmax_pv(s_ref, rows, cmax, vt, m, c):
        m_new = jnp.maximum(m, cmax)
        alpha = jnp.exp2(m - m_new)
        p = jnp.exp2(s_ref[0:rows, cols(c)] - m_new)
        pv = _dotf(vt, p.astype(BF16))
        acc_ref[:, cols(c)] = alpha * acc_ref[:, cols(c)] + pv
        return m_new

    def stage(k_next, s_next, s_cur, rows_cur, cmax_cur, vt_cur, m):
        ms, cm = [], []
        for c in range(n_tiles):
            if k_next is not None:
                cm.append(scores(k_next, c, s_next, KV_CHUNK))
            if s_cur is not None:
                ms.append(softmax_pv(s_cur, rows_cur, cmax_cur[:, cols(c)], vt_cur, m[:, cols(c)], c))
        cat = lambda xs: jnp.concatenate(xs, axis=1) if xs else None
        return cat(cm), cat(ms)

    s_a, s_b = s_refs
    k_ctx = kc_ref[0, 0]
    cmax = jnp.concatenate([scores(k_ctx, c, s_a, tc) for c in range(n_tiles)], axis=1)
    if not n_chunks:
        _, m = stage(None, None, s_a, tc, cmax, vtc_ref[0, 0, 0], m0)
    else:
        def k_chunk(j):
            start = pl.multiple_of(j * KV_CHUNK, KV_CHUNK)
            return kl_ref[0, 0, pl.ds(start, KV_CHUNK), :]

        cmax, m = stage(k_chunk(0), s_b, s_a, tc, cmax, vtc_ref[0, 0, 0], m0)

        def body(i, carry):
            cmax, m = carry
            j = 2 * i
            cmax, m = stage(k_chunk(j + 1), s_a, s_b, KV_CHUNK, cmax, vtl_ref[0, 0, j], m)
            return stage(k_chunk(j + 2), s_b, s_a, KV_CHUNK, cmax, vtl_ref[0, 0, j + 1], m)

        cmax, m = lax.fori_loop(0, n_chunks // 2 - 1, body, (cmax, m))
        j = n_chunks - 2
        cmax, m = stage(k_chunk(j + 1), s_a, s_b, KV_CHUNK, cmax, vtl_ref[0, 0, j], m)
        _, m = stage(None, None, s_a, KV_CHUNK, cmax, vtl_ref[0, 0, j + 1], m)

    acc = acc_ref[0:HEAD_DIM, :]
    l = acc_ref[HEAD_DIM:HEAD_DIM + 1, :]
    if diff:
        _finish_diff(acc, l, gate_ref, out_ref, lam_ref, subg_ref, lam_init, tq)
    else:
        _finish_plain(acc, l, gate_ref, out_ref, n_sets, tq)


def _dense_attention(name, q, kc, vtc, kl, vtl, gate, n_sets, tq, sink=None, diff=None):
    bsz, _, t, dk = q.shape
    hkv, tc = kc.shape[1], kc.shape[2]
    dv = vtc.shape[3]
    n_chunks = 0 if kl is None else kl.shape[2] // KV_CHUNK
    assert n_chunks % 2 == 0 and tq % COL_TILE == 0
    out_w = LANES if diff else n_sets * LANES
    n = n_sets * tq
    args = [q, kc, vtc]
    specs = [pl.BlockSpec((1, n_sets, tq, dk), lambda b, h, i: (b, h, i, 0)),
             pl.BlockSpec((1, 1, tc, dk), lambda b, h, i: (b, h, 0, 0)),
             pl.BlockSpec((1, 1, 1, dv, tc), lambda b, h, i: (b, h, 0, 0, 0))]
    if n_chunks:
        args += [kl, vtl]
        specs += [pl.BlockSpec((1, 1, kl.shape[2], dk), lambda b, h, i: (b, h, 0, 0)),
                  pl.BlockSpec((1, 1, n_chunks, dv, KV_CHUNK), lambda b, h, i: (b, h, 0, 0, 0))]
    args.append(gate)
    specs.append(pl.BlockSpec((1, tq, out_w), lambda b, h, i: (b, i, h)))
    if sink is not None:
        args.append(sink)
        specs.append(pl.BlockSpec((1, 1, n), lambda b, h, i: (h, 0, 0)))
    lam_init = 0.0
    if diff is not None:
        lam_par, subg, lam_init = diff
        args += [lam_par, subg]
        specs += [pl.BlockSpec(lam_par.shape, lambda b, h, i: (0, 0)),
                  pl.BlockSpec(subg.shape, lambda b, h, i: (0, 0))]
    kern = functools.partial(_dense_attn_kernel, n_sets=n_sets, tq=tq, n_chunks=n_chunks,
                             has_sink=sink is not None, diff=diff is not None, lam_init=lam_init)
    return pl.pallas_call(
        kern,
        grid=(bsz, hkv, t // tq),
        in_specs=specs,
        out_specs=pl.BlockSpec((1, tq, out_w), lambda b, h, i: (b, i, h)),
        out_shape=jax.ShapeDtypeStruct((bsz, t, GROUP_WIDTH), BF16),
        scratch_shapes=[pltpu.VMEM((dv, n), F32)] + [pltpu.VMEM((max(KV_CHUNK, tc), n), F32)] * 2,
        compiler_params=_params(3),
        name=name,
    )(*args)


def _band_attn_kernel(q_ref, kc_ref, vtc_ref, kl_ref, vtl_ref, gate_ref, sink_ref, out_ref, s_a, s_b,
                      *, blocks, n_chunks):
    w = WINDOW
    per_chunk = KV_CHUNK // w
    tc = kc_ref.shape[2]
    i = pl.program_id(2)
    first_step = i == 0
    last_step = i == pl.num_programs(2) - 1
    r2 = lax.broadcasted_iota(jnp.int32, (2 * w, 1), 0)
    r1 = lax.broadcasted_iota(jnp.int32, (w, 1), 0)
    qc = lax.broadcasted_iota(jnp.int32, (1, 2 * w), 1) & (w - 1)
    sink = sink_ref[0] * LOG2E
    k_ctx = kc_ref[0, 0]

    def pieces(t):
        chunk = i * (blocks // per_chunk) + t // per_chunk
        tt = t % per_chunk
        if tt < per_chunk - 1:
            pair = (chunk * KV_CHUNK + tt * w, chunk, tt * w, (r2 - qc) <= w, None)
            if tt > 0:
                single = (chunk * KV_CHUNK + (tt - 1) * w, chunk, (tt - 1) * w, r1 >= qc, None)
            else:
                prev_chunk = jnp.maximum(chunk - 1, 0)
                bias = jnp.where(first_step, NEG_INF, 0.0) if t == 0 else None
                single = (prev_chunk * KV_CHUNK + (per_chunk - 1) * w, prev_chunk, (per_chunk - 1) * w,
                          r1 >= qc, bias)
        else:
            pair = (chunk * KV_CHUNK + (tt - 1) * w, chunk, (tt - 1) * w, r2 >= qc, None)
            next_chunk = jnp.minimum(chunk + 1, n_chunks - 1)
            bias = jnp.where(last_step, NEG_INF, 0.0) if t == blocks - 1 else None
            single = (next_chunk * KV_CHUNK, next_chunk, 0, r1 <= qc, bias)
        return pair, single

    def q_block(t):
        rows = slice(t * w, (t + 1) * w)
        return jnp.concatenate([q_ref[0, 0, rows, :], q_ref[0, 1, rows, :]], axis=0)

    def scores(t, s_ref):
        qt = q_block(t)
        s_ctx = _dot_nt(k_ctx, qt)
        s_ref[0:tc, :] = s_ctx
        cmax = jnp.max(s_ctx, axis=0, keepdims=True)
        row0 = tc
        for (start, _, _, mask, bias), rows in zip(pieces(t), (2 * w, w)):
            k = kl_ref[0, 0, pl.ds(pl.multiple_of(start, w), rows), :]
            s = _dot_nt(k, qt)
            if bias is not None:
                s = s + bias
            s = jnp.where(mask, s, NEG_INF)
            s_ref[row0:row0 + rows, :] = s
            cmax = jnp.maximum(cmax, jnp.max(s, axis=0, keepdims=True))
            row0 += rows
        return cmax

    def softmax_pv(t, s_ref, cmax):
        m = jnp.maximum(sink, cmax)
        acc = _dotf(vtc_ref[0, 0, 0], jnp.exp2(s_ref[0:tc, :] - m).astype(BF16))
        row0 = tc
        for (_, chunk, lane0, _, _), rows in zip(pieces(t), (2 * w, w)):
            vt = vtl_ref[0, 0, chunk, :, lane0:lane0 + rows]
            acc = acc + _dotf(vt, jnp.exp2(s_ref[row0:row0 + rows, :] - m).astype(BF16))
            row0 += rows
        l = acc[HEAD_DIM:HEAD_DIM + 1, :] + jnp.exp2(sink - m)
        o_t = acc[0:HEAD_DIM, :] * (1.0 / l)
        rows = slice(t * w, (t + 1) * w)
        for g in range(2):
            gcols = slice(g * LANES, (g + 1) * LANES)
            o = o_t[:, g * w:(g + 1) * w].T
            out_ref[0, rows, gcols] = (o * gate_ref[0, rows, gcols].astype(F32)).astype(BF16)

    bufs = (s_a, s_b)
    cmax = scores(0, bufs[0])
    for t in range(blocks):
        cmax_next = scores(t + 1, bufs[(t + 1) % 2]) if t + 1 < blocks else None
        softmax_pv(t, bufs[t % 2], cmax)
        cmax = cmax_next


def _band_attention(q, kc, vtc, kl, vtl, gate, sink):
    bsz, _, s, dk = q.shape
    tq = min(TQ_BAND, s)
    tc = kc.shape[2]
    n_chunks = s // KV_CHUNK
    assert tq % KV_CHUNK == 0 and dk == HEAD_DIM
    return pl.pallas_call(
        functools.partial(_band_attn_kernel, blocks=tq // WINDOW, n_chunks=n_chunks),
        grid=(bsz, 2, s // tq),
        in_specs=[
            pl.BlockSpec((1, 2, tq, dk), lambda b, h, i: (b, h, i, 0)),
            pl.BlockSpec((1, 1, tc, dk), lambda b, h, i: (b, h, 0, 0)),
            pl.BlockSpec((1, 1, 1, DV_AUG, tc), lambda b, h, i: (b, h, 0, 0, 0)),
            pl.BlockSpec((1, 1, s, dk), lambda b, h, i: (b, h, 0, 0)),
            pl.BlockSpec((1, 1, n_chunks, DV_AUG, KV_CHUNK), lambda b, h, i: (b, h, 0, 0, 0)),
            pl.BlockSpec((1, tq, 2 * LANES), lambda b, h, i: (b, i, h)),
            pl.BlockSpec((1, 1, 2 * WINDOW), lambda b, h, i: (h, 0, 0)),
        ],
        out_specs=pl.BlockSpec((1, tq, 2 * LANES), lambda b, h, i: (b, i, h)),
        out_shape=jax.ShapeDtypeStruct((bsz, s, GROUP_WIDTH), BF16),
        scratch_shapes=[pltpu.VMEM((tc + 3 * WINDOW, 2 * WINDOW), F32)] * 2,
        compiler_params=_params(3),
        name="attn_a_band",
    )(q, kc, vtc, kl, vtl, gate, sink)


def _out_kernel(*refs, final):
    ua, ub, uc, ud, w_ref, x_ref, gate_ref = refs[:7]
    out_ref = refs[-1]
    acc = (_dotf(ua[0], w_ref[0]) + _dotf(ub[0], w_ref[1])
           + _dotf(uc[0], w_ref[2]) + _dotf(ud[0], w_ref[3]))
    xn = x_ref[0] + gate_ref[0, 0] * acc
    if final:
        xn = _rms(xn, 1.0 / D_MODEL) * refs[7][...]
    out_ref[0] = xn


def _out_project(us, w_out, x, mods, layer, row_fn, tm, final_g=None):
    bsz, t, d = x.shape
    u_spec = pl.BlockSpec((1, tm, GROUP_WIDTH), lambda b, i: (b, i, 0))
    specs = [u_spec] * 4 + [_full_spec(w_out), pl.BlockSpec((1, tm, d), lambda b, i: (b, i, 0)),
                            _mod_spec(layer, row_fn, 2)]
    args = list(us) + [w_out, x, mods]
    if final_g is not None:
        specs.append(pl.BlockSpec((1, d), lambda b, i: (0, 0)))
        args.append(final_g)
    return pl.pallas_call(
        functools.partial(_out_kernel, final=final_g is not None),
        grid=(bsz, t // tm),
        in_specs=specs,
        out_specs=pl.BlockSpec((1, tm, d), lambda b, i: (b, i, 0)),
        out_shape=jax.ShapeDtypeStruct((bsz, t, d), F32),
        compiler_params=_params(2),
        name="out_proj",
    )(*args)


def _rope_tables(rows, rot_dim):
    row = jnp.broadcast_to(jnp.arange(rows)[:, None], (rows, GRID_W)).reshape(-1).astype(F32)
    col = jnp.broadcast_to(jnp.arange(GRID_W)[None, :], (rows, GRID_W)).reshape(-1).astype(F32)
    axis_dim = rot_dim // 2
    inv_freq = ROPE_THETA ** (-jnp.arange(0, axis_dim, 2, dtype=F32) / axis_dim)
    ang_r = row[:, None] * inv_freq[None, :]
    ang_c = col[:, None] * inv_freq[None, :]
    ang = jnp.concatenate([ang_r, ang_r, ang_c, ang_c], axis=-1)
    quarter = rot_dim // 4
    sign = jnp.where((jnp.arange(rot_dim) // quarter) % 2 == 0, -1.0, 1.0).astype(F32)
    cos, sin = jnp.cos(ang), jnp.sin(ang) * sign
    reps = LANES // rot_dim
    return jnp.tile(cos, (1, reps)), jnp.tile(sin, (1, reps))


def _layer_weights(w_in, w_uq, w_ukv, gq, gkv):
    wa = w_in[:, 0:1536]
    wb = w_in[:, 1536:3584]
    zeros64 = jnp.zeros((w_in.shape[0], 64), w_in.dtype)
    wc = jnp.concatenate([w_in[:, 3584:4032], zeros64,
                          w_in[:, 4032:4160],
                          w_in[:, 4160:4224], zeros64,
                          w_in[:, 4224:4736]], axis=1)
    wd = w_in[:, 4736:6272]
    uq = w_uq.reshape(C_Q_LORA, 4, C_QK_DIM)
    uq = jnp.concatenate([uq[:, :, :128].reshape(C_Q_LORA, 512),
                          uq[:, :, 128:].reshape(C_Q_LORA, 256)], axis=1)
    uq = jnp.concatenate([uq, jnp.zeros((64, 768), uq.dtype)], axis=0)
    gq_pad = jnp.concatenate([gq, jnp.zeros((64,), gq.dtype)])[None, :]
    cast = lambda a: a.astype(BF16)
    return cast(wa), cast(wb), cast(wc), cast(wd), cast(uq), cast(w_ukv), gq_pad, gkv[None, :]


def kernel(x, c, ctx, c_ctx, w_mod, b_mod, norm_g, w_in, c_q_norm_g, c_kv_norm_g, c_w_uq, c_w_ukv,
           d_q_norm_g, d_k_norm_g, a_sink, b_lambda, b_subln_g, w_out, final_norm_g):
    bsz, s, d = x.shape
    tc = ctx.shape[1]
    depth = w_mod.shape[0]
    assert d == D_MODEL and s % TM_LATENT == 0 and s % GRID_W == 0 and bsz + 1 <= MOD_ROWS
    assert tc % LANES == 0 and TM_LATENT == KV_CHUNK

    cc = jnp.concatenate([c, c_ctx[None, :], jnp.zeros((MOD_ROWS - bsz - 1, d), c.dtype)], axis=0)
    mods = _modulation(cc, w_mod, b_mod).reshape(depth, MOD_ROWS, 1, 3 * d)
    lat_row = lambda b: b
    ctx_row = lambda b: bsz

    rows = s // GRID_W
    cos_h, sin_h = _rope_tables(rows, HEAD_DIM)
    cos_b, sin_b = _rope_tables(rows, B_QK_DIM)
    one = jnp.ones((tc, LANES), F32)
    zero = jnp.zeros((tc, LANES), F32)
    tab_lat = {"h": (cos_h, sin_h), "b": (cos_b, sin_b)}
    tab_ctx = {"h": (one, zero), "b": (one, zero)}

    tq2, tq1 = min(N_DENSE // 2, s), min(N_DENSE, s)
    for l in range(depth):
        last = l == depth - 1
        wa, wb, wc, wd, uq, ukv, gq_pad, gkv = _layer_weights(
            w_in[l], c_w_uq[l], c_w_ukv[l], c_q_norm_g[l], c_kv_norm_g[l])
        dq_g, dk_g = d_q_norm_g[l][None, :], d_k_norm_g[l][None, :]
        wo = w_out[l].astype(BF16).reshape(4, GROUP_WIDTH, d)
        lam_init = 0.8 - 0.6 * math.exp(-0.3 * l)
        diff = (b_lambda[l], b_subln_g[l][None, :], lam_init)
        sink2 = a_sink[l].reshape(2, 2)

        def project(stream, row_fn, tabs, tm):
            norm = (norm_g[l][None, :], mods, l, row_fn)
            pa = _project(_proj_a_kernel, "proj_a", stream, norm, [wa], tabs["h"], tm, 4, 128, 2, 128)
            pb = _project(_proj_b_kernel, "proj_b", stream, norm, [wb], tabs["b"], tm, 8, 128, 4, 128)
            pc = _project(_proj_c_kernel, "proj_c", stream, norm, [wc, uq, ukv, gq_pad, gkv], tabs["b"],
                          tm, 4, 256, 4, 256)
            pd = _project(_proj_d_kernel, "proj_d", stream, norm, [wd, dq_g, dk_g], tabs["h"], tm,
                          4, 128, 2, 128)
            return pa, pb, pc, pd

        def sink_rows(tq):
            return jnp.repeat(sink2, tq, axis=1).reshape(2, 1, 2 * tq)

        (qa, ka, vta, gza), (qb, kb, vtb, gzb), (qc, kc, vtc, gzc), (qd, kd, vtd, gzd) = project(
            x, lat_row, tab_lat, TM_LATENT)
        (qa_c, ka_c, vta_c, gza_c), (qb_c, kb_c, vtb_c, gzb_c), (qc_c, kc_c, vtc_c, gzc_c), \
            (qd_c, kd_c, vtd_c, gzd_c) = project(ctx, ctx_row, tab_ctx, tc)

        ua = _band_attention(qa, ka_c, vta_c, ka, vta, gza, sink_rows(WINDOW))
        ub = _dense_attention("attn_b", qb, kb_c, vtb_c, kb, vtb, gzb, 2, tq2, diff=diff)
        uc = _dense_attention("attn_c", qc, kc_c, vtc_c, kc, vtc, gzc, 1, tq1)
        ud = _dense_attention("attn_d", qd, kd_c, vtd_c, kd, vtd, gzd, 2, tq2)
        if not last:
            ua_c = _dense_attention("attn_a_ctx", qa_c, ka_c, vta_c, None, None, gza_c, 2, tc,
                                    sink=sink_rows(tc))
            ub_c = _dense_attention("attn_b_ctx", qb_c, kb_c, vtb_c, None, None, gzb_c, 2, tc, diff=diff)
            uc_c = _dense_attention("attn_c_ctx", qc_c, kc_c, vtc_c, None, None, gzc_c, 1, tc)
            ud_c = _dense_attention("attn_d_ctx", qd_c, kd_c, vtd_c, None, None, gzd_c, 2, tc)
            ctx = _out_project((ua_c, ub_c, uc_c, ud_c), wo, ctx, mods, l, ctx_row, tc)
        x = _out_project((ua, ub, uc, ud), wo, x, mods, l, lat_row, TM_LATENT,
                         final_g=final_norm_g[None, :] if last else None)
    return x
```

```python
import functools
import math

import jax
import jax.numpy as jnp
from jax import lax
from jax.experimental import pallas as pl
from jax.experimental.pallas import tpu as pltpu

F32 = jnp.float32
BF16 = jnp.bfloat16

D_MODEL = 2048
HEAD_DIM = 128
GRID_W = 64
GROUP_WIDTH = 512
WINDOW = 128
ROPE_THETA = 10000.0
EPS = 1e-6
NEG_INF = -1e30
C_Q_LORA = 448
C_ROPE = 64
C_QK_DIM = 192
B_QK_DIM = 64
LOG2E = math.log2(math.e)
LANES = 128
ONES_ROWS = 16
DV_AUG = HEAD_DIM + ONES_ROWS
MOD_ROWS = 16
VMEM_LIMIT = 56 * 1024 * 1024

TM_LATENT = 1024
TM_QKNORM = 512
TM_OUT = 512
KV_CHUNK = 512
N_DENSE = 4096
TQ_BAND = 1024
MOD_TN = 512
COL_TILE = 256


def _dotf(a, b):
    return jnp.dot(a, b, preferred_element_type=F32)


def _dot_nt(a, b):
    return lax.dot_general(a, b, (((1,), (1,)), ((), ())), preferred_element_type=F32)


def _silu(z):
    return z / (1.0 + jnp.exp(-z))


def _rms(y, inv_n):
    return y * lax.rsqrt(jnp.sum(y * y, axis=-1, keepdims=True) * inv_n + EPS)


def _rope(x, cos, sin_signed, shift):
    lane = lax.broadcasted_iota(jnp.int32, x.shape, 1)
    even = (lane & shift) == 0
    x_up = pltpu.roll(x, LANES - shift, 1)
    x_dn = pltpu.roll(x, shift, 1)
    return x * cos + jnp.where(even, x_up, x_dn) * sin_signed


def _params(n_axes):
    return pltpu.CompilerParams(dimension_semantics=("parallel",) * n_axes,
                                vmem_limit_bytes=VMEM_LIMIT)


def _mod_kernel(c_ref, w_ref, b_ref, o_ref):
    a = _silu(c_ref[...]).astype(BF16)
    o_ref[0] = _dotf(a, w_ref[0].astype(BF16)) + b_ref[0]


def _modulation(cc, w_mod, b_mod):
    depth, d, n = w_mod.shape
    return pl.pallas_call(
        _mod_kernel,
        grid=(depth, n // MOD_TN),
        in_specs=[pl.BlockSpec((MOD_ROWS, d), lambda l, j: (0, 0)),
                  pl.BlockSpec((1, d, MOD_TN), lambda l, j: (l, 0, j)),
                  pl.BlockSpec((1, 1, MOD_TN), lambda l, j: (l, 0, j))],
        out_specs=pl.BlockSpec((1, MOD_ROWS, MOD_TN), lambda l, j: (l, 0, j)),
        out_shape=jax.ShapeDtypeStruct((depth, MOD_ROWS, n), F32),
        compiler_params=_params(2),
        name="modulation",
    )(cc, w_mod, b_mod.reshape(depth, 1, n))


def _prenorm(x_ref, g_ref, shift_ref, scale_ref):
    y = _rms(x_ref[0], 1.0 / D_MODEL) * g_ref[...]
    return (y * (1.0 + scale_ref[0, 0]) + shift_ref[0, 0]).astype(BF16)


def _mod_spec(layer, row_fn, which):
    return pl.BlockSpec((1, 1, 1, D_MODEL), lambda b, i: (layer, row_fn(b), 0, which))


def _store_vt(vt_ref, head, y):
    chunk = vt_ref.shape[-1]
    for j in range(vt_ref.shape[2]):
        vt_ref[0, head, j, 0:HEAD_DIM, :] = y[j * chunk:(j + 1) * chunk, :].T.astype(BF16)
        vt_ref[0, head, j, HEAD_DIM:DV_AUG, :] = jnp.ones((ONES_ROWS, chunk), BF16)


def _proj_a_kernel(x_ref, g_ref, shift_ref, scale_ref, w_ref, cos_ref, sin_ref,
                   q_ref, k_ref, vt_ref, gz_ref):
    h = _prenorm(x_ref, g_ref, shift_ref, scale_ref)
    cos, sin = cos_ref[...], sin_ref[...]
    scale = LOG2E / math.sqrt(HEAD_DIM)
    yq = _dotf(h, w_ref[:, 0:512])
    for hh in range(4):
        q_ref[0, hh] = (_rope(yq[:, hh * 128:(hh + 1) * 128], cos, sin, 32) * scale).astype(BF16)
    ykv = _dotf(h, w_ref[:, 512:1024])
    for hh in range(2):
        k_ref[0, hh] = _rope(ykv[:, hh * 128:(hh + 1) * 128], cos, sin, 32).astype(BF16)
        _store_vt(vt_ref, hh, ykv[:, 256 + hh * 128:256 + (hh + 1) * 128])
    gz_ref[0] = _silu(_dotf(h, w_ref[:, 1024:1536])).astype(BF16)


def _proj_b_kernel(x_ref, g_ref, shift_ref, scale_ref, w_ref, cos_ref, sin_ref,
                   q_ref, k_ref, vt_ref, gz_ref):
    h = _prenorm(x_ref, g_ref, shift_ref, scale_ref)
    cos, sin = cos_ref[...], sin_ref[...]
    scale = LOG2E / math.sqrt(B_QK_DIM)
    yq = _dotf(h, w_ref[:, 0:512])
    lane = lax.broadcasted_iota(jnp.int32, (h.shape[0], LANES), 1)
    first = lane < B_QK_DIM
    for hh in range(4):
        y = _rope(yq[:, hh * 128:(hh + 1) * 128], cos, sin, 16) * scale
        q_ref[0, 2 * hh] = jnp.where(first, y, 0.0).astype(BF16)
        q_ref[0, 2 * hh + 1] = jnp.where(first, 0.0, y).astype(BF16)
    yk = _dotf(h, w_ref[:, 512:1024])
    for hh in range(4):
        k_ref[0, hh] = _rope(yk[:, hh * 128:(hh + 1) * 128], cos, sin, 16).astype(BF16)
    yv = _dotf(h, w_ref[:, 1024:1536])
    for hh in range(4):
        _store_vt(vt_ref, hh, yv[:, hh * 128:(hh + 1) * 128])
    gz_ref[0] = _silu(_dotf(h, w_ref[:, 1536:2048])).astype(BF16)


def _proj_c_kernel(x_ref, g_ref, shift_ref, scale_ref, w_ref, wuq_ref, wukv_ref, gq_ref, gkv_ref,
                   cos_ref, sin_ref, q_ref, k_ref, vt_ref, gz_ref):
    h = _prenorm(x_ref, g_ref, shift_ref, scale_ref)
    cos, sin = cos_ref[...], sin_ref[...]
    scale = LOG2E / math.sqrt(C_QK_DIM)
    ycq = _dotf(h, w_ref[:, 0:512])
    cqn = (_rms(ycq, 1.0 / C_Q_LORA) * gq_ref[...]).astype(BF16)
    yq = _dotf(cqn, wuq_ref[...])
    yckv = _dotf(h, w_ref[:, 512:768])
    ckvn = (_rms(yckv[:, 0:128], 1.0 / 128) * gkv_ref[...]).astype(BF16)
    ykv = _dotf(ckvn, wukv_ref[...])
    kr_even = _rope(yckv[:, 128:256], cos, sin, 16)
    kr_odd = pltpu.roll(kr_even, C_ROPE, 1)
    qr = [_rope(yq[:, 512 + p * 128:512 + (p + 1) * 128], cos, sin, 16) * scale for p in range(2)]
    for hh in range(4):
        q_ref[0, hh, :, 0:128] = (yq[:, hh * 128:(hh + 1) * 128] * scale).astype(BF16)
        q_ref[0, hh, :, 128:256] = qr[hh // 2].astype(BF16)
        k_ref[0, hh, :, 0:128] = ykv[:, hh * 256:hh * 256 + 128].astype(BF16)
        k_ref[0, hh, :, 128:256] = (kr_even if hh % 2 == 0 else kr_odd).astype(BF16)
        _store_vt(vt_ref, hh, ykv[:, hh * 256 + 128:(hh + 1) * 256])
    gz_ref[0] = _silu(_dotf(h, w_ref[:, 768:1280])).astype(BF16)


def _proj_d_kernel(x_ref, g_ref, shift_ref, scale_ref, w_ref, gq_ref, gk_ref, cos_ref, sin_ref,
                   q_ref, k_ref, vt_ref, gz_ref):
    h = _prenorm(x_ref, g_ref, shift_ref, scale_ref)
    cos, sin = cos_ref[...], sin_ref[...]
    scale = LOG2E / math.sqrt(HEAD_DIM)
    yq = _dotf(h, w_ref[:, 0:512])
    for hh in range(4):
        yn = _rms(yq[:, hh * 128:(hh + 1) * 128], 1.0 / HEAD_DIM) * gq_ref[...]
        q_ref[0, hh] = (_rope(yn, cos, sin, 32) * scale).astype(BF16)
    yk = _dotf(h, w_ref[:, 512:768])
    for hh in range(2):
        yn = _rms(yk[:, hh * 128:(hh + 1) * 128], 1.0 / HEAD_DIM) * gk_ref[...]
        k_ref[0, hh] = _rope(yn, cos, sin, 32).astype(BF16)
    gz_ref[0] = _silu(_dotf(h, w_ref[:, 1024:1536])).astype(BF16)
    yv = _dotf(h, w_ref[:, 768:1024])
    for hh in range(2):
        _store_vt(vt_ref, hh, yv[:, hh * 128:(hh + 1) * 128])


def _full_spec(a):
    nd = a.ndim
    return pl.BlockSpec(a.shape, lambda b, i: (0,) * nd)


def _project(kern, name, x, norm, weights, tables, tm, n_q, dq, n_kv, dk):
    bsz, t, d = x.shape
    nt = t // tm
    chunk = min(tm, KV_CHUNK)
    sub = tm // chunk
    gain, mods, layer, row_fn = norm
    tab_specs = [pl.BlockSpec((tm, LANES), (lambda b, i: (i, 0)) if tab.shape[0] == t
                              else (lambda b, i: (0, 0))) for tab in tables]
    return pl.pallas_call(
        kern,
        grid=(bsz, nt),
        in_specs=[pl.BlockSpec((1, tm, d), lambda b, i: (b, i, 0)),
                  pl.BlockSpec((1, d), lambda b, i: (0, 0)),
                  _mod_spec(layer, row_fn, 0), _mod_spec(layer, row_fn, 1)]
        + [_full_spec(w) for w in weights] + tab_specs,
        out_specs=[pl.BlockSpec((1, n_q, tm, dq), lambda b, i: (b, 0, i, 0)),
                   pl.BlockSpec((1, n_kv, tm, dk), lambda b, i: (b, 0, i, 0)),
                   pl.BlockSpec((1, n_kv, sub, DV_AUG, chunk), lambda b, i: (b, 0, i, 0, 0)),
                   pl.BlockSpec((1, tm, GROUP_WIDTH), lambda b, i: (b, i, 0))],
        out_shape=[jax.ShapeDtypeStruct((bsz, n_q, t, dq), BF16),
                   jax.ShapeDtypeStruct((bsz, n_kv, t, dk), BF16),
                   jax.ShapeDtypeStruct((bsz, n_kv, nt * sub, DV_AUG, chunk), BF16),
                   jax.ShapeDtypeStruct((bsz, t, GROUP_WIDTH), BF16)],
        compiler_params=_params(2),
        name=name,
    )(x, gain, mods, mods, *weights, *tables)


def _finish_plain(acc, l, gate_ref, out_ref, n_sets, tq):
    o_t = acc * (1.0 / l)
    for g in range(n_sets):
        o = o_t[:, g * tq:(g + 1) * tq].T
        gate = gate_ref[0, :, g * 128:(g + 1) * 128].astype(F32)
        out_ref[0, :, g * 128:(g + 1) * 128] = (o * gate).astype(BF16)


def _finish_diff(acc, l, gate_ref, out_ref, lam_ref, subg_ref, lam_init, tq):
    o_t = acc * (1.0 / l)
    lp = lam_ref[...]
    s1 = jnp.sum(lp[0:1] * lp[1:2], axis=1, keepdims=True)
    s2 = jnp.sum(lp[2:3] * lp[3:4], axis=1, keepdims=True)
    lam = jnp.exp(s1) - jnp.exp(s2) + lam_init
    d = o_t[:, 0:tq] - lam * o_t[:, tq:2 * tq]
    dn = d * lax.rsqrt(jnp.mean(d * d, axis=0, keepdims=True) + EPS)
    o = dn.T * subg_ref[...] * (1.0 - lam_init)
    out_ref[0] = (o * gate_ref[0].astype(F32)).astype(BF16)


def _dense_attn_kernel(*refs, n_sets, tq, n_chunks, has_sink, diff, lam_init):
    refs = list(refs)
    q_ref, kc_ref, vtc_ref = refs[:3]
    pos = 3
    if n_chunks:
        kl_ref, vtl_ref = refs[pos:pos + 2]
        pos += 2
    gate_ref = refs[pos]
    pos += 1
    if has_sink:
        sink_ref = refs[pos]
        pos += 1
    if diff:
        lam_ref, subg_ref = refs[pos:pos + 2]
        pos += 2
    out_ref, acc_ref = refs[pos:pos + 2]
    s_refs = refs[pos + 2:pos + 4]

    n = n_sets * tq
    n_tiles = n // COL_TILE
    per_set = tq // COL_TILE
    tc = kc_ref.shape[2]
    acc_ref[0:HEAD_DIM, :] = jnp.zeros((HEAD_DIM, n), F32)
    if has_sink:
        m0 = sink_ref[0] * LOG2E
        acc_ref[HEAD_DIM:DV_AUG, :] = jnp.ones((ONES_ROWS, n), F32)
    else:
        m0 = jnp.full((1, n), NEG_INF, F32)
        acc_ref[HEAD_DIM:DV_AUG, :] = jnp.zeros((ONES_ROWS, n), F32)

    def cols(c):
        return slice(c * COL_TILE, (c + 1) * COL_TILE)

    def scores(k, c, s_ref, rows):
        qt = q_ref[0, c // per_set, (c % per_set) * COL_TILE:(c % per_set + 1) * COL_TILE, :]
        s = _dot_nt(k, qt)
        s_ref[0:rows, cols(c)] = s
        return jnp.max(s, axis=0, keepdims=True)

    def softmax_pv(s_ref, rows, cmax, vt, m, c):
        m_new = jnp.maximum(m, cmax)
        alpha = jnp.exp2(m - m_new)
        p = jnp.exp2(s_ref[0:rows, cols(c)] - m_new)
        pv = _dotf(vt, p.astype(BF16))
        acc_ref[:, cols(c)] = alpha * acc_ref[:, cols(c)] + pv
        return m_new

    def stage(k_next, s_next, s_cur, rows_cur, cmax_cur, vt_cur, m):
        ms, cm = [], []
        for c in range(n_tiles):
            if k_next is not None:
                cm.append(scores(k_next, c, s_next, KV_CHUNK))
            if s_cur is not None:
                ms.append(softmax_pv(s_cur, rows_cur, cmax_cur[:, cols(c)], vt_cur, m[:, cols(c)], c))
        cat = lambda xs: jnp.concatenate(xs, axis=1) if xs else None
        return cat(cm), cat(ms)

    s_a, s_b = s_refs
    k_ctx = kc_ref[0, 0]
    cmax = jnp.concatenate([scores(k_ctx, c, s_a, tc) for c in range(n_tiles)], axis=1)
    if not n_chunks:
        _, m = stage(None, None, s_a, tc, cmax, vtc_ref[0, 0, 0], m0)
    else:
        def k_chunk(j):
            start = pl.multiple_of(j * KV_CHUNK, KV_CHUNK)
            return kl_ref[0, 0, pl.ds(start, KV_CHUNK), :]

        cmax, m = stage(k_chunk(0), s_b, s_a, tc, cmax, vtc_ref[0, 0, 0], m0)

        def body(i, carry):
            cmax, m = carry
            j = 2 * i
            cmax, m = stage(k_chunk(j + 1), s_a, s_b, KV_CHUNK, cmax, vtl_ref[0, 0, j], m)
            return stage(k_chunk(j + 2), s_b, s_a, KV_CHUNK, cmax, vtl_ref[0, 0, j + 1], m)

        cmax, m = lax.fori_loop(0, n_chunks // 2 - 1, body, (cmax, m))
        j = n_chunks - 2
        cmax, m = stage(k_chunk(j + 1), s_a, s_b, KV_CHUNK, cmax, vtl_ref[0, 0, j], m)
        _, m = stage(None, None, s_a, KV_CHUNK, cmax, vtl_ref[0, 0, j + 1], m)

    acc = acc_ref[0:HEAD_DIM, :]
    l = acc_ref[HEAD_DIM:HEAD_DIM + 1, :]
    if diff:
        _finish_diff(acc, l, gate_ref, out_ref, lam_ref, subg_ref, lam_init, tq)
    else:
        _finish_plain(acc, l, gate_ref, out_ref, n_sets, tq)


def _dense_attention(name, q, kc, vtc, kl, vtl, gate, n_sets, tq, sink=None, diff=None):
    bsz, _, t, dk = q.shape
    hkv, tc = kc.shape[1], kc.shape[2]
    dv = vtc.shape[3]
    n_chunks = 0 if kl is None else kl.shape[2] // KV_CHUNK
    assert n_chunks % 2 == 0 and tq % COL_TILE == 0
    out_w = LANES if diff else n_sets * LANES
    n = n_sets * tq
    args = [q, kc, vtc]
    specs = [pl.BlockSpec((1, n_sets, tq, dk), lambda b, h, i: (b, h, i, 0)),
             pl.BlockSpec((1, 1, tc, dk), lambda b, h, i: (b, h, 0, 0)),
             pl.BlockSpec((1, 1, 1, dv, tc), lambda b, h, i: (b, h, 0, 0, 0))]
    if n_chunks:
        args += [kl, vtl]
        specs += [pl.BlockSpec((1, 1, kl.shape[2], dk), lambda b, h, i: (b, h, 0, 0)),
                  pl.BlockSpec((1, 1, n_chunks, dv, KV_CHUNK), lambda b, h, i: (b, h, 0, 0, 0))]
    args.append(gate)
    specs.append(pl.BlockSpec((1, tq, out_w), lambda b, h, i: (b, i, h)))
    if sink is not None:
        args.append(sink)
        specs.append(pl.BlockSpec((1, 1, n), lambda b, h, i: (h, 0, 0)))
    lam_init = 0.0
    if diff is not None:
        lam_par, subg, lam_init = diff
        args += [lam_par, subg]
        specs += [pl.BlockSpec(lam_par.shape, lambda b, h, i: (0, 0)),
                  pl.BlockSpec(subg.shape, lambda b, h, i: (0, 0))]
    kern = functools.partial(_dense_attn_kernel, n_sets=n_sets, tq=tq, n_chunks=n_chunks,
                             has_sink=sink is not None, diff=diff is not None, lam_init=lam_init)
    return pl.pallas_call(
        kern,
        grid=(bsz, hkv, t // tq),
        in_specs=specs,
        out_specs=pl.BlockSpec((1, tq, out_w), lambda b, h, i: (b, i, h)),
        out_shape=jax.ShapeDtypeStruct((bsz, t, GROUP_WIDTH), BF16),
        scratch_shapes=[pltpu.VMEM((dv, n), F32)] + [pltpu.VMEM((max(KV_CHUNK, tc), n), F32)] * 2,
        compiler_params=_params(3),
        name=name,
    )(*args)


def _band_attn_kernel(q_ref, kc_ref, vtc_ref, kl_ref, vtl_ref, gate_ref, sink_ref, out_ref, s_a, s_b,
                      *, blocks, n_chunks):
    w = WINDOW
    per_chunk = KV_CHUNK // w
    tc = kc_ref.shape[2]
    i = pl.program_id(2)
    first_step = i == 0
    last_step = i == pl.num_programs(2) - 1
    r2 = lax.broadcasted_iota(jnp.int32, (2 * w, 1), 0)
    r1 = lax.broadcasted_iota(jnp.int32, (w, 1), 0)
    qc = lax.broadcasted_iota(jnp.int32, (1, 2 * w), 1) & (w - 1)
    sink = sink_ref[0] * LOG2E
    k_ctx = kc_ref[0, 0]

    def pieces(t):
        chunk = i * (blocks // per_chunk) + t // per_chunk
        tt = t % per_chunk
        if tt < per_chunk - 1:
            pair = (chunk * KV_CHUNK + tt * w, chunk, tt * w, (r2 - qc) <= w, None)
            if tt > 0:
                single = (chunk * KV_CHUNK + (tt - 1) * w, chunk, (tt - 1) * w, r1 >= qc, None)
            else:
                prev_chunk = jnp.maximum(chunk - 1, 0)
                bias = jnp.where(first_step, NEG_INF, 0.0) if t == 0 else None
                single = (prev_chunk * KV_CHUNK + (per_chunk - 1) * w, prev_chunk, (per_chunk - 1) * w,
                          r1 >= qc, bias)
        else:
            pair = (chunk * KV_CHUNK + (tt - 1) * w, chunk, (tt - 1) * w, r2 >= qc, None)
            next_chunk = jnp.minimum(chunk + 1, n_chunks - 1)
            bias = jnp.where(last_step, NEG_INF, 0.0) if t == blocks - 1 else None
            single = (next_chunk * KV_CHUNK, next_chunk, 0, r1 <= qc, bias)
        return pair, single

    def q_block(t):
        rows = slice(t * w, (t + 1) * w)
        return jnp.concatenate([q_ref[0, 0, rows, :], q_ref[0, 1, rows, :]], axis=0)

    def scores(t, s_ref):
        qt = q_block(t)
        s_ctx = _dot_nt(k_ctx, qt)
        s_ref[0:tc, :] = s_ctx
        cmax = jnp.max(s_ctx, axis=0, keepdims=True)
        row0 = tc
        for (start, _, _, mask, bias), rows in zip(pieces(t), (2 * w, w)):
            k = kl_ref[0, 0, pl.ds(pl.multiple_of(start, w), rows), :]
            s = _dot_nt(k, qt)
            if bias is not None:
                s = s + bias
            s = jnp.where(mask, s, NEG_INF)
            s_ref[row0:row0 + rows, :] = s
            cmax = jnp.maximum(cmax, jnp.max(s, axis=0, keepdims=True))
            row0 += rows
        return cmax

    def softmax_pv(t, s_ref, cmax):
        m = jnp.maximum(sink, cmax)
        acc = _dotf(vtc_ref[0, 0, 0], jnp.exp2(s_ref[0:tc, :] - m).astype(BF16))
        row0 = tc
        for (_, chunk, lane0, _, _), rows in zip(pieces(t), (2 * w, w)):
            vt = vtl_ref[0, 0, chunk, :, lane0:lane0 + rows]
            acc = acc + _dotf(vt, jnp.exp2(s_ref[row0:row0 + rows, :] - m).astype(BF16))
            row0 += rows
        l = acc[HEAD_DIM:HEAD_DIM + 1, :] + jnp.exp2(sink - m)
        o_t = acc[0:HEAD_DIM, :] * (1.0 / l)
        rows = slice(t * w, (t + 1) * w)
        for g in range(2):
            gcols = slice(g * LANES, (g + 1) * LANES)
            o = o_t[:, g * w:(g + 1) * w].T
            out_ref[0, rows, gcols] = (o * gate_ref[0, rows, gcols].astype(F32)).astype(BF16)

    bufs = (s_a, s_b)
    cmax = scores(0, bufs[0])
    for t in range(blocks):
        cmax_next = scores(t + 1, bufs[(t + 1) % 2]) if t + 1 < blocks else None
        softmax_pv(t, bufs[t % 2], cmax)
        cmax = cmax_next


def _band_attention(q, kc, vtc, kl, vtl, gate, sink):
    bsz, _, s, dk = q.shape
    tq = min(TQ_BAND, s)
    tc = kc.shape[2]
    n_chunks = s // KV_CHUNK
    assert tq % KV_CHUNK == 0 and dk == HEAD_DIM
    return pl.pallas_call(
        functools.partial(_band_attn_kernel, blocks=tq // WINDOW, n_chunks=n_chunks),
        grid=(bsz, 2, s // tq),
        in_specs=[
            pl.BlockSpec((1, 2, tq, dk), lambda b, h, i: (b, h, i, 0)),
            pl.BlockSpec((1, 1, tc, dk), lambda b, h, i: (b, h, 0, 0)),
            pl.BlockSpec((1, 1, 1, DV_AUG, tc), lambda b, h, i: (b, h, 0, 0, 0)),
            pl.BlockSpec((1, 1, s, dk), lambda b, h, i: (b, h, 0, 0)),
            pl.BlockSpec((1, 1, n_chunks, DV_AUG, KV_CHUNK), lambda b, h, i: (b, h, 0, 0, 0)),
            pl.BlockSpec((1, tq, 2 * LANES), lambda b, h, i: (b, i, h)),
            pl.BlockSpec((1, 1, 2 * WINDOW), lambda b, h, i: (h, 0, 0)),
        ],
        out_specs=pl.BlockSpec((1, tq, 2 * LANES), lambda b, h, i: (b, i, h)),
        out_shape=jax.ShapeDtypeStruct((bsz, s, GROUP_WIDTH), BF16),
        scratch_shapes=[pltpu.VMEM((tc + 3 * WINDOW, 2 * WINDOW), F32)] * 2,
        compiler_params=_params(3),
        name="attn_a_band",
    )(q, kc, vtc, kl, vtl, gate, sink)


def _out_kernel(*refs, final):
    ua, ub, uc, ud, w_ref, x_ref, gate_ref = refs[:7]
    out_ref = refs[-1]
    acc = (_dotf(ua[0], w_ref[0]) + _dotf(ub[0], w_ref[1])
           + _dotf(uc[0], w_ref[2]) + _dotf(ud[0], w_ref[3]))
    xn = x_ref[0] + gate_ref[0, 0] * acc
    if final:
        xn = _rms(xn, 1.0 / D_MODEL) * refs[7][...]
    out_ref[0] = xn


def _out_project(us, w_out, x, mods, layer, row_fn, tm, final_g=None):
    bsz, t, d = x.shape
    u_spec = pl.BlockSpec((1, tm, GROUP_WIDTH), lambda b, i: (b, i, 0))
    specs = [u_spec] * 4 + [_full_spec(w_out), pl.BlockSpec((1, tm, d), lambda b, i: (b, i, 0)),
                            _mod_spec(layer, row_fn, 2)]
    args = list(us) + [w_out, x, mods]
    if final_g is not None:
        specs.append(pl.BlockSpec((1, d), lambda b, i: (0, 0)))
        args.append(final_g)
    return pl.pallas_call(
        functools.partial(_out_kernel, final=final_g is not None),
        grid=(bsz, t // tm),
        in_specs=specs,
        out_specs=pl.BlockSpec((1, tm, d), lambda b, i: (b, i, 0)),
        out_shape=jax.ShapeDtypeStruct((bsz, t, d), F32),
        compiler_params=_params(2),
        name="out_proj",
    )(*args)


def _rope_tables(rows, rot_dim):
    row = jnp.broadcast_to(jnp.arange(rows)[:, None], (rows, GRID_W)).reshape(-1).astype(F32)
    col = jnp.broadcast_to(jnp.arange(GRID_W)[None, :], (rows, GRID_W)).reshape(-1).astype(F32)
    axis_dim = rot_dim // 2
    inv_freq = ROPE_THETA ** (-jnp.arange(0, axis_dim, 2, dtype=F32) / axis_dim)
    ang_r = row[:, None] * inv_freq[None, :]
    ang_c = col[:, None] * inv_freq[None, :]
    ang = jnp.concatenate([ang_r, ang_r, ang_c, ang_c], axis=-1)
    quarter = rot_dim // 4
    sign = jnp.where((jnp.arange(rot_dim) // quarter) % 2 == 0, -1.0, 1.0).astype(F32)
    cos, sin = jnp.cos(ang), jnp.sin(ang) * sign
    reps = LANES // rot_dim
    return jnp.tile(cos, (1, reps)), jnp.tile(sin, (1, reps))


def _layer_weights(w_in, w_uq, w_ukv, gq, gkv):
    wa = w_in[:, 0:1536]
    wb = w_in[:, 1536:3584]
    zeros64 = jnp.zeros((w_in.shape[0], 64), w_in.dtype)
    wc = jnp.concatenate([w_in[:, 3584:4032], zeros64,
                          w_in[:, 4032:4160],
                          w_in[:, 4160:4224], zeros64,
                          w_in[:, 4224:4736]], axis=1)
    wd = w_in[:, 4736:6272]
    uq = w_uq.reshape(C_Q_LORA, 4, C_QK_DIM)
    uq = jnp.concatenate([uq[:, :, :128].reshape(C_Q_LORA, 512),
                          uq[:, :, 128:].reshape(C_Q_LORA, 256)], axis=1)
    uq = jnp.concatenate([uq, jnp.zeros((64, 768), uq.dtype)], axis=0)
    gq_pad = jnp.concatenate([gq, jnp.zeros((64,), gq.dtype)])[None, :]
    cast = lambda a: a.astype(BF16)
    return cast(wa), cast(wb), cast(wc), cast(wd), cast(uq), cast(w_ukv), gq_pad, gkv[None, :]


def kernel(x, c, ctx, c_ctx, w_mod, b_mod, norm_g, w_in, c_q_norm_g, c_kv_norm_g, c_w_uq, c_w_ukv,
           d_q_norm_g, d_k_norm_g, a_sink, b_lambda, b_subln_g, w_out, final_norm_g):
    bsz, s, d = x.shape
    tc = ctx.shape[1]
    depth = w_mod.shape[0]
    assert d == D_MODEL and s % TM_LATENT == 0 and s % GRID_W == 0 and bsz + 1 <= MOD_ROWS
    assert tc % LANES == 0 and TM_LATENT % KV_CHUNK == 0 and tc <= KV_CHUNK

    cc = jnp.concatenate([c, c_ctx[None, :], jnp.zeros((MOD_ROWS - bsz - 1, d), c.dtype)], axis=0)
    mods = _modulation(cc, w_mod, b_mod).reshape(depth, MOD_ROWS, 1, 3 * d)
    lat_row = lambda b: b
    ctx_row = lambda b: bsz

    rows = s // GRID_W
    cos_h, sin_h = _rope_tables(rows, HEAD_DIM)
    cos_b, sin_b = _rope_tables(rows, B_QK_DIM)
    one = jnp.ones((tc, LANES), F32)
    zero = jnp.zeros((tc, LANES), F32)
    tab_lat = {"h": (cos_h, sin_h), "b": (cos_b, sin_b)}
    tab_ctx = {"h": (one, zero), "b": (one, zero)}

    tq2, tq1 = min(N_DENSE // 2, s), min(N_DENSE, s)
    for l in range(depth):
        last = l == depth - 1
        wa, wb, wc, wd, uq, ukv, gq_pad, gkv = _layer_weights(
            w_in[l], c_w_uq[l], c_w_ukv[l], c_q_norm_g[l], c_kv_norm_g[l])
        dq_g, dk_g = d_q_norm_g[l][None, :], d_k_norm_g[l][None, :]
        wo = w_out[l].astype(BF16).reshape(4, GROUP_WIDTH, d)
        lam_init = 0.8 - 0.6 * math.exp(-0.3 * l)
        diff = (b_lambda[l], b_subln_g[l][None, :], lam_init)
        sink2 = a_sink[l].reshape(2, 2)

        def project(stream, row_fn, tabs, tm):
            norm = (norm_g[l][None, :], mods, l, row_fn)
            pa = _project(_proj_a_kernel, "proj_a", stream, norm, [wa], tabs["h"], tm, 4, 128, 2, 128)
            pb = _project(_proj_b_kernel, "proj_b", stream, norm, [wb], tabs["b"], tm, 8, 128, 4, 128)
            pc = _project(_proj_c_kernel, "proj_c", stream, norm, [wc, uq, ukv, gq_pad, gkv], tabs["b"],
                          tm, 4, 256, 4, 256)
            pd = _project(_proj_d_kernel, "proj_d", stream, norm, [wd, dq_g, dk_g], tabs["h"],
                          min(tm, TM_QKNORM), 4, 128, 2, 128)
            return pa, pb, pc, pd

        def sink_rows(tq):
            return jnp.repeat(sink2, tq, axis=1).reshape(2, 1, 2 * tq)

        (qa, ka, vta, gza), (qb, kb, vtb, gzb), (qc, kc, vtc, gzc), (qd, kd, vtd, gzd) = project(
            x, lat_row, tab_lat, TM_LATENT)
        (qa_c, ka_c, vta_c, gza_c), (qb_c, kb_c, vtb_c, gzb_c), (qc_c, kc_c, vtc_c, gzc_c), \
            (qd_c, kd_c, vtd_c, gzd_c) = project(ctx, ctx_row, tab_ctx, tc)

        ua = _band_attention(qa, ka_c, vta_c, ka, vta, gza, sink_rows(WINDOW))
        ub = _dense_attention("attn_b", qb, kb_c, vtb_c, kb, vtb, gzb, 2, tq2, diff=diff)
        uc = _dense_attention("attn_c", qc, kc_c, vtc_c, kc, vtc, gzc, 1, tq1)
        ud = _dense_attention("attn_d", qd, kd_c, vtd_c, kd, vtd, gzd, 2, tq2)
        if not last:
            ua_c = _dense_attention("attn_a_ctx", qa_c, ka_c, vta_c, None, None, gza_c, 2, tc,
                                    sink=sink_rows(tc))
            ub_c = _dense_attention("attn_b_ctx", qb_c, kb_c, vtb_c, None, None, gzb_c, 2, tc, diff=diff)
            uc_c = _dense_attention("attn_c_ctx", qc_c, kc_c, vtc_c, None, None, gzc_c, 1, tc)
            ud_c = _dense_attention("attn_d_ctx", qd_c, kd_c, vtd_c, None, None, gzd_c, 2, tc)
            ctx = _out_project((ua_c, ub_c, uc_c, ud_c), wo, ctx, mods, l, ctx_row, tc)
        x = _out_project((ua, ub, uc, ud), wo, x, mods, l, lat_row, TM_OUT,
                         final_g=final_norm_g[None, :] if last else None)
    return x
```

```python
import functools
import math

import jax
import jax.numpy as jnp
from jax import lax
from jax.experimental import pallas as pl
from jax.experimental.pallas import tpu as pltpu

F32 = jnp.float32
BF16 = jnp.bfloat16

D_MODEL = 2048
HEAD_DIM = 128
GRID_W = 64
GROUP_WIDTH = 512
WINDOW = 128
ROPE_THETA = 10000.0
EPS = 1e-6
NEG_INF = -1e30
C_Q_LORA = 448
C_ROPE = 64
C_QK_DIM = 192
B_QK_DIM = 64
LOG2E = math.log2(math.e)
LANES = 128
ONES_ROWS = 16
DV_AUG = HEAD_DIM + ONES_ROWS
MOD_ROWS = 16
VMEM_LIMIT = 56 * 1024 * 1024

TM_LATENT = 1024
TM_QKNORM = 512
TM_OUT = 512
KV_CHUNK = 512
N_DENSE = 4096
TQ_BAND = 1024
MOD_TN = 512
COL_TILE = 256


def _dotf(a, b):
    return jnp.dot(a, b, preferred_element_type=F32)


def _dot_nt(a, b):
    return lax.dot_general(a, b, (((1,), (1,)), ((), ())), preferred_element_type=F32)


def _silu(z):
    return z / (1.0 + jnp.exp(-z))


def _rms(y, inv_n):
    return y * lax.rsqrt(jnp.sum(y * y, axis=-1, keepdims=True) * inv_n + EPS)


def _rope(x, cos, sin_signed, shift):
    lane = lax.broadcasted_iota(jnp.int32, x.shape, 1)
    even = (lane & shift) == 0
    x_up = pltpu.roll(x, LANES - shift, 1)
    x_dn = pltpu.roll(x, shift, 1)
    return x * cos + jnp.where(even, x_up, x_dn) * sin_signed


def _params(n_axes):
    return pltpu.CompilerParams(dimension_semantics=("parallel",) * n_axes,
                                vmem_limit_bytes=VMEM_LIMIT)


def _mod_kernel(c_ref, w_ref, b_ref, o_ref):
    a = _silu(c_ref[...]).astype(BF16)
    o_ref[0] = _dotf(a, w_ref[0].astype(BF16)) + b_ref[0]


def _modulation(cc, w_mod, b_mod):
    depth, d, n = w_mod.shape
    return pl.pallas_call(
        _mod_kernel,
        grid=(depth, n // MOD_TN),
        in_specs=[pl.BlockSpec((MOD_ROWS, d), lambda l, j: (0, 0)),
                  pl.BlockSpec((1, d, MOD_TN), lambda l, j: (l, 0, j)),
                  pl.BlockSpec((1, 1, MOD_TN), lambda l, j: (l, 0, j))],
        out_specs=pl.BlockSpec((1, MOD_ROWS, MOD_TN), lambda l, j: (l, 0, j)),
        out_shape=jax.ShapeDtypeStruct((depth, MOD_ROWS, n), F32),
        compiler_params=_params(2),
        name="modulation",
    )(cc, w_mod, b_mod.reshape(depth, 1, n))


def _prenorm(x_ref, g_ref, shift_ref, scale_ref):
    y = _rms(x_ref[0], 1.0 / D_MODEL) * g_ref[...]
    return (y * (1.0 + scale_ref[0, 0]) + shift_ref[0, 0]).astype(BF16)


def _mod_spec(layer, row_fn, which):
    return pl.BlockSpec((1, 1, 1, D_MODEL), lambda b, i: (layer, row_fn(b), 0, which))


def _store_vt(vt_ref, head, y):
    chunk = vt_ref.shape[-1]
    for j in range(vt_ref.shape[2]):
        vt_ref[0, head, j, 0:HEAD_DIM, :] = y[j * chunk:(j + 1) * chunk, :].T.astype(BF16)
        vt_ref[0, head, j, HEAD_DIM:DV_AUG, :] = jnp.ones((ONES_ROWS, chunk), BF16)


def _proj_a_kernel(x_ref, g_ref, shift_ref, scale_ref, w_ref, cos_ref, sin_ref,
                   q_ref, k_ref, vt_ref, gz_ref):
    h = _prenorm(x_ref, g_ref, shift_ref, scale_ref)
    cos, sin = cos_ref[...], sin_ref[...]
    scale = LOG2E / math.sqrt(HEAD_DIM)
    yq = _dotf(h, w_ref[:, 0:512])
    for hh in range(4):
        q_ref[0, hh] = (_rope(yq[:, hh * 128:(hh + 1) * 128], cos, sin, 32) * scale).astype(BF16)
    ykv = _dotf(h, w_ref[:, 512:1024])
    for hh in range(2):
        k_ref[0, hh] = _rope(ykv[:, hh * 128:(hh + 1) * 128], cos, sin, 32).astype(BF16)
        _store_vt(vt_ref, hh, ykv[:, 256 + hh * 128:256 + (hh + 1) * 128])
    gz_ref[0] = _silu(_dotf(h, w_ref[:, 1024:1536])).astype(BF16)


def _proj_b_kernel(x_ref, g_ref, shift_ref, scale_ref, w_ref, cos_ref, sin_ref,
                   q_ref, k_ref, vt_ref, gz_ref):
    h = _prenorm(x_ref, g_ref, shift_ref, scale_ref)
    cos, sin = cos_ref[...], sin_ref[...]
    scale = LOG2E / math.sqrt(B_QK_DIM)
    yq = _dotf(h, w_ref[:, 0:512])
    lane = lax.broadcasted_iota(jnp.int32, (h.shape[0], LANES), 1)
    first = lane < B_QK_DIM
    for hh in range(4):
        y = _rope(yq[:, hh * 128:(hh + 1) * 128], cos, sin, 16) * scale
        q_ref[0, 2 * hh] = jnp.where(first, y, 0.0).astype(BF16)
        q_ref[0, 2 * hh + 1] = jnp.where(first, 0.0, y).astype(BF16)
    yk = _dotf(h, w_ref[:, 512:1024])
    for hh in range(4):
        k_ref[0, hh] = _rope(yk[:, hh * 128:(hh + 1) * 128], cos, sin, 16).astype(BF16)
    yv = _dotf(h, w_ref[:, 1024:1536])
    for hh in range(4):
        _store_vt(vt_ref, hh, yv[:, hh * 128:(hh + 1) * 128])
    gz_ref[0] = _silu(_dotf(h, w_ref[:, 1536:2048])).astype(BF16)


def _proj_c_kernel(x_ref, g_ref, shift_ref, scale_ref, w_ref, wuq_ref, wukv_ref, gq_ref, gkv_ref,
                   cos_ref, sin_ref, q_ref, k_ref, vt_ref, gz_ref):
    h = _prenorm(x_ref, g_ref, shift_ref, scale_ref)
    cos, sin = cos_ref[...], sin_ref[...]
    scale = LOG2E / math.sqrt(C_QK_DIM)
    ycq = _dotf(h, w_ref[:, 0:512])
    yckv = _dotf(h, w_ref[:, 512:768])
    gz_ref[0] = _silu(_dotf(h, w_ref[:, 768:1280])).astype(BF16)
    cqn = (_rms(ycq, 1.0 / C_Q_LORA) * gq_ref[...]).astype(BF16)
    yq = _dotf(cqn, wuq_ref[...])
    ckvn = (_rms(yckv[:, 0:128], 1.0 / 128) * gkv_ref[...]).astype(BF16)
    ykv = _dotf(ckvn, wukv_ref[...])
    kr_even = _rope(yckv[:, 128:256], cos, sin, 16)
    kr_odd = pltpu.roll(kr_even, C_ROPE, 1)
    qr = [_rope(yq[:, 512 + p * 128:512 + (p + 1) * 128], cos, sin, 16) * scale for p in range(2)]
    for hh in range(4):
        q_ref[0, hh, :, 0:128] = (yq[:, hh * 128:(hh + 1) * 128] * scale).astype(BF16)
        q_ref[0, hh, :, 128:256] = qr[hh // 2].astype(BF16)
        k_ref[0, hh, :, 0:128] = ykv[:, hh * 256:hh * 256 + 128].astype(BF16)
        k_ref[0, hh, :, 128:256] = (kr_even if hh % 2 == 0 else kr_odd).astype(BF16)
        _store_vt(vt_ref, hh, ykv[:, hh * 256 + 128:(hh + 1) * 256])


def _proj_d_kernel(x_ref, g_ref, shift_ref, scale_ref, w_ref, gq_ref, gk_ref, cos_ref, sin_ref,
                   q_ref, k_ref, vt_ref, gz_ref):
    h = _prenorm(x_ref, g_ref, shift_ref, scale_ref)
    cos, sin = cos_ref[...], sin_ref[...]
    scale = LOG2E / math.sqrt(HEAD_DIM)
    yq = _dotf(h, w_ref[:, 0:512])
    for hh in range(4):
        yn = _rms(yq[:, hh * 128:(hh + 1) * 128], 1.0 / HEAD_DIM) * gq_ref[...]
        q_ref[0, hh] = (_rope(yn, cos, sin, 32) * scale).astype(BF16)
    yk = _dotf(h, w_ref[:, 512:768])
    for hh in range(2):
        yn = _rms(yk[:, hh * 128:(hh + 1) * 128], 1.0 / HEAD_DIM) * gk_ref[...]
        k_ref[0, hh] = _rope(yn, cos, sin, 32).astype(BF16)
    gz_ref[0] = _silu(_dotf(h, w_ref[:, 1024:1536])).astype(BF16)
    yv = _dotf(h, w_ref[:, 768:1024])
    for hh in range(2):
        _store_vt(vt_ref, hh, yv[:, hh * 128:(hh + 1) * 128])


def _full_spec(a):
    nd = a.ndim
    return pl.BlockSpec(a.shape, lambda b, i: (0,) * nd)


def _project(kern, name, x, norm, weights, tables, tm, n_q, dq, n_kv, dk):
    bsz, t, d = x.shape
    nt = t // tm
    chunk = min(tm, KV_CHUNK)
    sub = tm // chunk
    gain, mods, layer, row_fn = norm
    tab_specs = [pl.BlockSpec((tm, LANES), (lambda b, i: (i, 0)) if tab.shape[0] == t
                              else (lambda b, i: (0, 0))) for tab in tables]
    return pl.pallas_call(
        kern,
        grid=(bsz, nt),
        in_specs=[pl.BlockSpec((1, tm, d), lambda b, i: (b, i, 0)),
                  pl.BlockSpec((1, d), lambda b, i: (0, 0)),
                  _mod_spec(layer, row_fn, 0), _mod_spec(layer, row_fn, 1)]
        + [_full_spec(w) for w in weights] + tab_specs,
        out_specs=[pl.BlockSpec((1, n_q, tm, dq), lambda b, i: (b, 0, i, 0)),
                   pl.BlockSpec((1, n_kv, tm, dk), lambda b, i: (b, 0, i, 0)),
                   pl.BlockSpec((1, n_kv, sub, DV_AUG, chunk), lambda b, i: (b, 0, i, 0, 0)),
                   pl.BlockSpec((1, tm, GROUP_WIDTH), lambda b, i: (b, i, 0))],
        out_shape=[jax.ShapeDtypeStruct((bsz, n_q, t, dq), BF16),
                   jax.ShapeDtypeStruct((bsz, n_kv, t, dk), BF16),
                   jax.ShapeDtypeStruct((bsz, n_kv, nt * sub, DV_AUG, chunk), BF16),
                   jax.ShapeDtypeStruct((bsz, t, GROUP_WIDTH), BF16)],
        compiler_params=_params(2),
        name=name,
    )(x, gain, mods, mods, *weights, *tables)


def _finish_plain(acc, l, gate_ref, out_ref, n_sets, tq):
    o_t = acc * (1.0 / l)
    for g in range(n_sets):
        o = o_t[:, g * tq:(g + 1) * tq].astype(BF16).T
        out_ref[0, :, g * 128:(g + 1) * 128] = o * gate_ref[0, :, g * 128:(g + 1) * 128]


def _finish_diff(acc, l, gate_ref, out_ref, lam_ref, subg_ref, lam_init, tq):
    o_t = acc * (1.0 / l)
    lp = lam_ref[...]
    s1 = jnp.sum(lp[0:1] * lp[1:2], axis=1, keepdims=True)
    s2 = jnp.sum(lp[2:3] * lp[3:4], axis=1, keepdims=True)
    lam = jnp.exp(s1) - jnp.exp(s2) + lam_init
    d = o_t[:, 0:tq] - lam * o_t[:, tq:2 * tq]
    dn = d * lax.rsqrt(jnp.mean(d * d, axis=0, keepdims=True) + EPS)
    o = dn.T * subg_ref[...] * (1.0 - lam_init)
    out_ref[0] = (o * gate_ref[0].astype(F32)).astype(BF16)


def _dense_attn_kernel(*refs, n_sets, tq, n_chunks, has_sink, diff, lam_init):
    refs = list(refs)
    q_ref, kc_ref, vtc_ref = refs[:3]
    pos = 3
    if n_chunks:
        kl_ref, vtl_ref = refs[pos:pos + 2]
        pos += 2
    gate_ref = refs[pos]
    pos += 1
    if has_sink:
        sink_ref = refs[pos]
        pos += 1
    if diff:
        lam_ref, subg_ref = refs[pos:pos + 2]
        pos += 2
    out_ref, acc_ref = refs[pos:pos + 2]
    s_refs = refs[pos + 2:pos + 4]

    n = n_sets * tq
    n_tiles = n // COL_TILE
    per_set = tq // COL_TILE
    tc = kc_ref.shape[2]
    acc_ref[0:HEAD_DIM, :] = jnp.zeros((HEAD_DIM, n), F32)
    if has_sink:
        m0 = sink_ref[0] * LOG2E
        acc_ref[HEAD_DIM:DV_AUG, :] = jnp.ones((ONES_ROWS, n), F32)
    else:
        m0 = jnp.full((1, n), NEG_INF, F32)
        acc_ref[HEAD_DIM:DV_AUG, :] = jnp.zeros((ONES_ROWS, n), F32)

    def cols(c):
        return slice(c * COL_TILE, (c + 1) * COL_TILE)

    def scores(k, c, s_ref, rows):
        qt = q_ref[0, c // per_set, (c % per_set) * COL_TILE:(c % per_set + 1) * COL_TILE, :]
        s = _dot_nt(k, qt)
        s_ref[0:rows, cols(c)] = s
        return jnp.max(s, axis=0, keepdims=True)

    def softmax_pv(s_ref, rows, cmax, vt, m, c):
        m_new = jnp.maximum(m, cmax)
        alpha = jnp.exp2(m - m_new)
        p = jnp.exp2(s_ref[0:rows, cols(c)] - m_new)
        pv = _dotf(vt, p.astype(BF16))
        acc_ref[:, cols(c)] = alpha * acc_ref[:, cols(c)] + pv
        return m_new

    def stage(k_next, s_next, s_cur, rows_cur, cmax_cur, vt_cur, m):
        ms, cm = [], []
        for c in range(n_tiles):
            if k_next is not None:
                cm.append(scores(k_next, c, s_next, KV_CHUNK))
            if s_cur is not None:
                ms.append(softmax_pv(s_cur, rows_cur, cmax_cur[:, cols(c)], vt_cur, m[:, cols(c)], c))
        cat = lambda xs: jnp.concatenate(xs, axis=1) if xs else None
        return cat(cm), cat(ms)

    s_a, s_b = s_refs
    k_ctx = kc_ref[0, 0]
    cmax = jnp.concatenate([scores(k_ctx, c, s_a, tc) for c in range(n_tiles)], axis=1)
    if not n_chunks:
        _, m = stage(None, None, s_a, tc, cmax, vtc_ref[0, 0, 0], m0)
    else:
        def k_chunk(j):
            start = pl.multiple_of(j * KV_CHUNK, KV_CHUNK)
            return kl_ref[0, 0, pl.ds(start, KV_CHUNK), :]

        cmax, m = stage(k_chunk(0), s_b, s_a, tc, cmax, vtc_ref[0, 0, 0], m0)

        def body(i, carry):
            cmax, m = carry
            j = 2 * i
            cmax, m = stage(k_chunk(j + 1), s_a, s_b, KV_CHUNK, cmax, vtl_ref[0, 0, j], m)
            return stage(k_chunk(j + 2), s_b, s_a, KV_CHUNK, cmax, vtl_ref[0, 0, j + 1], m)

        cmax, m = lax.fori_loop(0, n_chunks // 2 - 1, body, (cmax, m))
        j = n_chunks - 2
        cmax, m = stage(k_chunk(j + 1), s_a, s_b, KV_CHUNK, cmax, vtl_ref[0, 0, j], m)
        _, m = stage(None, None, s_a, KV_CHUNK, cmax, vtl_ref[0, 0, j + 1], m)

    acc = acc_ref[0:HEAD_DIM, :]
    l = acc_ref[HEAD_DIM:HEAD_DIM + 1, :]
    if diff:
        _finish_diff(acc, l, gate_ref, out_ref, lam_ref, subg_ref, lam_init, tq)
    else:
        _finish_plain(acc, l, gate_ref, out_ref, n_sets, tq)


def _dense_attention(name, q, kc, vtc, kl, vtl, gate, n_sets, tq, sink=None, diff=None):
    bsz, _, t, dk = q.shape
    hkv, tc = kc.shape[1], kc.shape[2]
    dv = vtc.shape[3]
    n_chunks = 0 if kl is None else kl.shape[2] // KV_CHUNK
    assert n_chunks % 2 == 0 and tq % COL_TILE == 0
    out_w = LANES if diff else n_sets * LANES
    n = n_sets * tq
    args = [q, kc, vtc]
    specs = [pl.BlockSpec((1, n_sets, tq, dk), lambda b, h, i: (b, h, i, 0)),
             pl.BlockSpec((1, 1, tc, dk), lambda b, h, i: (b, h, 0, 0)),
             pl.BlockSpec((1, 1, 1, dv, tc), lambda b, h, i: (b, h, 0, 0, 0))]
    if n_chunks:
        args += [kl, vtl]
        specs += [pl.BlockSpec((1, 1, kl.shape[2], dk), lambda b, h, i: (b, h, 0, 0)),
                  pl.BlockSpec((1, 1, n_chunks, dv, KV_CHUNK), lambda b, h, i: (b, h, 0, 0, 0))]
    args.append(gate)
    specs.append(pl.BlockSpec((1, tq, out_w), lambda b, h, i: (b, i, h)))
    if sink is not None:
        args.append(sink)
        specs.append(pl.BlockSpec((1, 1, n), lambda b, h, i: (h, 0, 0)))
    lam_init = 0.0
    if diff is not None:
        lam_par, subg, lam_init = diff
        args += [lam_par, subg]
        specs += [pl.BlockSpec(lam_par.shape, lambda b, h, i: (0, 0)),
                  pl.BlockSpec(subg.shape, lambda b, h, i: (0, 0))]
    kern = functools.partial(_dense_attn_kernel, n_sets=n_sets, tq=tq, n_chunks=n_chunks,
                             has_sink=sink is not None, diff=diff is not None, lam_init=lam_init)
    return pl.pallas_call(
        kern,
        grid=(bsz, hkv, t // tq),
        in_specs=specs,
        out_specs=pl.BlockSpec((1, tq, out_w), lambda b, h, i: (b, i, h)),
        out_shape=jax.ShapeDtypeStruct((bsz, t, GROUP_WIDTH), BF16),
        scratch_shapes=[pltpu.VMEM((dv, n), F32)] + [pltpu.VMEM((max(KV_CHUNK, tc), n), F32)] * 2,
        compiler_params=_params(3),
        name=name,
    )(*args)


def _band_attn_kernel(q_ref, kc_ref, vtc_ref, kl_ref, vtl_ref, gate_ref, sink_ref, out_ref, s_a, s_b,
                      *, blocks, n_chunks):
    w = WINDOW
    per_chunk = KV_CHUNK // w
    tc = kc_ref.shape[2]
    i = pl.program_id(2)
    first_step = i == 0
    last_step = i == pl.num_programs(2) - 1
    r2 = lax.broadcasted_iota(jnp.int32, (2 * w, 1), 0)
    r1 = lax.broadcasted_iota(jnp.int32, (w, 1), 0)
    qc = lax.broadcasted_iota(jnp.int32, (1, 2 * w), 1) & (w - 1)
    sink = sink_ref[0] * LOG2E
    k_ctx = kc_ref[0, 0]

    def pieces(t):
        chunk = i * (blocks // per_chunk) + t // per_chunk
        tt = t % per_chunk
        if tt < per_chunk - 1:
            pair = (chunk * KV_CHUNK + tt * w, chunk, tt * w, (r2 - qc) <= w, None)
            if tt > 0:
                single = (chunk * KV_CHUNK + (tt - 1) * w, chunk, (tt - 1) * w, r1 >= qc, None)
            else:
                prev_chunk = jnp.maximum(chunk - 1, 0)
                bias = jnp.where(first_step, NEG_INF, 0.0) if t == 0 else None
                single = (prev_chunk * KV_CHUNK + (per_chunk - 1) * w, prev_chunk, (per_chunk - 1) * w,
                          r1 >= qc, bias)
        else:
            pair = (chunk * KV_CHUNK + (tt - 1) * w, chunk, (tt - 1) * w, r2 >= qc, None)
            next_chunk = jnp.minimum(chunk + 1, n_chunks - 1)
            bias = jnp.where(last_step, NEG_INF, 0.0) if t == blocks - 1 else None
            single = (next_chunk * KV_CHUNK, next_chunk, 0, r1 <= qc, bias)
        return pair, single

    def q_block(t):
        rows = slice(t * w, (t + 1) * w)
        return jnp.concatenate([q_ref[0, 0, rows, :], q_ref[0, 1, rows, :]], axis=0)

    def scores(t, s_ref):
        qt = q_block(t)
        s_ctx = _dot_nt(k_ctx, qt)
        s_ref[0:tc, :] = s_ctx
        cmax = jnp.max(s_ctx, axis=0, keepdims=True)
        row0 = tc
        for (start, _, _, mask, bias), rows in zip(pieces(t), (2 * w, w)):
            k = kl_ref[0, 0, pl.ds(pl.multiple_of(start, w), rows), :]
            s = _dot_nt(k, qt)
            if bias is not None:
                s = s + bias
            s = jnp.where(mask, s, NEG_INF)
            s_ref[row0:row0 + rows, :] = s
            cmax = jnp.maximum(cmax, jnp.max(s, axis=0, keepdims=True))
            row0 += rows
        return cmax

    def softmax_pv(t, s_ref, cmax):
        m = jnp.maximum(sink, cmax)
        acc = _dotf(vtc_ref[0, 0, 0], jnp.exp2(s_ref[0:tc, :] - m).astype(BF16))
        row0 = tc
        for (_, chunk, lane0, _, _), rows in zip(pieces(t), (2 * w, w)):
            vt = vtl_ref[0, 0, chunk, :, lane0:lane0 + rows]
            acc = acc + _dotf(vt, jnp.exp2(s_ref[row0:row0 + rows, :] - m).astype(BF16))
            row0 += rows
        l = acc[HEAD_DIM:HEAD_DIM + 1, :] + jnp.exp2(sink - m)
        o_t = acc[0:HEAD_DIM, :] * (1.0 / l)
        rows = slice(t * w, (t + 1) * w)
        for g in range(2):
            gcols = slice(g * LANES, (g + 1) * LANES)
            o = o_t[:, g * w:(g + 1) * w].T
            out_ref[0, rows, gcols] = (o * gate_ref[0, rows, gcols].astype(F32)).astype(BF16)

    bufs = (s_a, s_b)
    cmax = scores(0, bufs[0])
    for t in range(blocks):
        cmax_next = scores(t + 1, bufs[(t + 1) % 2]) if t + 1 < blocks else None
        softmax_pv(t, bufs[t % 2], cmax)
        cmax = cmax_next


def _band_attention(q, kc, vtc, kl, vtl, gate, sink):
    bsz, _, s, dk = q.shape
    tq = min(TQ_BAND, s)
    tc = kc.shape[2]
    n_chunks = s // KV_CHUNK
    assert tq % KV_CHUNK == 0 and dk == HEAD_DIM
    return pl.pallas_call(
        functools.partial(_band_attn_kernel, blocks=tq // WINDOW, n_chunks=n_chunks),
        grid=(bsz, 2, s // tq),
        in_specs=[
            pl.BlockSpec((1, 2, tq, dk), lambda b, h, i: (b, h, i, 0)),
            pl.BlockSpec((1, 1, tc, dk), lambda b, h, i: (b, h, 0, 0)),
            pl.BlockSpec((1, 1, 1, DV_AUG, tc), lambda b, h, i: (b, h, 0, 0, 0)),
            pl.BlockSpec((1, 1, s, dk), lambda b, h, i: (b, h, 0, 0)),
            pl.BlockSpec((1, 1, n_chunks, DV_AUG, KV_CHUNK), lambda b, h, i: (b, h, 0, 0, 0)),
            pl.BlockSpec((1, tq, 2 * LANES), lambda b, h, i: (b, i, h)),
            pl.BlockSpec((1, 1, 2 * WINDOW), lambda b, h, i: (h, 0, 0)),
        ],
        out_specs=pl.BlockSpec((1, tq, 2 * LANES), lambda b, h, i: (b, i, h)),
        out_shape=jax.ShapeDtypeStruct((bsz, s, GROUP_WIDTH), BF16),
        scratch_shapes=[pltpu.VMEM((tc + 3 * WINDOW, 2 * WINDOW), F32)] * 2,
        compiler_params=_params(3),
        name="attn_a_band",
    )(q, kc, vtc, kl, vtl, gate, sink)


def _out_kernel(*refs, final):
    ua, ub, uc, ud, w_ref, x_ref, gate_ref = refs[:7]
    out_ref = refs[-1]
    u = jnp.concatenate([ua[0], ub[0], uc[0], ud[0]], axis=1)
    acc = _dotf(u, w_ref[...].reshape(4 * GROUP_WIDTH, D_MODEL))
    xn = x_ref[0] + gate_ref[0, 0] * acc
    if final:
        xn = _rms(xn, 1.0 / D_MODEL) * refs[7][...]
    out_ref[0] = xn


def _out_project(us, w_out, x, mods, layer, row_fn, tm, final_g=None):
    bsz, t, d = x.shape
    u_spec = pl.BlockSpec((1, tm, GROUP_WIDTH), lambda b, i: (b, i, 0))
    specs = [u_spec] * 4 + [_full_spec(w_out), pl.BlockSpec((1, tm, d), lambda b, i: (b, i, 0)),
                            _mod_spec(layer, row_fn, 2)]
    args = list(us) + [w_out, x, mods]
    if final_g is not None:
        specs.append(pl.BlockSpec((1, d), lambda b, i: (0, 0)))
        args.append(final_g)
    return pl.pallas_call(
        functools.partial(_out_kernel, final=final_g is not None),
        grid=(bsz, t // tm),
        in_specs=specs,
        out_specs=pl.BlockSpec((1, tm, d), lambda b, i: (b, i, 0)),
        out_shape=jax.ShapeDtypeStruct((bsz, t, d), F32),
        compiler_params=_params(2),
        name="out_proj",
    )(*args)


def _rope_tables(rows, rot_dim):
    row = jnp.broadcast_to(jnp.arange(rows)[:, None], (rows, GRID_W)).reshape(-1).astype(F32)
    col = jnp.broadcast_to(jnp.arange(GRID_W)[None, :], (rows, GRID_W)).reshape(-1).astype(F32)
    axis_dim = rot_dim // 2
    inv_freq = ROPE_THETA ** (-jnp.arange(0, axis_dim, 2, dtype=F32) / axis_dim)
    ang_r = row[:, None] * inv_freq[None, :]
    ang_c = col[:, None] * inv_freq[None, :]
    ang = jnp.concatenate([ang_r, ang_r, ang_c, ang_c], axis=-1)
    quarter = rot_dim // 4
    sign = jnp.where((jnp.arange(rot_dim) // quarter) % 2 == 0, -1.0, 1.0).astype(F32)
    cos, sin = jnp.cos(ang), jnp.sin(ang) * sign
    reps = LANES // rot_dim
    return jnp.tile(cos, (1, reps)), jnp.tile(sin, (1, reps))


def _layer_weights(w_in, w_uq, w_ukv, gq, gkv):
    wa = w_in[:, 0:1536]
    wb = w_in[:, 1536:3584]
    zeros64 = jnp.zeros((w_in.shape[0], 64), w_in.dtype)
    wc = jnp.concatenate([w_in[:, 3584:4032], zeros64,
                          w_in[:, 4032:4160],
                          w_in[:, 4160:4224], zeros64,
                          w_in[:, 4224:4736]], axis=1)
    wd = w_in[:, 4736:6272]
    uq = w_uq.reshape(C_Q_LORA, 4, C_QK_DIM)
    uq = jnp.concatenate([uq[:, :, :128].reshape(C_Q_LORA, 512),
                          uq[:, :, 128:].reshape(C_Q_LORA, 256)], axis=1)
    uq = jnp.concatenate([uq, jnp.zeros((64, 768), uq.dtype)], axis=0)
    gq_pad = jnp.concatenate([gq, jnp.zeros((64,), gq.dtype)])[None, :]
    cast = lambda a: a.astype(BF16)
    return cast(wa), cast(wb), cast(wc), cast(wd), cast(uq), cast(w_ukv), gq_pad, gkv[None, :]


def kernel(x, c, ctx, c_ctx, w_mod, b_mod, norm_g, w_in, c_q_norm_g, c_kv_norm_g, c_w_uq, c_w_ukv,
           d_q_norm_g, d_k_norm_g, a_sink, b_lambda, b_subln_g, w_out, final_norm_g):
    bsz, s, d = x.shape
    tc = ctx.shape[1]
    depth = w_mod.shape[0]
    assert d == D_MODEL and s % TM_LATENT == 0 and s % GRID_W == 0 and bsz + 1 <= MOD_ROWS
    assert tc % LANES == 0 and TM_LATENT % KV_CHUNK == 0 and tc <= KV_CHUNK

    cc = jnp.concatenate([c, c_ctx[None, :], jnp.zeros((MOD_ROWS - bsz - 1, d), c.dtype)], axis=0)
    mods = _modulation(cc, w_mod, b_mod).reshape(depth, MOD_ROWS, 1, 3 * d)
    lat_row = lambda b: b
    ctx_row = lambda b: bsz

    rows = s // GRID_W
    cos_h, sin_h = _rope_tables(rows, HEAD_DIM)
    cos_b, sin_b = _rope_tables(rows, B_QK_DIM)
    one = jnp.ones((tc, LANES), F32)
    zero = jnp.zeros((tc, LANES), F32)
    tab_lat = {"h": (cos_h, sin_h), "b": (cos_b, sin_b)}
    tab_ctx = {"h": (one, zero), "b": (one, zero)}

    tq2, tq1 = min(N_DENSE // 2, s), min(N_DENSE, s)
    for l in range(depth):
        last = l == depth - 1
        wa, wb, wc, wd, uq, ukv, gq_pad, gkv = _layer_weights(
            w_in[l], c_w_uq[l], c_w_ukv[l], c_q_norm_g[l], c_kv_norm_g[l])
        dq_g, dk_g = d_q_norm_g[l][None, :], d_k_norm_g[l][None, :]
        wo = w_out[l].astype(BF16).reshape(4, GROUP_WIDTH, d)
        lam_init = 0.8 - 0.6 * math.exp(-0.3 * l)
        diff = (b_lambda[l], b_subln_g[l][None, :], lam_init)
        sink2 = a_sink[l].reshape(2, 2)

        def project(stream, row_fn, tabs, tm):
            norm = (norm_g[l][None, :], mods, l, row_fn)
            pa = _project(_proj_a_kernel, "proj_a", stream, norm, [wa], tabs["h"], tm, 4, 128, 2, 128)
            pb = _project(_proj_b_kernel, "proj_b", stream, norm, [wb], tabs["b"], tm, 8, 128, 4, 128)
            pc = _project(_proj_c_kernel, "proj_c", stream, norm, [wc, uq, ukv, gq_pad, gkv], tabs["b"],
                          tm, 4, 256, 4, 256)
            pd = _project(_proj_d_kernel, "proj_d", stream, norm, [wd, dq_g, dk_g], tabs["h"],
                          min(tm, TM_QKNORM), 4, 128, 2, 128)
            return pa, pb, pc, pd

        def sink_rows(tq):
            return jnp.repeat(sink2, tq, axis=1).reshape(2, 1, 2 * tq)

        (qa, ka, vta, gza), (qb, kb, vtb, gzb), (qc, kc, vtc, gzc), (qd, kd, vtd, gzd) = project(
            x, lat_row, tab_lat, TM_LATENT)
        (qa_c, ka_c, vta_c, gza_c), (qb_c, kb_c, vtb_c, gzb_c), (qc_c, kc_c, vtc_c, gzc_c), \
            (qd_c, kd_c, vtd_c, gzd_c) = project(ctx, ctx_row, tab_ctx, tc)

        ua = _band_attention(qa, ka_c, vta_c, ka, vta, gza, sink_rows(WINDOW))
        ub = _dense_attention("attn_b", qb, kb_c, vtb_c, kb, vtb, gzb, 2, tq2, diff=diff)
        uc = _dense_attention("attn_c", qc, kc_c, vtc_c, kc, vtc, gzc, 1, tq1)
        ud = _dense_attention("attn_d", qd, kd_c, vtd_c, kd, vtd, gzd, 2, tq2)
        if not last:
            ua_c = _dense_attention("attn_a_ctx", qa_c, ka_c, vta_c, None, None, gza_c, 2, tc,
                                    sink=sink_rows(tc))
            ub_c = _dense_attention("attn_b_ctx", qb_c, kb_c, vtb_c, None, None, gzb_c, 2, tc, diff=diff)
            uc_c = _dense_attention("attn_c_ctx", qc_c, kc_c, vtc_c, None, None, gzc_c, 1, tc)
            ud_c = _dense_attention("attn_d_ctx", qd_c, kd_c, vtd_c, None, None, gzd_c, 2, tc)
            ctx = _out_project((ua_c, ub_c, uc_c, ud_c), wo, ctx, mods, l, ctx_row, tc)
        x = _out_project((ua, ub, uc, ud), wo, x, mods, l, lat_row, TM_OUT,
                         final_g=final_norm_g[None, :] if last else None)
    return x
```

```python
import functools
import math

import jax
import jax.numpy as jnp
from jax import lax
from jax.experimental import pallas as pl
from jax.experimental.pallas import tpu as pltpu

F32 = jnp.float32
BF16 = jnp.bfloat16

D_MODEL = 2048
HEAD_DIM = 128
GRID_W = 64
GROUP_WIDTH = 512
WINDOW = 128
ROPE_THETA = 10000.0
EPS = 1e-6
NEG_INF = -1e30
C_Q_LORA = 448
C_ROPE = 64
C_QK_DIM = 192
B_QK_DIM = 64
LOG2E = math.log2(math.e)
LANES = 128
ONES_ROWS = 16
DV_AUG = HEAD_DIM + ONES_ROWS
MOD_ROWS = 16
VMEM_LIMIT = 56 * 1024 * 1024

TM_LATENT = 1024
TM_QKNORM = 1024
TM_OUT = 1024
KV_CHUNK = 512
N_DENSE = 4096
TQ_BAND = 1024
BAND_AHEAD = 3
MOD_TN = 512
COL_TILE = 256


def _dotf(a, b):
    return jnp.dot(a, b, preferred_element_type=F32)


def _dot_nt(a, b):
    return lax.dot_general(a, b, (((1,), (1,)), ((), ())), preferred_element_type=F32)


def _silu(z):
    return z / (1.0 + jnp.exp(-z))


def _rms(y, inv_n):
    return y * lax.rsqrt(jnp.sum(y * y, axis=-1, keepdims=True) * inv_n + EPS)


def _rope(x, cos, sin_signed, shift):
    lane = lax.broadcasted_iota(jnp.int32, x.shape, 1)
    even = (lane & shift) == 0
    x_up = pltpu.roll(x, LANES - shift, 1)
    x_dn = pltpu.roll(x, shift, 1)
    return x * cos + jnp.where(even, x_up, x_dn) * sin_signed


def _params(n_axes):
    return pltpu.CompilerParams(dimension_semantics=("parallel",) * n_axes,
                                vmem_limit_bytes=VMEM_LIMIT)


def _mod_kernel(c_ref, w_ref, b_ref, o_ref):
    a = _silu(c_ref[...]).astype(BF16)
    o_ref[0] = _dotf(a, w_ref[0].astype(BF16)) + b_ref[0]


def _modulation(cc, w_mod, b_mod):
    depth, d, n = w_mod.shape
    return pl.pallas_call(
        _mod_kernel,
        grid=(depth, n // MOD_TN),
        in_specs=[pl.BlockSpec((MOD_ROWS, d), lambda l, j: (0, 0)),
                  pl.BlockSpec((1, d, MOD_TN), lambda l, j: (l, 0, j)),
                  pl.BlockSpec((1, 1, MOD_TN), lambda l, j: (l, 0, j))],
        out_specs=pl.BlockSpec((1, MOD_ROWS, MOD_TN), lambda l, j: (l, 0, j)),
        out_shape=jax.ShapeDtypeStruct((depth, MOD_ROWS, n), F32),
        compiler_params=_params(2),
        name="modulation",
    )(cc, w_mod, b_mod.reshape(depth, 1, n))


def _prenorm(x_ref, g_ref, shift_ref, scale_ref):
    y = _rms(x_ref[0], 1.0 / D_MODEL) * g_ref[...]
    return (y * (1.0 + scale_ref[0, 0]) + shift_ref[0, 0]).astype(BF16)


def _mod_spec(layer, row_fn, which):
    return pl.BlockSpec((1, 1, 1, D_MODEL), lambda b, i: (layer, row_fn(b), 0, which))


def _store_vt(vt_ref, head, y):
    chunk = vt_ref.shape[-1]
    for j in range(vt_ref.shape[2]):
        vt_ref[0, head, j, 0:HEAD_DIM, :] = y[j * chunk:(j + 1) * chunk, :].T.astype(BF16)
        vt_ref[0, head, j, HEAD_DIM:DV_AUG, :] = jnp.ones((ONES_ROWS, chunk), BF16)


def _proj_a_body(h, w_ref, cos, sin, q_ref, k_ref, vt_ref, gz_ref):
    scale = LOG2E / math.sqrt(HEAD_DIM)
    yq = _dotf(h, w_ref[:, 0:512])
    for hh in range(4):
        q_ref[0, hh] = (_rope(yq[:, hh * 128:(hh + 1) * 128], cos, sin, 32) * scale).astype(BF16)
    ykv = _dotf(h, w_ref[:, 512:1024])
    for hh in range(2):
        k_ref[0, hh] = _rope(ykv[:, hh * 128:(hh + 1) * 128], cos, sin, 32).astype(BF16)
        _store_vt(vt_ref, hh, ykv[:, 256 + hh * 128:256 + (hh + 1) * 128])
    gz_ref[0] = _silu(_dotf(h, w_ref[:, 1024:1536])).astype(BF16)


def _proj_b_kernel(x_ref, g_ref, shift_ref, scale_ref, w_ref, cos_ref, sin_ref,
                   q_ref, k_ref, vt_ref, gz_ref):
    h = _prenorm(x_ref, g_ref, shift_ref, scale_ref)
    cos, sin = cos_ref[...], sin_ref[...]
    scale = LOG2E / math.sqrt(B_QK_DIM)
    yq = _dotf(h, w_ref[:, 0:512])
    lane = lax.broadcasted_iota(jnp.int32, (h.shape[0], LANES), 1)
    first = lane < B_QK_DIM
    for hh in range(4):
        y = _rope(yq[:, hh * 128:(hh + 1) * 128], cos, sin, 16) * scale
        q_ref[0, 2 * hh] = jnp.where(first, y, 0.0).astype(BF16)
        q_ref[0, 2 * hh + 1] = jnp.where(first, 0.0, y).astype(BF16)
    yk = _dotf(h, w_ref[:, 512:1024])
    for hh in range(4):
        k_ref[0, hh] = _rope(yk[:, hh * 128:(hh + 1) * 128], cos, sin, 16).astype(BF16)
    yv = _dotf(h, w_ref[:, 1024:1536])
    for hh in range(4):
        _store_vt(vt_ref, hh, yv[:, hh * 128:(hh + 1) * 128])
    gz_ref[0] = _silu(_dotf(h, w_ref[:, 1536:2048])).astype(BF16)


def _proj_c_kernel(x_ref, g_ref, shift_ref, scale_ref, w_ref, wuq_ref, wukv_ref, gq_ref, gkv_ref,
                   cos_ref, sin_ref, q_ref, k_ref, vt_ref, gz_ref):
    h = _prenorm(x_ref, g_ref, shift_ref, scale_ref)
    cos, sin = cos_ref[...], sin_ref[...]
    scale = LOG2E / math.sqrt(C_QK_DIM)
    ycq = _dotf(h, w_ref[:, 0:512])
    yckv = _dotf(h, w_ref[:, 512:768])
    gz_ref[0] = _silu(_dotf(h, w_ref[:, 768:1280])).astype(BF16)
    cqn = (_rms(ycq, 1.0 / C_Q_LORA) * gq_ref[...]).astype(BF16)
    yq = _dotf(cqn, wuq_ref[...])
    ckvn = (_rms(yckv[:, 0:128], 1.0 / 128) * gkv_ref[...]).astype(BF16)
    ykv = _dotf(ckvn, wukv_ref[...])
    kr_even = _rope(yckv[:, 128:256], cos, sin, 16)
    kr_odd = pltpu.roll(kr_even, C_ROPE, 1)
    qr = [_rope(yq[:, 512 + p * 128:512 + (p + 1) * 128], cos, sin, 16) * scale for p in range(2)]
    for hh in range(4):
        q_ref[0, hh, :, 0:128] = (yq[:, hh * 128:(hh + 1) * 128] * scale).astype(BF16)
        q_ref[0, hh, :, 128:256] = qr[hh // 2].astype(BF16)
        k_ref[0, hh, :, 0:128] = ykv[:, hh * 256:hh * 256 + 128].astype(BF16)
        k_ref[0, hh, :, 128:256] = (kr_even if hh % 2 == 0 else kr_odd).astype(BF16)
        _store_vt(vt_ref, hh, ykv[:, hh * 256 + 128:(hh + 1) * 256])


def _proj_ad_kernel(x_ref, g_ref, shift_ref, scale_ref, wa_ref, wd_ref, gq_ref, gk_ref, cos_ref, sin_ref,
                    qa_ref, ka_ref, vta_ref, gza_ref, qd_ref, kd_ref, vtd_ref, gzd_ref):
    h = _prenorm(x_ref, g_ref, shift_ref, scale_ref)
    cos, sin = cos_ref[...], sin_ref[...]
    scale = LOG2E / math.sqrt(HEAD_DIM)
    yq = _dotf(h, wd_ref[:, 0:512])
    for hh in range(4):
        yn = _rms(yq[:, hh * 128:(hh + 1) * 128], 1.0 / HEAD_DIM) * gq_ref[...]
        qd_ref[0, hh] = (_rope(yn, cos, sin, 32) * scale).astype(BF16)
    yk = _dotf(h, wd_ref[:, 512:768])
    for hh in range(2):
        yn = _rms(yk[:, hh * 128:(hh + 1) * 128], 1.0 / HEAD_DIM) * gk_ref[...]
        kd_ref[0, hh] = _rope(yn, cos, sin, 32).astype(BF16)
    gzd_ref[0] = _silu(_dotf(h, wd_ref[:, 1024:1536])).astype(BF16)
    _proj_a_body(h, wa_ref, cos, sin, qa_ref, ka_ref, vta_ref, gza_ref)
    yv = _dotf(h, wd_ref[:, 768:1024])
    for hh in range(2):
        _store_vt(vtd_ref, hh, yv[:, hh * 128:(hh + 1) * 128])


def _full_spec(a):
    nd = a.ndim
    return pl.BlockSpec(a.shape, lambda b, i: (0,) * nd, pipeline_mode=pl.Buffered(1))


def _project(kern, name, x, norm, weights, tables, tm, mixers):
    bsz, t, d = x.shape
    nt = t // tm
    chunk = min(tm, KV_CHUNK)
    sub = tm // chunk
    gain, mods, layer, row_fn = norm
    tab_specs = [pl.BlockSpec((tm, LANES), (lambda b, i: (i, 0)) if tab.shape[0] == t
                              else (lambda b, i: (0, 0))) for tab in tables]
    out_specs, out_shape = [], []
    for n_q, dq, n_kv, dk in mixers:
        out_specs += [pl.BlockSpec((1, n_q, tm, dq), lambda b, i: (b, 0, i, 0)),
                      pl.BlockSpec((1, n_kv, tm, dk), lambda b, i: (b, 0, i, 0)),
                      pl.BlockSpec((1, n_kv, sub, DV_AUG, chunk), lambda b, i: (b, 0, i, 0, 0)),
                      pl.BlockSpec((1, tm, GROUP_WIDTH), lambda b, i: (b, i, 0))]
        out_shape += [jax.ShapeDtypeStruct((bsz, n_q, t, dq), BF16),
                      jax.ShapeDtypeStruct((bsz, n_kv, t, dk), BF16),
                      jax.ShapeDtypeStruct((bsz, n_kv, nt * sub, DV_AUG, chunk), BF16),
                      jax.ShapeDtypeStruct((bsz, t, GROUP_WIDTH), BF16)]
    outs = pl.pallas_call(
        kern,
        grid=(bsz, nt),
        in_specs=[pl.BlockSpec((1, tm, d), lambda b, i: (b, i, 0)),
                  pl.BlockSpec((1, d), lambda b, i: (0, 0)),
                  _mod_spec(layer, row_fn, 0), _mod_spec(layer, row_fn, 1)]
        + [_full_spec(w) for w in weights] + tab_specs,
        out_specs=out_specs,
        out_shape=out_shape,
        compiler_params=_params(2),
        name=name,
    )(x, gain, mods, mods, *weights, *tables)
    return [tuple(outs[4 * j:4 * j + 4]) for j in range(len(mixers))]


def _finish_plain(acc, l, gate_ref, out_ref, n_sets, tq):
    o_t = acc * (1.0 / l)
    for g in range(n_sets):
        o = o_t[:, g * tq:(g + 1) * tq].astype(BF16).T
        out_ref[0, :, g * 128:(g + 1) * 128] = o * gate_ref[0, :, g * 128:(g + 1) * 128]


def _finish_diff(acc, l, gate_ref, out_ref, lam_ref, subg_ref, lam_init, tq):
    o_t = acc * (1.0 / l)
    lp = lam_ref[...]
    s1 = jnp.sum(lp[0:1] * lp[1:2], axis=1, keepdims=True)
    s2 = jnp.sum(lp[2:3] * lp[3:4], axis=1, keepdims=True)
    lam = jnp.exp(s1) - jnp.exp(s2) + lam_init
    d = o_t[:, 0:tq] - lam * o_t[:, tq:2 * tq]
    dn = d * lax.rsqrt(jnp.mean(d * d, axis=0, keepdims=True) + EPS)
    o = dn.T * subg_ref[...] * (1.0 - lam_init)
    out_ref[0] = (o * gate_ref[0].astype(F32)).astype(BF16)


def _dense_attn_kernel(*refs, n_sets, tq, n_chunks, has_sink, diff, lam_init):
    refs = list(refs)
    q_ref, kc_ref, vtc_ref = refs[:3]
    pos = 3
    if n_chunks:
        kl_ref, vtl_ref = refs[pos:pos + 2]
        pos += 2
    gate_ref = refs[pos]
    pos += 1
    if has_sink:
        sink_ref = refs[pos]
        pos += 1
    if diff:
        lam_ref, subg_ref = refs[pos:pos + 2]
        pos += 2
    out_ref, acc_ref = refs[pos:pos + 2]
    s_refs = refs[pos + 2:pos + 4]

    n = n_sets * tq
    n_tiles = n // COL_TILE
    per_set = tq // COL_TILE
    tc = kc_ref.shape[2]
    acc_ref[0:HEAD_DIM, :] = jnp.zeros((HEAD_DIM, n), F32)
    if has_sink:
        m0 = sink_ref[0] * LOG2E
        acc_ref[HEAD_DIM:DV_AUG, :] = jnp.ones((ONES_ROWS, n), F32)
    else:
        m0 = jnp.full((1, n), NEG_INF, F32)
        acc_ref[HEAD_DIM:DV_AUG, :] = jnp.zeros((ONES_ROWS, n), F32)

    def cols(c):
        return slice(c * COL_TILE, (c + 1) * COL_TILE)

    def scores(k, c, s_ref, rows):
        qt = q_ref[0, c // per_set, (c % per_set) * COL_TILE:(c % per_set + 1) * COL_TILE, :]
        s = _dot_nt(k, qt)
        s_ref[0:rows, cols(c)] = s
        return jnp.max(s, axis=0, keepdims=True)

    def softmax_pv(s_ref, rows, cmax, vt, m, c):
        m_new = jnp.maximum(m, cmax)
        alpha = jnp.exp2(m - m_new)
        p = jnp.exp2(s_ref[0:rows, cols(c)] - m_new)
        pv = _dotf(vt, p.astype(BF16))
        acc_ref[:, cols(c)] = alpha * acc_ref[:, cols(c)] + pv
        return m_new

    def stage(k_next, s_next, s_cur, rows_cur, cmax_cur, vt_cur, m):
        ms, cm = [], []
        for c in range(n_tiles):
            if k_next is not None:
                cm.append(scores(k_next, c, s_next, KV_CHUNK))
            if s_cur is not None:
                ms.append(softmax_pv(s_cur, rows_cur, cmax_cur[:, cols(c)], vt_cur, m[:, cols(c)], c))
        cat = lambda xs: jnp.concatenate(xs, axis=1) if xs else None
        return cat(cm), cat(ms)

    s_a, s_b = s_refs
    k_ctx = kc_ref[0, 0]
    cmax = jnp.concatenate([scores(k_ctx, c, s_a, tc) for c in range(n_tiles)], axis=1)
    if not n_chunks:
        _, m = stage(None, None, s_a, tc, cmax, vtc_ref[0, 0, 0], m0)
    else:
        def k_chunk(j):
            start = pl.multiple_of(j * KV_CHUNK, KV_CHUNK)
            return kl_ref[0, 0, pl.ds(start, KV_CHUNK), :]

        cmax, m = stage(k_chunk(0), s_b, s_a, tc, cmax, vtc_ref[0, 0, 0], m0)

        def body(i, carry):
            cmax, m = carry
            j = 2 * i
            cmax, m = stage(k_chunk(j + 1), s_a, s_b, KV_CHUNK, cmax, vtl_ref[0, 0, j], m)
            return stage(k_chunk(j + 2), s_b, s_a, KV_CHUNK, cmax, vtl_ref[0, 0, j + 1], m)

        cmax, m = lax.fori_loop(0, n_chunks // 2 - 1, body, (cmax, m))
        j = n_chunks - 2
        cmax, m = stage(k_chunk(j + 1), s_a, s_b, KV_CHUNK, cmax, vtl_ref[0, 0, j], m)
        _, m = stage(None, None, s_a, KV_CHUNK, cmax, vtl_ref[0, 0, j + 1], m)

    acc = acc_ref[0:HEAD_DIM, :]
    l = acc_ref[HEAD_DIM:HEAD_DIM + 1, :]
    if diff:
        _finish_diff(acc, l, gate_ref, out_ref, lam_ref, subg_ref, lam_init, tq)
    else:
        _finish_plain(acc, l, gate_ref, out_ref, n_sets, tq)


def _dense_attention(name, q, kc, vtc, kl, vtl, gate, n_sets, tq, sink=None, diff=None):
    bsz, _, t, dk = q.shape
    hkv, tc = kc.shape[1], kc.shape[2]
    dv = vtc.shape[3]
    n_chunks = 0 if kl is None else kl.shape[2] // KV_CHUNK
    assert n_chunks % 2 == 0 and tq % COL_TILE == 0
    out_w = LANES if diff else n_sets * LANES
    n = n_sets * tq
    args = [q, kc, vtc]
    specs = [pl.BlockSpec((1, n_sets, tq, dk), lambda b, h, i: (b, h, i, 0)),
             pl.BlockSpec((1, 1, tc, dk), lambda b, h, i: (b, h, 0, 0)),
             pl.BlockSpec((1, 1, 1, dv, tc), lambda b, h, i: (b, h, 0, 0, 0))]
    if n_chunks:
        args += [kl, vtl]
        specs += [pl.BlockSpec((1, 1, kl.shape[2], dk), lambda b, h, i: (b, h, 0, 0)),
                  pl.BlockSpec((1, 1, n_chunks, dv, KV_CHUNK), lambda b, h, i: (b, h, 0, 0, 0))]
    args.append(gate)
    specs.append(pl.BlockSpec((1, tq, out_w), lambda b, h, i: (b, i, h)))
    if sink is not None:
        args.append(sink)
        specs.append(pl.BlockSpec((1, 1, n), lambda b, h, i: (h, 0, 0)))
    lam_init = 0.0
    if diff is not None:
        lam_par, subg, lam_init = diff
        args += [lam_par, subg]
        specs += [pl.BlockSpec(lam_par.shape, lambda b, h, i: (0, 0)),
                  pl.BlockSpec(subg.shape, lambda b, h, i: (0, 0))]
    kern = functools.partial(_dense_attn_kernel, n_sets=n_sets, tq=tq, n_chunks=n_chunks,
                             has_sink=sink is not None, diff=diff is not None, lam_init=lam_init)
    return pl.pallas_call(
        kern,
        grid=(bsz, hkv, t // tq),
        in_specs=specs,
        out_specs=pl.BlockSpec((1, tq, out_w), lambda b, h, i: (b, i, h)),
        out_shape=jax.ShapeDtypeStruct((bsz, t, GROUP_WIDTH), BF16),
        scratch_shapes=[pltpu.VMEM((dv, n), F32)] + [pltpu.VMEM((max(KV_CHUNK, tc), n), F32)] * 2,
        compiler_params=_params(3),
        name=name,
    )(*args)


def _band_attn_kernel(q_ref, kc_ref, vtc_ref, kl_ref, vtl_ref, gate_ref, sink_ref, out_ref, *bufs,
                      blocks, n_chunks):
    w = WINDOW
    per_chunk = KV_CHUNK // w
    tc = kc_ref.shape[2]
    i = pl.program_id(2)
    first_step = i == 0
    last_step = i == pl.num_programs(2) - 1
    r1 = lax.broadcasted_iota(jnp.int32, (w, 1), 0)
    qc = lax.broadcasted_iota(jnp.int32, (1, 2 * w), 1) & (w - 1)
    prev_ok = r1 >= qc
    next_ok = r1 <= qc
    sink = sink_ref[0] * LOG2E
    k_ctx = kc_ref[0, 0]

    def pieces(t):
        chunk = i * (blocks // per_chunk) + t // per_chunk
        tt = t % per_chunk
        if tt < per_chunk - 1:
            pair = (chunk * KV_CHUNK + tt * w, chunk, tt * w, [(w, next_ok)], None)
            if tt > 0:
                single = (chunk * KV_CHUNK + (tt - 1) * w, chunk, (tt - 1) * w, [(0, prev_ok)], None)
            else:
                prev_chunk = jnp.maximum(chunk - 1, 0)
                bias = jnp.where(first_step, NEG_INF, 0.0) if t == 0 else None
                single = (prev_chunk * KV_CHUNK + (per_chunk - 1) * w, prev_chunk, (per_chunk - 1) * w,
                          [(0, prev_ok)], bias)
        else:
            pair = (chunk * KV_CHUNK + (tt - 1) * w, chunk, (tt - 1) * w, [(0, prev_ok)], None)
            next_chunk = jnp.minimum(chunk + 1, n_chunks - 1)
            bias = jnp.where(last_step, NEG_INF, 0.0) if t == blocks - 1 else None
            single = (next_chunk * KV_CHUNK, next_chunk, 0, [(0, next_ok)], bias)
        return pair, single

    def q_block(t):
        rows = slice(t * w, (t + 1) * w)
        return jnp.concatenate([q_ref[0, 0, rows, :], q_ref[0, 1, rows, :]], axis=0)

    def scores(t, s_ref):
        qt = q_block(t)
        s_ctx = _dot_nt(k_ctx, qt)
        s_ref[0:tc, :] = s_ctx
        cmax = jnp.max(s_ctx, axis=0, keepdims=True)
        row0 = tc
        for (start, _, _, masked, bias), rows in zip(pieces(t), (2 * w, w)):
            k = kl_ref[0, 0, pl.ds(pl.multiple_of(start, w), rows), :]
            s = _dot_nt(k, qt)
            if bias is not None:
                s = s + bias
            parts = []
            for r0 in range(0, rows, w):
                part = s[r0:r0 + w]
                for off, mask in masked:
                    if off == r0:
                        part = jnp.where(mask, part, NEG_INF)
                parts.append(part)
                cmax = jnp.maximum(cmax, jnp.max(part, axis=0, keepdims=True))
            for j, part in enumerate(parts):
                s_ref[row0 + j * w:row0 + (j + 1) * w, :] = part
            row0 += rows
        return cmax

    def softmax_pv(t, s_ref, cmax):
        m = jnp.maximum(sink, cmax)
        acc = _dotf(vtc_ref[0, 0, 0], jnp.exp2(s_ref[0:tc, :] - m).astype(BF16))
        row0 = tc
        for (_, chunk, lane0, _, _), rows in zip(pieces(t), (2 * w, w)):
            vt = vtl_ref[0, 0, chunk, :, lane0:lane0 + rows]
            acc = acc + _dotf(vt, jnp.exp2(s_ref[row0:row0 + rows, :] - m).astype(BF16))
            row0 += rows
        l = acc[HEAD_DIM:HEAD_DIM + 1, :] + jnp.exp2(sink - m)
        o_t = acc[0:HEAD_DIM, :] * (1.0 / l)
        rows = slice(t * w, (t + 1) * w)
        for g in range(2):
            gcols = slice(g * LANES, (g + 1) * LANES)
            o = o_t[:, g * w:(g + 1) * w].T
            out_ref[0, rows, gcols] = (o * gate_ref[0, rows, gcols].astype(F32)).astype(BF16)

    n_buf = len(bufs)
    ahead = n_buf - 1
    cmaxes = [scores(t, bufs[t % n_buf]) for t in range(min(ahead, blocks))]
    for t in range(blocks):
        if t + ahead < blocks:
            cmaxes.append(scores(t + ahead, bufs[(t + ahead) % n_buf]))
        softmax_pv(t, bufs[t % n_buf], cmaxes[t])


def _band_attention(q, kc, vtc, kl, vtl, gate, sink):
    bsz, _, s, dk = q.shape
    tq = min(TQ_BAND, s)
    tc = kc.shape[2]
    n_chunks = s // KV_CHUNK
    assert tq % KV_CHUNK == 0 and dk == HEAD_DIM
    return pl.pallas_call(
        functools.partial(_band_attn_kernel, blocks=tq // WINDOW, n_chunks=n_chunks),
        grid=(bsz, 2, s // tq),
        in_specs=[
            pl.BlockSpec((1, 2, tq, dk), lambda b, h, i: (b, h, i, 0)),
            pl.BlockSpec((1, 1, tc, dk), lambda b, h, i: (b, h, 0, 0)),
            pl.BlockSpec((1, 1, 1, DV_AUG, tc), lambda b, h, i: (b, h, 0, 0, 0)),
            pl.BlockSpec((1, 1, s, dk), lambda b, h, i: (b, h, 0, 0)),
            pl.BlockSpec((1, 1, n_chunks, DV_AUG, KV_CHUNK), lambda b, h, i: (b, h, 0, 0, 0)),
            pl.BlockSpec((1, tq, 2 * LANES), lambda b, h, i: (b, i, h)),
            pl.BlockSpec((1, 1, 2 * WINDOW), lambda b, h, i: (h, 0, 0)),
        ],
        out_specs=pl.BlockSpec((1, tq, 2 * LANES), lambda b, h, i: (b, i, h)),
        out_shape=jax.ShapeDtypeStruct((bsz, s, GROUP_WIDTH), BF16),
        scratch_shapes=[pltpu.VMEM((tc + 3 * WINDOW, 2 * WINDOW), F32)] * (BAND_AHEAD + 1),
        compiler_params=_params(3),
        name="attn_a_band",
    )(q, kc, vtc, kl, vtl, gate, sink)


def _out_kernel(*refs, final):
    ua, ub, uc, ud, w_ref, x_ref, gate_ref = refs[:7]
    out_ref = refs[-1]
    u = jnp.concatenate([ua[0], ub[0], uc[0], ud[0]], axis=1)
    acc = _dotf(u, w_ref[...].reshape(4 * GROUP_WIDTH, D_MODEL))
    xn = x_ref[0] + gate_ref[0, 0] * acc
    if final:
        xn = _rms(xn, 1.0 / D_MODEL) * refs[7][...]
    out_ref[0] = xn


def _out_project(us, w_out, x, mods, layer, row_fn, tm, final_g=None):
    bsz, t, d = x.shape
    u_spec = pl.BlockSpec((1, tm, GROUP_WIDTH), lambda b, i: (b, i, 0))
    specs = [u_spec] * 4 + [_full_spec(w_out), pl.BlockSpec((1, tm, d), lambda b, i: (b, i, 0)),
                            _mod_spec(layer, row_fn, 2)]
    args = list(us) + [w_out, x, mods]
    if final_g is not None:
        specs.append(pl.BlockSpec((1, d), lambda b, i: (0, 0)))
        args.append(final_g)
    return pl.pallas_call(
        functools.partial(_out_kernel, final=final_g is not None),
        grid=(bsz, t // tm),
        in_specs=specs,
        out_specs=pl.BlockSpec((1, tm, d), lambda b, i: (b, i, 0)),
        out_shape=jax.ShapeDtypeStruct((bsz, t, d), F32),
        compiler_params=_params(2),
        name="out_proj",
    )(*args)


def _rope_tables(rows, rot_dim):
    row = jnp.broadcast_to(jnp.arange(rows)[:, None], (rows, GRID_W)).reshape(-1).astype(F32)
    col = jnp.broadcast_to(jnp.arange(GRID_W)[None, :], (rows, GRID_W)).reshape(-1).astype(F32)
    axis_dim = rot_dim // 2
    inv_freq = ROPE_THETA ** (-jnp.arange(0, axis_dim, 2, dtype=F32) / axis_dim)
    ang_r = row[:, None] * inv_freq[None, :]
    ang_c = col[:, None] * inv_freq[None, :]
    ang = jnp.concatenate([ang_r, ang_r, ang_c, ang_c], axis=-1)
    quarter = rot_dim // 4
    sign = jnp.where((jnp.arange(rot_dim) // quarter) % 2 == 0, -1.0, 1.0).astype(F32)
    cos, sin = jnp.cos(ang), jnp.sin(ang) * sign
    reps = LANES // rot_dim
    return jnp.tile(cos, (1, reps)), jnp.tile(sin, (1, reps))


def _layer_weights(w_in, w_uq, w_ukv, gq, gkv):
    wa = w_in[:, 0:1536]
    wb = w_in[:, 1536:3584]
    zeros64 = jnp.zeros((w_in.shape[0], 64), w_in.dtype)
    wc = jnp.concatenate([w_in[:, 3584:4032], zeros64,
                          w_in[:, 4032:4160],
                          w_in[:, 4160:4224], zeros64,
                          w_in[:, 4224:4736]], axis=1)
    wd = w_in[:, 4736:6272]
    uq = w_uq.reshape(C_Q_LORA, 4, C_QK_DIM)
    uq = jnp.concatenate([uq[:, :, :128].reshape(C_Q_LORA, 512),
                          uq[:, :, 128:].reshape(C_Q_LORA, 256)], axis=1)
    uq = jnp.concatenate([uq, jnp.zeros((64, 768), uq.dtype)], axis=0)
    gq_pad = jnp.concatenate([gq, jnp.zeros((64,), gq.dtype)])[None, :]
    cast = lambda a: a.astype(BF16)
    return cast(wa), cast(wb), cast(wc), cast(wd), cast(uq), cast(w_ukv), gq_pad, gkv[None, :]


def kernel(x, c, ctx, c_ctx, w_mod, b_mod, norm_g, w_in, c_q_norm_g, c_kv_norm_g, c_w_uq, c_w_ukv,
           d_q_norm_g, d_k_norm_g, a_sink, b_lambda, b_subln_g, w_out, final_norm_g):
    bsz, s, d = x.shape
    tc = ctx.shape[1]
    depth = w_mod.shape[0]
    assert d == D_MODEL and s % TM_LATENT == 0 and s % GRID_W == 0 and bsz + 1 <= MOD_ROWS
    assert tc % LANES == 0 and TM_LATENT % KV_CHUNK == 0 and tc <= KV_CHUNK

    cc = jnp.concatenate([c, c_ctx[None, :], jnp.zeros((MOD_ROWS - bsz - 1, d), c.dtype)], axis=0)
    mods = _modulation(cc, w_mod, b_mod).reshape(depth, MOD_ROWS, 1, 3 * d)
    lat_row = lambda b: b
    ctx_row = lambda b: bsz

    rows = s // GRID_W
    cos_h, sin_h = _rope_tables(rows, HEAD_DIM)
    cos_b, sin_b = _rope_tables(rows, B_QK_DIM)
    one = jnp.ones((tc, LANES), F32)
    zero = jnp.zeros((tc, LANES), F32)
    tab_lat = {"h": (cos_h, sin_h), "b": (cos_b, sin_b)}
    tab_ctx = {"h": (one, zero), "b": (one, zero)}

    tq2, tq1 = min(N_DENSE // 2, s), min(N_DENSE, s)
    for l in range(depth):
        last = l == depth - 1
        wa, wb, wc, wd, uq, ukv, gq_pad, gkv = _layer_weights(
            w_in[l], c_w_uq[l], c_w_ukv[l], c_q_norm_g[l], c_kv_norm_g[l])
        dq_g, dk_g = d_q_norm_g[l][None, :], d_k_norm_g[l][None, :]
        wo = w_out[l].astype(BF16).reshape(4, GROUP_WIDTH, d)
        lam_init = 0.8 - 0.6 * math.exp(-0.3 * l)
        diff = (b_lambda[l], b_subln_g[l][None, :], lam_init)
        sink2 = a_sink[l].reshape(2, 2)

        def project(stream, row_fn, tabs, tm):
            norm = (norm_g[l][None, :], mods, l, row_fn)
            pa, pd = _project(_proj_ad_kernel, "proj_ad", stream, norm, [wa, wd, dq_g, dk_g], tabs["h"],
                              min(tm, TM_QKNORM), [(4, 128, 2, 128), (4, 128, 2, 128)])
            pb, = _project(_proj_b_kernel, "proj_b", stream, norm, [wb], tabs["b"], tm, [(8, 128, 4, 128)])
            pc, = _project(_proj_c_kernel, "proj_c", stream, norm, [wc, uq, ukv, gq_pad, gkv], tabs["b"],
                           tm, [(4, 256, 4, 256)])
            return pa, pb, pc, pd

        def sink_rows(tq):
            return jnp.repeat(sink2, tq, axis=1).reshape(2, 1, 2 * tq)

        (qa, ka, vta, gza), (qb, kb, vtb, gzb), (qc, kc, vtc, gzc), (qd, kd, vtd, gzd) = project(
            x, lat_row, tab_lat, TM_LATENT)
        (qa_c, ka_c, vta_c, gza_c), (qb_c, kb_c, vtb_c, gzb_c), (qc_c, kc_c, vtc_c, gzc_c), \
            (qd_c, kd_c, vtd_c, gzd_c) = project(ctx, ctx_row, tab_ctx, tc)

        ua = _band_attention(qa, ka_c, vta_c, ka, vta, gza, sink_rows(WINDOW))
        ub = _dense_attention("attn_b", qb, kb_c, vtb_c, kb, vtb, gzb, 2, tq2, diff=diff)
        uc = _dense_attention("attn_c", qc, kc_c, vtc_c, kc, vtc, gzc, 1, tq1)
        ud = _dense_attention("attn_d", qd, kd_c, vtd_c, kd, vtd, gzd, 2, tq2)
        if not last:
            ua_c = _dense_attention("attn_a_ctx", qa_c, ka_c, vta_c, None, None, gza_c, 2, tc,
                                    sink=sink_rows(tc))
            ub_c = _dense_attention("attn_b_ctx", qb_c, kb_c, vtb_c, None, None, gzb_c, 2, tc, diff=diff)
            uc_c = _dense_attention("attn_c_ctx", qc_c, kc_c, vtc_c, None, None, gzc_c, 1, tc)
            ud_c = _dense_attention("attn_d_ctx", qd_c, kd_c, vtd_c, None, None, gzd_c, 2, tc)
            ctx = _out_project((ua_c, ub_c, uc_c, ud_c), wo, ctx, mods, l, ctx_row, tc)
        x = _out_project((ua, ub, uc, ud), wo, x, mods, l, lat_row, TM_OUT,
                         final_g=final_norm_g[None, :] if last else None)
    return x
```

```python
import functools
import math

import jax
import jax.numpy as jnp
from jax import lax
from jax.experimental import pallas as pl
from jax.experimental.pallas import tpu as pltpu

F32 = jnp.float32
BF16 = jnp.bfloat16

D_MODEL = 2048
HEAD_DIM = 128
GRID_W = 64
GROUP_WIDTH = 512
WINDOW = 128
ROPE_THETA = 10000.0
EPS = 1e-6
NEG_INF = -1e30
C_Q_LORA = 448
C_ROPE = 64
C_QK_DIM = 192
B_QK_DIM = 64
LOG2E = math.log2(math.e)
LANES = 128
ONES_ROWS = 16
DV_AUG = HEAD_DIM + ONES_ROWS
MOD_ROWS = 16
VMEM_LIMIT = 56 * 1024 * 1024

TM_LATENT = 1024
TM_QKNORM = 1024
TM_OUT = 1024
KV_CHUNK = 512
N_DENSE = 4096
TQ_BAND = 1024
BAND_AHEAD = 3
MOD_TN = 512
COL_TILE = 256


def _dotf(a, b):
    return jnp.dot(a, b, preferred_element_type=F32)


def _dot_nt(a, b):
    return lax.dot_general(a, b, (((1,), (1,)), ((), ())), preferred_element_type=F32)


def _silu(z):
    return z / (1.0 + jnp.exp(-z))


def _rms(y, inv_n):
    return y * lax.rsqrt(jnp.sum(y * y, axis=-1, keepdims=True) * inv_n + EPS)


def _rope(x, cos, sin_signed, shift):
    lane = lax.broadcasted_iota(jnp.int32, x.shape, 1)
    even = (lane & shift) == 0
    x_up = pltpu.roll(x, LANES - shift, 1)
    x_dn = pltpu.roll(x, shift, 1)
    return x * cos + jnp.where(even, x_up, x_dn) * sin_signed


def _params(n_axes):
    return pltpu.CompilerParams(dimension_semantics=("parallel",) * n_axes,
                                vmem_limit_bytes=VMEM_LIMIT)


def _mod_kernel(c_ref, w_ref, b_ref, o_ref):
    a = _silu(c_ref[...]).astype(BF16)
    o_ref[0] = _dotf(a, w_ref[0].astype(BF16)) + b_ref[0]


def _modulation(cc, w_mod, b_mod):
    depth, d, n = w_mod.shape
    return pl.pallas_call(
        _mod_kernel,
        grid=(depth, n // MOD_TN),
        in_specs=[pl.BlockSpec((MOD_ROWS, d), lambda l, j: (0, 0)),
                  pl.BlockSpec((1, d, MOD_TN), lambda l, j: (l, 0, j)),
                  pl.BlockSpec((1, 1, MOD_TN), lambda l, j: (l, 0, j))],
        out_specs=pl.BlockSpec((1, MOD_ROWS, MOD_TN), lambda l, j: (l, 0, j)),
        out_shape=jax.ShapeDtypeStruct((depth, MOD_ROWS, n), F32),
        compiler_params=_params(2),
        name="modulation",
    )(cc, w_mod, b_mod.reshape(depth, 1, n))


def _prenorm(x_ref, g_ref, shift_ref, scale_ref):
    y = _rms(x_ref[0], 1.0 / D_MODEL) * g_ref[...]
    return (y * (1.0 + scale_ref[0, 0]) + shift_ref[0, 0]).astype(BF16)


def _mod_spec(layer, row_fn, which):
    return pl.BlockSpec((1, 1, 1, D_MODEL), lambda b, i: (layer, row_fn(b), 0, which))


def _store_vt(vt_ref, head, y):
    chunk = vt_ref.shape[-1]
    for j in range(vt_ref.shape[2]):
        vt_ref[0, head, j, 0:HEAD_DIM, :] = y[j * chunk:(j + 1) * chunk, :].T.astype(BF16)
        vt_ref[0, head, j, HEAD_DIM:DV_AUG, :] = jnp.ones((ONES_ROWS, chunk), BF16)


def _proj_a_body(h, w_ref, cos, sin, q_ref, k_ref, vt_ref, gz_ref):
    scale = LOG2E / math.sqrt(HEAD_DIM)
    yq = _dotf(h, w_ref[:, 0:512])
    for hh in range(4):
        q_ref[0, hh] = (_rope(yq[:, hh * 128:(hh + 1) * 128], cos, sin, 32) * scale).astype(BF16)
    ykv = _dotf(h, w_ref[:, 512:1024])
    for hh in range(2):
        k_ref[0, hh] = _rope(ykv[:, hh * 128:(hh + 1) * 128], cos, sin, 32).astype(BF16)
        _store_vt(vt_ref, hh, ykv[:, 256 + hh * 128:256 + (hh + 1) * 128])
    gz_ref[0] = _silu(_dotf(h, w_ref[:, 1024:1536])).astype(BF16)


def _proj_b_body(h, w_ref, cos, sin, q_ref, k_ref, vt_ref, gz_ref):
    scale = LOG2E / math.sqrt(B_QK_DIM)
    yq = _dotf(h, w_ref[:, 0:512])
    lane = lax.broadcasted_iota(jnp.int32, (h.shape[0], LANES), 1)
    first = lane < B_QK_DIM
    for hh in range(4):
        y = _rope(yq[:, hh * 128:(hh + 1) * 128], cos, sin, 16) * scale
        q_ref[0, 2 * hh] = jnp.where(first, y, 0.0).astype(BF16)
        q_ref[0, 2 * hh + 1] = jnp.where(first, 0.0, y).astype(BF16)
    yk = _dotf(h, w_ref[:, 512:1024])
    for hh in range(4):
        k_ref[0, hh] = _rope(yk[:, hh * 128:(hh + 1) * 128], cos, sin, 16).astype(BF16)
    yv = _dotf(h, w_ref[:, 1024:1536])
    for hh in range(4):
        _store_vt(vt_ref, hh, yv[:, hh * 128:(hh + 1) * 128])
    gz_ref[0] = _silu(_dotf(h, w_ref[:, 1536:2048])).astype(BF16)


def _proj_b_kernel(x_ref, g_ref, shift_ref, scale_ref, w_ref, cos_ref, sin_ref,
                   q_ref, k_ref, vt_ref, gz_ref):
    h = _prenorm(x_ref, g_ref, shift_ref, scale_ref)
    _proj_b_body(h, w_ref, cos_ref[...], sin_ref[...], q_ref, k_ref, vt_ref, gz_ref)


def _proj_c_kernel(x_ref, g_ref, shift_ref, scale_ref, w_ref, wuq_ref, wukv_ref, gq_ref, gkv_ref,
                   cos_ref, sin_ref, q_ref, k_ref, vt_ref, gz_ref):
    h = _prenorm(x_ref, g_ref, shift_ref, scale_ref)
    cos, sin = cos_ref[...], sin_ref[...]
    scale = LOG2E / math.sqrt(C_QK_DIM)
    ycq = _dotf(h, w_ref[:, 0:512])
    yckv = _dotf(h, w_ref[:, 512:768])
    gz_ref[0] = _silu(_dotf(h, w_ref[:, 768:1280])).astype(BF16)
    cqn = (_rms(ycq, 1.0 / C_Q_LORA) * gq_ref[...]).astype(BF16)
    yq = _dotf(cqn, wuq_ref[...])
    ckvn = (_rms(yckv[:, 0:128], 1.0 / 128) * gkv_ref[...]).astype(BF16)
    ykv = _dotf(ckvn, wukv_ref[...])
    kr_even = _rope(yckv[:, 128:256], cos, sin, 16)
    kr_odd = pltpu.roll(kr_even, C_ROPE, 1)
    qr = [_rope(yq[:, 512 + p * 128:512 + (p + 1) * 128], cos, sin, 16) * scale for p in range(2)]
    for hh in range(4):
        q_ref[0, hh, :, 0:128] = (yq[:, hh * 128:(hh + 1) * 128] * scale).astype(BF16)
        q_ref[0, hh, :, 128:256] = qr[hh // 2].astype(BF16)
        k_ref[0, hh, :, 0:128] = ykv[:, hh * 256:hh * 256 + 128].astype(BF16)
        k_ref[0, hh, :, 128:256] = (kr_even if hh % 2 == 0 else kr_odd).astype(BF16)
        _store_vt(vt_ref, hh, ykv[:, hh * 256 + 128:(hh + 1) * 256])


def _proj_ad_kernel(x_ref, g_ref, shift_ref, scale_ref, wa_ref, wd_ref, gq_ref, gk_ref, cos_ref, sin_ref,
                    qa_ref, ka_ref, vta_ref, gza_ref, qd_ref, kd_ref, vtd_ref, gzd_ref):
    h = _prenorm(x_ref, g_ref, shift_ref, scale_ref)
    cos, sin = cos_ref[...], sin_ref[...]
    scale = LOG2E / math.sqrt(HEAD_DIM)
    yq = _dotf(h, wd_ref[:, 0:512])
    for hh in range(4):
        yn = _rms(yq[:, hh * 128:(hh + 1) * 128], 1.0 / HEAD_DIM) * gq_ref[...]
        qd_ref[0, hh] = (_rope(yn, cos, sin, 32) * scale).astype(BF16)
    yk = _dotf(h, wd_ref[:, 512:768])
    for hh in range(2):
        yn = _rms(yk[:, hh * 128:(hh + 1) * 128], 1.0 / HEAD_DIM) * gk_ref[...]
        kd_ref[0, hh] = _rope(yn, cos, sin, 32).astype(BF16)
    gzd_ref[0] = _silu(_dotf(h, wd_ref[:, 1024:1536])).astype(BF16)
    _proj_a_body(h, wa_ref, cos, sin, qa_ref, ka_ref, vta_ref, gza_ref)
    yv = _dotf(h, wd_ref[:, 768:1024])
    for hh in range(2):
        _store_vt(vtd_ref, hh, yv[:, hh * 128:(hh + 1) * 128])


def _full_spec(a):
    nd = a.ndim
    return pl.BlockSpec(a.shape, lambda b, i: (0,) * nd, pipeline_mode=pl.Buffered(1))


def _project(kern, name, x, norm, weights, tables, tm, mixers):
    bsz, t, d = x.shape
    nt = t // tm
    chunk = min(tm, KV_CHUNK)
    sub = tm // chunk
    gain, mods, layer, row_fn = norm
    tab_specs = [pl.BlockSpec((tm, LANES), (lambda b, i: (i, 0)) if tab.shape[0] == t
                              else (lambda b, i: (0, 0))) for tab in tables]
    out_specs, out_shape = [], []
    for n_q, dq, n_kv, dk in mixers:
        out_specs += [pl.BlockSpec((1, n_q, tm, dq), lambda b, i: (b, 0, i, 0)),
                      pl.BlockSpec((1, n_kv, tm, dk), lambda b, i: (b, 0, i, 0)),
                      pl.BlockSpec((1, n_kv, sub, DV_AUG, chunk), lambda b, i: (b, 0, i, 0, 0)),
                      pl.BlockSpec((1, tm, GROUP_WIDTH), lambda b, i: (b, i, 0))]
        out_shape += [jax.ShapeDtypeStruct((bsz, n_q, t, dq), BF16),
                      jax.ShapeDtypeStruct((bsz, n_kv, t, dk), BF16),
                      jax.ShapeDtypeStruct((bsz, n_kv, nt * sub, DV_AUG, chunk), BF16),
                      jax.ShapeDtypeStruct((bsz, t, GROUP_WIDTH), BF16)]
    outs = pl.pallas_call(
        kern,
        grid=(bsz, nt),
        in_specs=[pl.BlockSpec((1, tm, d), lambda b, i: (b, i, 0)),
                  pl.BlockSpec((1, d), lambda b, i: (0, 0)),
                  _mod_spec(layer, row_fn, 0), _mod_spec(layer, row_fn, 1)]
        + [_full_spec(w) for w in weights] + tab_specs,
        out_specs=out_specs,
        out_shape=out_shape,
        compiler_params=_params(2),
        name=name,
    )(x, gain, mods, mods, *weights, *tables)
    return [tuple(outs[4 * j:4 * j + 4]) for j in range(len(mixers))]


def _finish_plain(acc, l, gate_ref, out_ref, n_sets, tq):
    o_t = acc * (1.0 / l)
    for g in range(n_sets):
        o = o_t[:, g * tq:(g + 1) * tq].astype(BF16).T
        out_ref[0, :, g * 128:(g + 1) * 128] = o * gate_ref[0, :, g * 128:(g + 1) * 128]


def _finish_diff(acc, l, gate_ref, out_ref, lam_ref, subg_ref, lam_init, tq):
    o_t = acc * (1.0 / l)
    lp = lam_ref[...]
    s1 = jnp.sum(lp[0:1] * lp[1:2], axis=1, keepdims=True)
    s2 = jnp.sum(lp[2:3] * lp[3:4], axis=1, keepdims=True)
    lam = jnp.exp(s1) - jnp.exp(s2) + lam_init
    d = o_t[:, 0:tq] - lam * o_t[:, tq:2 * tq]
    dn = d * lax.rsqrt(jnp.mean(d * d, axis=0, keepdims=True) + EPS)
    o = dn.T * subg_ref[...] * (1.0 - lam_init)
    out_ref[0] = (o * gate_ref[0].astype(F32)).astype(BF16)


def _dense_attn_kernel(*refs, n_sets, tq, n_chunks, has_sink, diff, lam_init):
    refs = list(refs)
    q_ref, kc_ref, vtc_ref = refs[:3]
    pos = 3
    if n_chunks:
        kl_ref, vtl_ref = refs[pos:pos + 2]
        pos += 2
    gate_ref = refs[pos]
    pos += 1
    if has_sink:
        sink_ref = refs[pos]
        pos += 1
    if diff:
        lam_ref, subg_ref = refs[pos:pos + 2]
        pos += 2
    out_ref, acc_ref = refs[pos:pos + 2]
    s_refs = refs[pos + 2:pos + 4]

    n = n_sets * tq
    n_tiles = n // COL_TILE
    per_set = tq // COL_TILE
    tc = kc_ref.shape[2]
    acc_ref[0:HEAD_DIM, :] = jnp.zeros((HEAD_DIM, n), F32)
    if has_sink:
        m0 = sink_ref[0] * LOG2E
        acc_ref[HEAD_DIM:DV_AUG, :] = jnp.ones((ONES_ROWS, n), F32)
    else:
        m0 = jnp.full((1, n), NEG_INF, F32)
        acc_ref[HEAD_DIM:DV_AUG, :] = jnp.zeros((ONES_ROWS, n), F32)

    def cols(c):
        return slice(c * COL_TILE, (c + 1) * COL_TILE)

    def scores(k, c, s_ref, rows):
        qt = q_ref[0, c // per_set, (c % per_set) * COL_TILE:(c % per_set + 1) * COL_TILE, :]
        s = _dot_nt(k, qt)
        s_ref[0:rows, cols(c)] = s
        return jnp.max(s, axis=0, keepdims=True)

    def softmax_pv(s_ref, rows, cmax, vt, m, c):
        m_new = jnp.maximum(m, cmax)
        alpha = jnp.exp2(m - m_new)
        p = jnp.exp2(s_ref[0:rows, cols(c)] - m_new)
        pv = _dotf(vt, p.astype(BF16))
        acc_ref[:, cols(c)] = alpha * acc_ref[:, cols(c)] + pv
        return m_new

    def stage(k_next, s_next, s_cur, rows_cur, cmax_cur, vt_cur, m):
        ms, cm = [], []
        for c in range(n_tiles):
            if k_next is not None:
                cm.append(scores(k_next, c, s_next, KV_CHUNK))
            if s_cur is not None:
                ms.append(softmax_pv(s_cur, rows_cur, cmax_cur[:, cols(c)], vt_cur, m[:, cols(c)], c))
        cat = lambda xs: jnp.concatenate(xs, axis=1) if xs else None
        return cat(cm), cat(ms)

    s_a, s_b = s_refs
    k_ctx = kc_ref[0, 0]
    cmax = jnp.concatenate([scores(k_ctx, c, s_a, tc) for c in range(n_tiles)], axis=1)
    if not n_chunks:
        _, m = stage(None, None, s_a, tc, cmax, vtc_ref[0, 0, 0], m0)
    else:
        def k_chunk(j):
            start = pl.multiple_of(j * KV_CHUNK, KV_CHUNK)
            return kl_ref[0, 0, pl.ds(start, KV_CHUNK), :]

        cmax, m = stage(k_chunk(0), s_b, s_a, tc, cmax, vtc_ref[0, 0, 0], m0)

        def body(i, carry):
            cmax, m = carry
            j = 2 * i
            cmax, m = stage(k_chunk(j + 1), s_a, s_b, KV_CHUNK, cmax, vtl_ref[0, 0, j], m)
            return stage(k_chunk(j + 2), s_b, s_a, KV_CHUNK, cmax, vtl_ref[0, 0, j + 1], m)

        cmax, m = lax.fori_loop(0, n_chunks // 2 - 1, body, (cmax, m))
        j = n_chunks - 2
        cmax, m = stage(k_chunk(j + 1), s_a, s_b, KV_CHUNK, cmax, vtl_ref[0, 0, j], m)
        _, m = stage(None, None, s_a, KV_CHUNK, cmax, vtl_ref[0, 0, j + 1], m)

    acc = acc_ref[0:HEAD_DIM, :]
    l = acc_ref[HEAD_DIM:HEAD_DIM + 1, :]
    if diff:
        _finish_diff(acc, l, gate_ref, out_ref, lam_ref, subg_ref, lam_init, tq)
    else:
        _finish_plain(acc, l, gate_ref, out_ref, n_sets, tq)


def _ctx_kv_specs(kc, vtc, tc):
    dk, dv, chunk = kc.shape[3], vtc.shape[3], vtc.shape[4]
    per = chunk // tc
    return [pl.BlockSpec((1, 1, tc, dk), lambda b, h, i: (0, h, b, 0)),
            pl.BlockSpec((1, 1, 1, dv, tc), lambda b, h, i: (0, h, b // per, 0, b % per))]


def _dense_attention(name, q, kc, vtc, kl, vtl, gate, n_sets, tq, bsz, tc, sink=None, diff=None):
    dk = q.shape[3]
    hkv, dv = kc.shape[1], vtc.shape[3]
    n_chunks = 0 if kl is None else kl.shape[2] // KV_CHUNK
    assert n_chunks % 2 == 0 and tq % COL_TILE == 0
    out_w = LANES if diff else n_sets * LANES
    n = n_sets * tq
    args = [q, kc, vtc]
    if n_chunks:
        t = q.shape[2]
        q_map, io_map = (lambda b, h, i: (b, h, i, 0)), (lambda b, h, i: (b, i, h))
    else:
        assert tq == tc
        t = tc
        q_map, io_map = (lambda b, h, i: (0, h, b, 0)), (lambda b, h, i: (0, b, h))
    specs = [pl.BlockSpec((1, n_sets, tq, dk), q_map)] + _ctx_kv_specs(kc, vtc, tc)
    if n_chunks:
        args += [kl, vtl]
        specs += [pl.BlockSpec((1, 1, kl.shape[2], dk), lambda b, h, i: (b, h, 0, 0)),
                  pl.BlockSpec((1, 1, n_chunks, dv, KV_CHUNK), lambda b, h, i: (b, h, 0, 0, 0))]
    args.append(gate)
    specs.append(pl.BlockSpec((1, tq, out_w), io_map))
    if sink is not None:
        args.append(sink)
        specs.append(pl.BlockSpec((1, 1, n), lambda b, h, i: (h, 0, 0)))
    lam_init = 0.0
    if diff is not None:
        lam_par, subg, lam_init = diff
        args += [lam_par, subg]
        specs += [pl.BlockSpec(lam_par.shape, lambda b, h, i: (0, 0)),
                  pl.BlockSpec(subg.shape, lambda b, h, i: (0, 0))]
    kern = functools.partial(_dense_attn_kernel, n_sets=n_sets, tq=tq, n_chunks=n_chunks,
                             has_sink=sink is not None, diff=diff is not None, lam_init=lam_init)
    return pl.pallas_call(
        kern,
        grid=(bsz, hkv, t // tq),
        in_specs=specs,
        out_specs=pl.BlockSpec((1, tq, out_w), io_map),
        out_shape=jax.ShapeDtypeStruct(gate.shape, BF16),
        scratch_shapes=[pltpu.VMEM((dv, n), F32)] + [pltpu.VMEM((max(KV_CHUNK, tc), n), F32)] * 2,
        compiler_params=_params(3),
        name=name,
    )(*args)


def _band_attn_kernel(q_ref, kc_ref, vtc_ref, kl_ref, vtl_ref, gate_ref, sink_ref, out_ref, *bufs,
                      blocks, n_chunks):
    w = WINDOW
    per_chunk = KV_CHUNK // w
    tc = kc_ref.shape[2]
    i = pl.program_id(2)
    first_step = i == 0
    last_step = i == pl.num_programs(2) - 1
    r1 = lax.broadcasted_iota(jnp.int32, (w, 1), 0)
    qc = lax.broadcasted_iota(jnp.int32, (1, 2 * w), 1) & (w - 1)
    prev_ok = r1 >= qc
    next_ok = r1 <= qc
    sink = sink_ref[0] * LOG2E
    k_ctx = kc_ref[0, 0]

    def pieces(t):
        chunk = i * (blocks // per_chunk) + t // per_chunk
        tt = t % per_chunk
        if tt < per_chunk - 1:
            pair = (chunk * KV_CHUNK + tt * w, chunk, tt * w, [(w, next_ok)], None)
            if tt > 0:
                single = (chunk * KV_CHUNK + (tt - 1) * w, chunk, (tt - 1) * w, [(0, prev_ok)], None)
            else:
                prev_chunk = jnp.maximum(chunk - 1, 0)
                bias = jnp.where(first_step, NEG_INF, 0.0) if t == 0 else None
                single = (prev_chunk * KV_CHUNK + (per_chunk - 1) * w, prev_chunk, (per_chunk - 1) * w,
                          [(0, prev_ok)], bias)
        else:
            pair = (chunk * KV_CHUNK + (tt - 1) * w, chunk, (tt - 1) * w, [(0, prev_ok)], None)
            next_chunk = jnp.minimum(chunk + 1, n_chunks - 1)
            bias = jnp.where(last_step, NEG_INF, 0.0) if t == blocks - 1 else None
            single = (next_chunk * KV_CHUNK, next_chunk, 0, [(0, next_ok)], bias)
        return pair, single

    def q_block(t):
        rows = slice(t * w, (t + 1) * w)
        return jnp.concatenate([q_ref[0, 0, rows, :], q_ref[0, 1, rows, :]], axis=0)

    def scores(t, s_ref):
        qt = q_block(t)
        s_ctx = _dot_nt(k_ctx, qt)
        s_ref[0:tc, :] = s_ctx
        cmax = jnp.max(s_ctx, axis=0, keepdims=True)
        row0 = tc
        for (start, _, _, masked, bias), rows in zip(pieces(t), (2 * w, w)):
            k = kl_ref[0, 0, pl.ds(pl.multiple_of(start, w), rows), :]
            s = _dot_nt(k, qt)
            if bias is not None:
                s = s + bias
            parts = []
            for r0 in range(0, rows, w):
                part = s[r0:r0 + w]
                for off, mask in masked:
                    if off == r0:
                        part = jnp.where(mask, part, NEG_INF)
                parts.append(part)
                cmax = jnp.maximum(cmax, jnp.max(part, axis=0, keepdims=True))
            for j, part in enumerate(parts):
                s_ref[row0 + j * w:row0 + (j + 1) * w, :] = part
            row0 += rows
        return cmax

    def softmax_pv(t, s_ref, cmax):
        m = jnp.maximum(sink, cmax)
        acc = _dotf(vtc_ref[0, 0, 0], jnp.exp2(s_ref[0:tc, :] - m).astype(BF16))
        row0 = tc
        for (_, chunk, lane0, _, _), rows in zip(pieces(t), (2 * w, w)):
            vt = vtl_ref[0, 0, chunk, :, lane0:lane0 + rows]
            acc = acc + _dotf(vt, jnp.exp2(s_ref[row0:row0 + rows, :] - m).astype(BF16))
            row0 += rows
        l = acc[HEAD_DIM:HEAD_DIM + 1, :] + jnp.exp2(sink - m)
        o_t = acc[0:HEAD_DIM, :] * (1.0 / l)
        rows = slice(t * w, (t + 1) * w)
        for g in range(2):
            gcols = slice(g * LANES, (g + 1) * LANES)
            o = o_t[:, g * w:(g + 1) * w].T
            out_ref[0, rows, gcols] = (o * gate_ref[0, rows, gcols].astype(F32)).astype(BF16)

    n_buf = len(bufs)
    ahead = n_buf - 1
    cmaxes = [scores(t, bufs[t % n_buf]) for t in range(min(ahead, blocks))]
    for t in range(blocks):
        if t + ahead < blocks:
            cmaxes.append(scores(t + ahead, bufs[(t + ahead) % n_buf]))
        softmax_pv(t, bufs[t % n_buf], cmaxes[t])


def _band_attention(q, kc, vtc, kl, vtl, gate, sink, tc):
    bsz, _, s, dk = q.shape
    tq = min(TQ_BAND, s)
    n_chunks = s // KV_CHUNK
    assert tq % KV_CHUNK == 0 and dk == HEAD_DIM
    return pl.pallas_call(
        functools.partial(_band_attn_kernel, blocks=tq // WINDOW, n_chunks=n_chunks),
        grid=(bsz, 2, s // tq),
        in_specs=[pl.BlockSpec((1, 2, tq, dk), lambda b, h, i: (b, h, i, 0))]
        + _ctx_kv_specs(kc, vtc, tc) + [
            pl.BlockSpec((1, 1, s, dk), lambda b, h, i: (b, h, 0, 0)),
            pl.BlockSpec((1, 1, n_chunks, DV_AUG, KV_CHUNK), lambda b, h, i: (b, h, 0, 0, 0)),
            pl.BlockSpec((1, tq, 2 * LANES), lambda b, h, i: (b, i, h)),
            pl.BlockSpec((1, 1, 2 * WINDOW), lambda b, h, i: (h, 0, 0)),
        ],
        out_specs=pl.BlockSpec((1, tq, 2 * LANES), lambda b, h, i: (b, i, h)),
        out_shape=jax.ShapeDtypeStruct((bsz, s, GROUP_WIDTH), BF16),
        scratch_shapes=[pltpu.VMEM((tc + 3 * WINDOW, 2 * WINDOW), F32)] * (BAND_AHEAD + 1),
        compiler_params=_params(3),
        name="attn_a_band",
    )(q, kc, vtc, kl, vtl, gate, sink)


def _out_kernel(*refs, final):
    ua, ub, uc, ud, w_ref, x_ref, gate_ref = refs[:7]
    out_ref = refs[-1]
    u = jnp.concatenate([ua[0], ub[0], uc[0], ud[0]], axis=1)
    acc = _dotf(u, w_ref[...].reshape(4 * GROUP_WIDTH, D_MODEL))
    xn = x_ref[0] + gate_ref[0, 0] * acc
    if final:
        xn = _rms(xn, 1.0 / D_MODEL) * refs[7][...]
    out_ref[0] = xn


def _out_project(us, w_out, x, mods, layer, row_fn, tm, final_g=None):
    bsz, t, d = x.shape
    u_spec = pl.BlockSpec((1, tm, GROUP_WIDTH), lambda b, i: (b, i, 0))
    specs = [u_spec] * 4 + [_full_spec(w_out), pl.BlockSpec((1, tm, d), lambda b, i: (b, i, 0)),
                            _mod_spec(layer, row_fn, 2)]
    args = list(us) + [w_out, x, mods]
    if final_g is not None:
        specs.append(pl.BlockSpec((1, d), lambda b, i: (0, 0)))
        args.append(final_g)
    return pl.pallas_call(
        functools.partial(_out_kernel, final=final_g is not None),
        grid=(bsz, t // tm),
        in_specs=specs,
        out_specs=pl.BlockSpec((1, tm, d), lambda b, i: (b, i, 0)),
        out_shape=jax.ShapeDtypeStruct((bsz, t, d), F32),
        compiler_params=_params(2),
        name="out_proj",
    )(*args)


def _rope_tables(rows, rot_dim):
    row = jnp.broadcast_to(jnp.arange(rows)[:, None], (rows, GRID_W)).reshape(-1).astype(F32)
    col = jnp.broadcast_to(jnp.arange(GRID_W)[None, :], (rows, GRID_W)).reshape(-1).astype(F32)
    axis_dim = rot_dim // 2
    inv_freq = ROPE_THETA ** (-jnp.arange(0, axis_dim, 2, dtype=F32) / axis_dim)
    ang_r = row[:, None] * inv_freq[None, :]
    ang_c = col[:, None] * inv_freq[None, :]
    ang = jnp.concatenate([ang_r, ang_r, ang_c, ang_c], axis=-1)
    quarter = rot_dim // 4
    sign = jnp.where((jnp.arange(rot_dim) // quarter) % 2 == 0, -1.0, 1.0).astype(F32)
    cos, sin = jnp.cos(ang), jnp.sin(ang) * sign
    reps = LANES // rot_dim
    return jnp.tile(cos, (1, reps)), jnp.tile(sin, (1, reps))


def _layer_weights(w_in, w_uq, w_ukv, gq, gkv):
    wa = w_in[:, 0:1536]
    wb = w_in[:, 1536:3584]
    zeros64 = jnp.zeros((w_in.shape[0], 64), w_in.dtype)
    wc = jnp.concatenate([w_in[:, 3584:4032], zeros64,
                          w_in[:, 4032:4160],
                          w_in[:, 4160:4224], zeros64,
                          w_in[:, 4224:4736]], axis=1)
    wd = w_in[:, 4736:6272]
    uq = w_uq.reshape(C_Q_LORA, 4, C_QK_DIM)
    uq = jnp.concatenate([uq[:, :, :128].reshape(C_Q_LORA, 512),
                          uq[:, :, 128:].reshape(C_Q_LORA, 256)], axis=1)
    uq = jnp.concatenate([uq, jnp.zeros((64, 768), uq.dtype)], axis=0)
    gq_pad = jnp.concatenate([gq, jnp.zeros((64,), gq.dtype)])[None, :]
    cast = lambda a: a.astype(BF16)
    return cast(wa), cast(wb), cast(wc), cast(wd), cast(uq), cast(w_ukv), gq_pad, gkv[None, :]


def kernel(x, c, ctx, c_ctx, w_mod, b_mod, norm_g, w_in, c_q_norm_g, c_kv_norm_g, c_w_uq, c_w_ukv,
           d_q_norm_g, d_k_norm_g, a_sink, b_lambda, b_subln_g, w_out, final_norm_g):
    bsz, s, d = x.shape
    tc = ctx.shape[1]
    depth = w_mod.shape[0]
    assert d == D_MODEL and s % TM_LATENT == 0 and s % GRID_W == 0 and bsz + 1 <= MOD_ROWS
    assert tc % LANES == 0 and TM_LATENT % KV_CHUNK == 0 and tc <= KV_CHUNK

    cc = jnp.concatenate([c, c_ctx[None, :], jnp.zeros((MOD_ROWS - bsz - 1, d), c.dtype)], axis=0)
    mods = _modulation(cc, w_mod, b_mod).reshape(depth, MOD_ROWS, 1, 3 * d)
    lat_row = lambda b: b
    ctx_row = lambda b: bsz

    rows = s // GRID_W
    cos_h, sin_h = _rope_tables(rows, HEAD_DIM)
    cos_b, sin_b = _rope_tables(rows, B_QK_DIM)
    ctx = ctx.reshape(1, bsz * tc, d)
    tm_ctx = max(t for t in (tc, 2 * tc, 4 * tc) if (bsz * tc) % t == 0 and t <= TM_LATENT)
    assert tm_ctx % min(tm_ctx, KV_CHUNK) == 0
    one = jnp.ones((tm_ctx, LANES), F32)
    zero = jnp.zeros((tm_ctx, LANES), F32)
    tab_lat = {"h": (cos_h, sin_h), "b": (cos_b, sin_b)}
    tab_ctx = {"h": (one, zero), "b": (one, zero)}

    tq2, tq1 = min(N_DENSE // 2, s), min(N_DENSE, s)
    for l in range(depth):
        last = l == depth - 1
        wa, wb, wc, wd, uq, ukv, gq_pad, gkv = _layer_weights(
            w_in[l], c_w_uq[l], c_w_ukv[l], c_q_norm_g[l], c_kv_norm_g[l])
        dq_g, dk_g = d_q_norm_g[l][None, :], d_k_norm_g[l][None, :]
        wo = w_out[l].astype(BF16).reshape(4, GROUP_WIDTH, d)
        lam_init = 0.8 - 0.6 * math.exp(-0.3 * l)
        diff = (b_lambda[l], b_subln_g[l][None, :], lam_init)
        sink2 = a_sink[l].reshape(2, 2)

        def project(stream, row_fn, tabs, tm):
            norm = (norm_g[l][None, :], mods, l, row_fn)
            pa, pd = _project(_proj_ad_kernel, "proj_ad", stream, norm, [wa, wd, dq_g, dk_g], tabs["h"],
                              min(tm, TM_QKNORM), [(4, 128, 2, 128), (4, 128, 2, 128)])
            pb, = _project(_proj_b_kernel, "proj_b", stream, norm, [wb], tabs["b"], tm, [(8, 128, 4, 128)])
            pc, = _project(_proj_c_kernel, "proj_c", stream, norm, [wc, uq, ukv, gq_pad, gkv], tabs["b"],
                           tm, [(4, 256, 4, 256)])
            return pa, pb, pc, pd

        def sink_rows(tq):
            return jnp.repeat(sink2, tq, axis=1).reshape(2, 1, 2 * tq)

        (qa, ka, vta, gza), (qb, kb, vtb, gzb), (qc, kc, vtc, gzc), (qd, kd, vtd, gzd) = project(
            x, lat_row, tab_lat, TM_LATENT)
        (qa_c, ka_c, vta_c, gza_c), (qb_c, kb_c, vtb_c, gzb_c), (qc_c, kc_c, vtc_c, gzc_c), \
            (qd_c, kd_c, vtd_c, gzd_c) = project(ctx, ctx_row, tab_ctx, tm_ctx)

        ua = _band_attention(qa, ka_c, vta_c, ka, vta, gza, sink_rows(WINDOW), tc)
        ub = _dense_attention("attn_b", qb, kb_c, vtb_c, kb, vtb, gzb, 2, tq2, bsz, tc, diff=diff)
        uc = _dense_attention("attn_c", qc, kc_c, vtc_c, kc, vtc, gzc, 1, tq1, bsz, tc)
        ud = _dense_attention("attn_d", qd, kd_c, vtd_c, kd, vtd, gzd, 2, tq2, bsz, tc)
        if not last:
            ua_c = _dense_attention("attn_a_ctx", qa_c, ka_c, vta_c, None, None, gza_c, 2, tc, bsz, tc,
                                    sink=sink_rows(tc))
            ub_c = _dense_attention("attn_b_ctx", qb_c, kb_c, vtb_c, None, None, gzb_c, 2, tc, bsz, tc,
                                    diff=diff)
            uc_c = _dense_attention("attn_c_ctx", qc_c, kc_c, vtc_c, None, None, gzc_c, 1, tc, bsz, tc)
            ud_c = _dense_attention("attn_d_ctx", qd_c, kd_c, vtd_c, None, None, gzd_c, 2, tc, bsz, tc)
            ctx = _out_project((ua_c, ub_c, uc_c, ud_c), wo, ctx, mods, l, ctx_row, tm_ctx)
        x = _out_project((ua, ub, uc, ud), wo, x, mods, l, lat_row, TM_OUT,
                         final_g=final_norm_g[None, :] if last else None)
    return x
```

```python
import functools
import math

import jax
import jax.numpy as jnp
from jax import lax
from jax.experimental import pallas as pl
from jax.experimental.pallas import tpu as pltpu

F32 = jnp.float32
BF16 = jnp.bfloat16

D_MODEL = 2048
HEAD_DIM = 128
GRID_W = 64
GROUP_WIDTH = 512
WINDOW = 128
ROPE_THETA = 10000.0
EPS = 1e-6
NEG_INF = -1e30
C_Q_LORA = 448
C_ROPE = 64
C_QK_DIM = 192
B_QK_DIM = 64
LOG2E = math.log2(math.e)
LANES = 128
ONES_ROWS = 16
DV_AUG = HEAD_DIM + ONES_ROWS
MOD_ROWS = 16
VMEM_LIMIT = 56 * 1024 * 1024

TM_LATENT = 1024
TM_OUT = 1024
KV_CHUNK = 512
N_DENSE = 4096
TQ_BAND = 1024
BAND_AHEAD = 3
MOD_TN = 512
COL_TILE = 256


def _dotf(a, b):
    return jnp.dot(a, b, preferred_element_type=F32)


def _dot_nt(a, b):
    return lax.dot_general(a, b, (((1,), (1,)), ((), ())), preferred_element_type=F32)


def _silu(z):
    return z / (1.0 + jnp.exp(-z))


def _rms(y, inv_n):
    return y * lax.rsqrt(jnp.sum(y * y, axis=-1, keepdims=True) * inv_n + EPS)


def _rope(x, cos, sin_signed, shift):
    lane = lax.broadcasted_iota(jnp.int32, x.shape, 1)
    even = (lane & shift) == 0
    x_up = pltpu.roll(x, LANES - shift, 1)
    x_dn = pltpu.roll(x, shift, 1)
    return x * cos + jnp.where(even, x_up, x_dn) * sin_signed


def _params(n_axes):
    return pltpu.CompilerParams(dimension_semantics=("parallel",) * n_axes,
                                vmem_limit_bytes=VMEM_LIMIT)


def _mod_kernel(c_ref, w_ref, b_ref, o_ref):
    a = _silu(c_ref[...]).astype(BF16)
    o_ref[0] = _dotf(a, w_ref[0].astype(BF16)) + b_ref[0]


def _modulation(cc, w_mod, b_mod):
    depth, d, n = w_mod.shape
    return pl.pallas_call(
        _mod_kernel,
        grid=(depth, n // MOD_TN),
        in_specs=[pl.BlockSpec((MOD_ROWS, d), lambda l, j: (0, 0)),
                  pl.BlockSpec((1, d, MOD_TN), lambda l, j: (l, 0, j)),
                  pl.BlockSpec((1, 1, MOD_TN), lambda l, j: (l, 0, j))],
        out_specs=pl.BlockSpec((1, MOD_ROWS, MOD_TN), lambda l, j: (l, 0, j)),
        out_shape=jax.ShapeDtypeStruct((depth, MOD_ROWS, n), F32),
        compiler_params=_params(2),
        name="modulation",
    )(cc, w_mod, b_mod.reshape(depth, 1, n))


def _prenorm(x_ref, g_ref, shift_ref, scale_ref):
    y = _rms(x_ref[0], 1.0 / D_MODEL) * g_ref[...]
    return (y * (1.0 + scale_ref[0, 0]) + shift_ref[0, 0]).astype(BF16)


def _mod_spec(layer, row_fn, which):
    return pl.BlockSpec((1, 1, 1, D_MODEL), lambda b, i: (layer, row_fn(b), 0, which))


def _store_vt(vt_ref, head, y):
    chunk = vt_ref.shape[-1]
    for j in range(vt_ref.shape[2]):
        vt_ref[0, head, j, 0:HEAD_DIM, :] = y[j * chunk:(j + 1) * chunk, :].T.astype(BF16)
        vt_ref[0, head, j, HEAD_DIM:DV_AUG, :] = jnp.ones((ONES_ROWS, chunk), BF16)


def _proj_a_body(h, w_ref, cos, sin, q_ref, k_ref, vt_ref, gz_ref):
    scale = LOG2E / math.sqrt(HEAD_DIM)
    yq = _dotf(h, w_ref[:, 0:512])
    for hh in range(4):
        q_ref[0, hh] = (_rope(yq[:, hh * 128:(hh + 1) * 128], cos, sin, 32) * scale).astype(BF16)
    ykv = _dotf(h, w_ref[:, 512:1024])
    for hh in range(2):
        k_ref[0, hh] = _rope(ykv[:, hh * 128:(hh + 1) * 128], cos, sin, 32).astype(BF16)
        _store_vt(vt_ref, hh, ykv[:, 256 + hh * 128:256 + (hh + 1) * 128])
    gz_ref[0] = _silu(_dotf(h, w_ref[:, 1024:1536])).astype(BF16)


def _proj_b_kernel(x_ref, g_ref, shift_ref, scale_ref, w_ref, cos_ref, sin_ref,
                   q_ref, k_ref, vt_ref, gz_ref):
    h = _prenorm(x_ref, g_ref, shift_ref, scale_ref)
    cos, sin = cos_ref[...], sin_ref[...]
    scale = LOG2E / math.sqrt(B_QK_DIM)
    yq = _dotf(h, w_ref[:, 0:512])
    lane = lax.broadcasted_iota(jnp.int32, (h.shape[0], LANES), 1)
    first = lane < B_QK_DIM
    for hh in range(4):
        y = _rope(yq[:, hh * 128:(hh + 1) * 128], cos, sin, 16) * scale
        q_ref[0, 2 * hh] = jnp.where(first, y, 0.0).astype(BF16)
        q_ref[0, 2 * hh + 1] = jnp.where(first, 0.0, y).astype(BF16)
    yk = _dotf(h, w_ref[:, 512:1024])
    for hh in range(4):
        k_ref[0, hh] = _rope(yk[:, hh * 128:(hh + 1) * 128], cos, sin, 16).astype(BF16)
    yv = _dotf(h, w_ref[:, 1024:1536])
    for hh in range(4):
        _store_vt(vt_ref, hh, yv[:, hh * 128:(hh + 1) * 128])
    gz_ref[0] = _silu(_dotf(h, w_ref[:, 1536:2048])).astype(BF16)


def _proj_c_kernel(x_ref, g_ref, shift_ref, scale_ref, w_ref, wuq_ref, wukv_ref, gq_ref, gkv_ref,
                   cos_ref, sin_ref, q_ref, k_ref, vt_ref, gz_ref):
    h = _prenorm(x_ref, g_ref, shift_ref, scale_ref)
    cos, sin = cos_ref[...], sin_ref[...]
    scale = LOG2E / math.sqrt(C_QK_DIM)
    ycq = _dotf(h, w_ref[:, 0:512])
    yckv = _dotf(h, w_ref[:, 512:768])
    gz_ref[0] = _silu(_dotf(h, w_ref[:, 768:1280])).astype(BF16)
    cqn = (_rms(ycq, 1.0 / C_Q_LORA) * gq_ref[...]).astype(BF16)
    yq = _dotf(cqn, wuq_ref[...])
    ckvn = (_rms(yckv[:, 0:128], 1.0 / 128) * gkv_ref[...]).astype(BF16)
    ykv = _dotf(ckvn, wukv_ref[...])
    kr_even = _rope(yckv[:, 128:256], cos, sin, 16)
    kr_odd = pltpu.roll(kr_even, C_ROPE, 1)
    qr = [_rope(yq[:, 512 + p * 128:512 + (p + 1) * 128], cos, sin, 16) * scale for p in range(2)]
    for hh in range(4):
        q_ref[0, hh, :, 0:128] = (yq[:, hh * 128:(hh + 1) * 128] * scale).astype(BF16)
        q_ref[0, hh, :, 128:256] = qr[hh // 2].astype(BF16)
        k_ref[0, hh, :, 0:128] = ykv[:, hh * 256:hh * 256 + 128].astype(BF16)
        k_ref[0, hh, :, 128:256] = (kr_even if hh % 2 == 0 else kr_odd).astype(BF16)
        _store_vt(vt_ref, hh, ykv[:, hh * 256 + 128:(hh + 1) * 256])


def _proj_ad_kernel(x_ref, g_ref, shift_ref, scale_ref, wa_ref, wd_ref, gq_ref, gk_ref, cos_ref, sin_ref,
                    qa_ref, ka_ref, vta_ref, gza_ref, qd_ref, kd_ref, vtd_ref, gzd_ref):
    h = _prenorm(x_ref, g_ref, shift_ref, scale_ref)
    cos, sin = cos_ref[...], sin_ref[...]
    scale = LOG2E / math.sqrt(HEAD_DIM)
    yq = _dotf(h, wd_ref[:, 0:512])
    for hh in range(4):
        yn = _rms(yq[:, hh * 128:(hh + 1) * 128], 1.0 / HEAD_DIM) * gq_ref[...]
        qd_ref[0, hh] = (_rope(yn, cos, sin, 32) * scale).astype(BF16)
    yk = _dotf(h, wd_ref[:, 512:768])
    for hh in range(2):
        yn = _rms(yk[:, hh * 128:(hh + 1) * 128], 1.0 / HEAD_DIM) * gk_ref[...]
        kd_ref[0, hh] = _rope(yn, cos, sin, 32).astype(BF16)
    gzd_ref[0] = _silu(_dotf(h, wd_ref[:, 1024:1536])).astype(BF16)
    _proj_a_body(h, wa_ref, cos, sin, qa_ref, ka_ref, vta_ref, gza_ref)
    yv = _dotf(h, wd_ref[:, 768:1024])
    for hh in range(2):
        _store_vt(vtd_ref, hh, yv[:, hh * 128:(hh + 1) * 128])


def _full_spec(a):
    nd = a.ndim
    return pl.BlockSpec(a.shape, lambda b, i: (0,) * nd, pipeline_mode=pl.Buffered(1))


def _project(kern, name, x, norm, weights, tables, tm, mixers):
    bsz, t, d = x.shape
    nt = t // tm
    chunk = min(tm, KV_CHUNK)
    sub = tm // chunk
    gain, mods, layer, row_fn = norm
    tab_specs = [pl.BlockSpec((tm, LANES), (lambda b, i: (i, 0)) if tab.shape[0] == t
                              else (lambda b, i: (0, 0))) for tab in tables]
    out_specs, out_shape = [], []
    for n_q, dq, n_kv, dk in mixers:
        out_specs += [pl.BlockSpec((1, n_q, tm, dq), lambda b, i: (b, 0, i, 0)),
                      pl.BlockSpec((1, n_kv, tm, dk), lambda b, i: (b, 0, i, 0)),
                      pl.BlockSpec((1, n_kv, sub, DV_AUG, chunk), lambda b, i: (b, 0, i, 0, 0)),
                      pl.BlockSpec((1, tm, GROUP_WIDTH), lambda b, i: (b, i, 0))]
        out_shape += [jax.ShapeDtypeStruct((bsz, n_q, t, dq), BF16),
                      jax.ShapeDtypeStruct((bsz, n_kv, t, dk), BF16),
                      jax.ShapeDtypeStruct((bsz, n_kv, nt * sub, DV_AUG, chunk), BF16),
                      jax.ShapeDtypeStruct((bsz, t, GROUP_WIDTH), BF16)]
    outs = pl.pallas_call(
        kern,
        grid=(bsz, nt),
        in_specs=[pl.BlockSpec((1, tm, d), lambda b, i: (b, i, 0)),
                  pl.BlockSpec((1, d), lambda b, i: (0, 0)),
                  _mod_spec(layer, row_fn, 0), _mod_spec(layer, row_fn, 1)]
        + [_full_spec(w) for w in weights] + tab_specs,
        out_specs=out_specs,
        out_shape=out_shape,
        compiler_params=_params(2),
        name=name,
    )(x, gain, mods, mods, *weights, *tables)
    return [tuple(outs[4 * j:4 * j + 4]) for j in range(len(mixers))]


def _finish_plain(acc, l, gate_ref, out_ref, n_sets, tq):
    o_t = acc * (1.0 / l)
    for g in range(n_sets):
        o = o_t[:, g * tq:(g + 1) * tq].astype(BF16).T
        out_ref[0, :, g * 128:(g + 1) * 128] = o * gate_ref[0, :, g * 128:(g + 1) * 128]


def _finish_diff(acc, l, gate_ref, out_ref, lam_ref, subg_ref, lam_init, tq):
    o_t = acc * (1.0 / l)
    lp = lam_ref[...]
    s1 = jnp.sum(lp[0:1] * lp[1:2], axis=1, keepdims=True)
    s2 = jnp.sum(lp[2:3] * lp[3:4], axis=1, keepdims=True)
    lam = jnp.exp(s1) - jnp.exp(s2) + lam_init
    d = o_t[:, 0:tq] - lam * o_t[:, tq:2 * tq]
    dn = d * lax.rsqrt(jnp.mean(d * d, axis=0, keepdims=True) + EPS)
    o = dn.T * subg_ref[...] * (1.0 - lam_init)
    out_ref[0] = (o * gate_ref[0].astype(F32)).astype(BF16)


def _dense_attn_kernel(*refs, n_sets, tq, n_chunks, has_sink, diff, lam_init):
    refs = list(refs)
    q_ref, kc_ref, vtc_ref = refs[:3]
    pos = 3
    if n_chunks:
        kl_ref, vtl_ref = refs[pos:pos + 2]
        pos += 2
    gate_ref = refs[pos]
    pos += 1
    if has_sink:
        sink_ref = refs[pos]
        pos += 1
    if diff:
        lam_ref, subg_ref = refs[pos:pos + 2]
        pos += 2
    out_ref, acc_ref = refs[pos:pos + 2]
    s_refs = refs[pos + 2:pos + 4]

    n = n_sets * tq
    n_tiles = n // COL_TILE
    per_set = tq // COL_TILE
    tc = kc_ref.shape[2]
    acc_ref[0:HEAD_DIM, :] = jnp.zeros((HEAD_DIM, n), F32)
    if has_sink:
        m0 = sink_ref[0] * LOG2E
        acc_ref[HEAD_DIM:DV_AUG, :] = jnp.ones((ONES_ROWS, n), F32)
    else:
        m0 = jnp.full((1, n), NEG_INF, F32)
        acc_ref[HEAD_DIM:DV_AUG, :] = jnp.zeros((ONES_ROWS, n), F32)

    def cols(c):
        return slice(c * COL_TILE, (c + 1) * COL_TILE)

    def scores(k, c, s_ref, rows):
        qt = q_ref[0, c // per_set, (c % per_set) * COL_TILE:(c % per_set + 1) * COL_TILE, :]
        s = _dot_nt(k, qt)
        s_ref[0:rows, cols(c)] = s
        return jnp.max(s, axis=0, keepdims=True)

    def softmax_pv(s_ref, rows, cmax, vt, m, c):
        m_new = jnp.maximum(m, cmax)
        alpha = jnp.exp2(m - m_new)
        p = jnp.exp2(s_ref[0:rows, cols(c)] - m_new)
        pv = _dotf(vt, p.astype(BF16))
        acc_ref[:, cols(c)] = alpha * acc_ref[:, cols(c)] + pv
        return m_new

    def stage(k_next, s_next, s_cur, rows_cur, cmax_cur, vt_cur, m):
        ms, cm = [], []
        for c in range(n_tiles):
            if k_next is not None:
                cm.append(scores(k_next, c, s_next, KV_CHUNK))
            if s_cur is not None:
                ms.append(softmax_pv(s_cur, rows_cur, cmax_cur[:, cols(c)], vt_cur, m[:, cols(c)], c))
        cat = lambda xs: jnp.concatenate(xs, axis=1) if xs else None
        return cat(cm), cat(ms)

    s_a, s_b = s_refs
    k_ctx = kc_ref[0, 0]
    cmax = jnp.concatenate([scores(k_ctx, c, s_a, tc) for c in range(n_tiles)], axis=1)
    if not n_chunks:
        _, m = stage(None, None, s_a, tc, cmax, vtc_ref[0, 0, 0], m0)
    else:
        def k_chunk(j):
            start = pl.multiple_of(j * KV_CHUNK, KV_CHUNK)
            return kl_ref[0, 0, pl.ds(start, KV_CHUNK), :]

        cmax, m = stage(k_chunk(0), s_b, s_a, tc, cmax, vtc_ref[0, 0, 0], m0)

        def body(i, carry):
            cmax, m = carry
            j = 2 * i
            cmax, m = stage(k_chunk(j + 1), s_a, s_b, KV_CHUNK, cmax, vtl_ref[0, 0, j], m)
            return stage(k_chunk(j + 2), s_b, s_a, KV_CHUNK, cmax, vtl_ref[0, 0, j + 1], m)

        cmax, m = lax.fori_loop(0, n_chunks // 2 - 1, body, (cmax, m))
        j = n_chunks - 2
        cmax, m = stage(k_chunk(j + 1), s_a, s_b, KV_CHUNK, cmax, vtl_ref[0, 0, j], m)
        _, m = stage(None, None, s_a, KV_CHUNK, cmax, vtl_ref[0, 0, j + 1], m)

    acc = acc_ref[0:HEAD_DIM, :]
    l = acc_ref[HEAD_DIM:HEAD_DIM + 1, :]
    if diff:
        _finish_diff(acc, l, gate_ref, out_ref, lam_ref, subg_ref, lam_init, tq)
    else:
        _finish_plain(acc, l, gate_ref, out_ref, n_sets, tq)


def _dense_attention(name, q, kc, vtc, kl, vtl, gate, n_sets, tq, sink=None, diff=None):
    bsz, _, t, dk = q.shape
    hkv, tc = kc.shape[1], kc.shape[2]
    dv = vtc.shape[3]
    n_chunks = 0 if kl is None else kl.shape[2] // KV_CHUNK
    assert n_chunks % 2 == 0 and tq % COL_TILE == 0
    out_w = LANES if diff else n_sets * LANES
    n = n_sets * tq
    args = [q, kc, vtc]
    specs = [pl.BlockSpec((1, n_sets, tq, dk), lambda b, h, i: (b, h, i, 0)),
             pl.BlockSpec((1, 1, tc, dk), lambda b, h, i: (b, h, 0, 0)),
             pl.BlockSpec((1, 1, 1, dv, tc), lambda b, h, i: (b, h, 0, 0, 0))]
    if n_chunks:
        args += [kl, vtl]
        specs += [pl.BlockSpec((1, 1, kl.shape[2], dk), lambda b, h, i: (b, h, 0, 0)),
                  pl.BlockSpec((1, 1, n_chunks, dv, KV_CHUNK), lambda b, h, i: (b, h, 0, 0, 0))]
    args.append(gate)
    specs.append(pl.BlockSpec((1, tq, out_w), lambda b, h, i: (b, i, h)))
    if sink is not None:
        args.append(sink)
        specs.append(pl.BlockSpec((1, 1, n), lambda b, h, i: (h, 0, 0)))
    lam_init = 0.0
    if diff is not None:
        lam_par, subg, lam_init = diff
        args += [lam_par, subg]
        specs += [pl.BlockSpec(lam_par.shape, lambda b, h, i: (0, 0)),
                  pl.BlockSpec(subg.shape, lambda b, h, i: (0, 0))]
    kern = functools.partial(_dense_attn_kernel, n_sets=n_sets, tq=tq, n_chunks=n_chunks,
                             has_sink=sink is not None, diff=diff is not None, lam_init=lam_init)
    return pl.pallas_call(
        kern,
        grid=(bsz, hkv, t // tq),
        in_specs=specs,
        out_specs=pl.BlockSpec((1, tq, out_w), lambda b, h, i: (b, i, h)),
        out_shape=jax.ShapeDtypeStruct((bsz, t, GROUP_WIDTH), BF16),
        scratch_shapes=[pltpu.VMEM((dv, n), F32)] + [pltpu.VMEM((max(KV_CHUNK, tc), n), F32)] * 2,
        compiler_params=_params(3),
        name=name,
    )(*args)


def _band_attn_kernel(q_ref, kc_ref, vtc_ref, kl_ref, vtl_ref, gate_ref, sink_ref, out_ref, *bufs,
                      blocks, n_chunks):
    w = WINDOW
    per_chunk = KV_CHUNK // w
    tc = kc_ref.shape[2]
    i = pl.program_id(2)
    first_step = i == 0
    last_step = i == pl.num_programs(2) - 1
    r1 = lax.broadcasted_iota(jnp.int32, (w, 1), 0)
    qc = lax.broadcasted_iota(jnp.int32, (1, 2 * w), 1) & (w - 1)
    prev_ok = r1 >= qc
    next_ok = r1 <= qc
    sink = sink_ref[0] * LOG2E
    k_ctx = kc_ref[0, 0]

    def pieces(t):
        chunk = i * (blocks // per_chunk) + t // per_chunk
        tt = t % per_chunk
        if tt < per_chunk - 1:
            pair = (chunk * KV_CHUNK + tt * w, chunk, tt * w, [(w, next_ok)], None)
            if tt > 0:
                single = (chunk * KV_CHUNK + (tt - 1) * w, chunk, (tt - 1) * w, [(0, prev_ok)], None)
            else:
                prev_chunk = jnp.maximum(chunk - 1, 0)
                bias = jnp.where(first_step, NEG_INF, 0.0) if t == 0 else None
                single = (prev_chunk * KV_CHUNK + (per_chunk - 1) * w, prev_chunk, (per_chunk - 1) * w,
                          [(0, prev_ok)], bias)
        else:
            pair = (chunk * KV_CHUNK + (tt - 1) * w, chunk, (tt - 1) * w, [(0, prev_ok)], None)
            next_chunk = jnp.minimum(chunk + 1, n_chunks - 1)
            bias = jnp.where(last_step, NEG_INF, 0.0) if t == blocks - 1 else None
            single = (next_chunk * KV_CHUNK, next_chunk, 0, [(0, next_ok)], bias)
        return pair, single

    def q_block(t):
        rows = slice(t * w, (t + 1) * w)
        return jnp.concatenate([q_ref[0, 0, rows, :], q_ref[0, 1, rows, :]], axis=0)

    def scores(t, s_ref):
        qt = q_block(t)
        s_ctx = _dot_nt(k_ctx, qt)
        s_ref[0:tc, :] = s_ctx
        cmax = jnp.max(s_ctx, axis=0, keepdims=True)
        row0 = tc
        for (start, _, _, masked, bias), rows in zip(pieces(t), (2 * w, w)):
            k = kl_ref[0, 0, pl.ds(pl.multiple_of(start, w), rows), :]
            s = _dot_nt(k, qt)
            if bias is not None:
                s = s + bias
            parts = []
            for r0 in range(0, rows, w):
                part = s[r0:r0 + w]
                for off, mask in masked:
                    if off == r0:
                        part = jnp.where(mask, part, NEG_INF)
                parts.append(part)
                cmax = jnp.maximum(cmax, jnp.max(part, axis=0, keepdims=True))
            for j, part in enumerate(parts):
                s_ref[row0 + j * w:row0 + (j + 1) * w, :] = part
            row0 += rows
        return cmax

    def softmax_pv(t, s_ref, cmax):
        m = jnp.maximum(sink, cmax)
        acc = _dotf(vtc_ref[0, 0, 0], jnp.exp2(s_ref[0:tc, :] - m).astype(BF16))
        row0 = tc
        for (_, chunk, lane0, _, _), rows in zip(pieces(t), (2 * w, w)):
            vt = vtl_ref[0, 0, chunk, :, lane0:lane0 + rows]
            acc = acc + _dotf(vt, jnp.exp2(s_ref[row0:row0 + rows, :] - m).astype(BF16))
            row0 += rows
        l = acc[HEAD_DIM:HEAD_DIM + 1, :] + jnp.exp2(sink - m)
        o_t = acc[0:HEAD_DIM, :] * (1.0 / l)
        rows = slice(t * w, (t + 1) * w)
        for g in range(2):
            gcols = slice(g * LANES, (g + 1) * LANES)
            o = o_t[:, g * w:(g + 1) * w].T
            out_ref[0, rows, gcols] = (o * gate_ref[0, rows, gcols].astype(F32)).astype(BF16)

    n_buf = len(bufs)
    ahead = n_buf - 1
    cmaxes = [scores(t, bufs[t % n_buf]) for t in range(min(ahead, blocks))]
    for t in range(blocks):
        if t + ahead < blocks:
            cmaxes.append(scores(t + ahead, bufs[(t + ahead) % n_buf]))
        softmax_pv(t, bufs[t % n_buf], cmaxes[t])


def _band_attention(q, kc, vtc, kl, vtl, gate, sink):
    bsz, _, s, dk = q.shape
    tq = min(TQ_BAND, s)
    tc = kc.shape[2]
    n_chunks = s // KV_CHUNK
    assert tq % KV_CHUNK == 0 and dk == HEAD_DIM
    return pl.pallas_call(
        functools.partial(_band_attn_kernel, blocks=tq // WINDOW, n_chunks=n_chunks),
        grid=(bsz, 2, s // tq),
        in_specs=[
            pl.BlockSpec((1, 2, tq, dk), lambda b, h, i: (b, h, i, 0)),
            pl.BlockSpec((1, 1, tc, dk), lambda b, h, i: (b, h, 0, 0)),
            pl.BlockSpec((1, 1, 1, DV_AUG, tc), lambda b, h, i: (b, h, 0, 0, 0)),
            pl.BlockSpec((1, 1, s, dk), lambda b, h, i: (b, h, 0, 0)),
            pl.BlockSpec((1, 1, n_chunks, DV_AUG, KV_CHUNK), lambda b, h, i: (b, h, 0, 0, 0)),
            pl.BlockSpec((1, tq, 2 * LANES), lambda b, h, i: (b, i, h)),
            pl.BlockSpec((1, 1, 2 * WINDOW), lambda b, h, i: (h, 0, 0)),
        ],
        out_specs=pl.BlockSpec((1, tq, 2 * LANES), lambda b, h, i: (b, i, h)),
        out_shape=jax.ShapeDtypeStruct((bsz, s, GROUP_WIDTH), BF16),
        scratch_shapes=[pltpu.VMEM((tc + 3 * WINDOW, 2 * WINDOW), F32)] * (BAND_AHEAD + 1),
        compiler_params=_params(3),
        name="attn_a_band",
    )(q, kc, vtc, kl, vtl, gate, sink)


def _out_kernel(*refs, final):
    ua, ub, uc, ud, w_ref, x_ref, gate_ref = refs[:7]
    out_ref = refs[-1]
    u = jnp.concatenate([ua[0], ub[0], uc[0], ud[0]], axis=1)
    acc = _dotf(u, w_ref[...].reshape(4 * GROUP_WIDTH, D_MODEL))
    xn = x_ref[0] + gate_ref[0, 0] * acc
    if final:
        xn = _rms(xn, 1.0 / D_MODEL) * refs[7][...]
    out_ref[0] = xn


def _out_project(us, w_out, x, mods, layer, row_fn, tm, final_g=None):
    bsz, t, d = x.shape
    u_spec = pl.BlockSpec((1, tm, GROUP_WIDTH), lambda b, i: (b, i, 0))
    specs = [u_spec] * 4 + [_full_spec(w_out), pl.BlockSpec((1, tm, d), lambda b, i: (b, i, 0)),
                            _mod_spec(layer, row_fn, 2)]
    args = list(us) + [w_out, x, mods]
    if final_g is not None:
        specs.append(pl.BlockSpec((1, d), lambda b, i: (0, 0)))
        args.append(final_g)
    return pl.pallas_call(
        functools.partial(_out_kernel, final=final_g is not None),
        grid=(bsz, t // tm),
        in_specs=specs,
        out_specs=pl.BlockSpec((1, tm, d), lambda b, i: (b, i, 0)),
        out_shape=jax.ShapeDtypeStruct((bsz, t, d), F32),
        compiler_params=_params(2),
        name="out_proj",
    )(*args)


def _rope_tables(rows, rot_dim):
    row = jnp.broadcast_to(jnp.arange(rows)[:, None], (rows, GRID_W)).reshape(-1).astype(F32)
    col = jnp.broadcast_to(jnp.arange(GRID_W)[None, :], (rows, GRID_W)).reshape(-1).astype(F32)
    axis_dim = rot_dim // 2
    inv_freq = ROPE_THETA ** (-jnp.arange(0, axis_dim, 2, dtype=F32) / axis_dim)
    ang_r = row[:, None] * inv_freq[None, :]
    ang_c = col[:, None] * inv_freq[None, :]
    ang = jnp.concatenate([ang_r, ang_r, ang_c, ang_c], axis=-1)
    quarter = rot_dim // 4
    sign = jnp.where((jnp.arange(rot_dim) // quarter) % 2 == 0, -1.0, 1.0).astype(F32)
    cos, sin = jnp.cos(ang), jnp.sin(ang) * sign
    reps = LANES // rot_dim
    return jnp.tile(cos, (1, reps)), jnp.tile(sin, (1, reps))


def _layer_weights(w_in, w_uq, w_ukv, gq, gkv):
    wa = w_in[:, 0:1536]
    wb = w_in[:, 1536:3584]
    zeros64 = jnp.zeros((w_in.shape[0], 64), w_in.dtype)
    wc = jnp.concatenate([w_in[:, 3584:4032], zeros64,
                          w_in[:, 4032:4160],
                          w_in[:, 4160:4224], zeros64,
                          w_in[:, 4224:4736]], axis=1)
    wd = w_in[:, 4736:6272]
    uq = w_uq.reshape(C_Q_LORA, 4, C_QK_DIM)
    uq = jnp.concatenate([uq[:, :, :128].reshape(C_Q_LORA, 512),
                          uq[:, :, 128:].reshape(C_Q_LORA, 256)], axis=1)
    uq = jnp.concatenate([uq, jnp.zeros((64, 768), uq.dtype)], axis=0)
    gq_pad = jnp.concatenate([gq, jnp.zeros((64,), gq.dtype)])[None, :]
    cast = lambda a: a.astype(BF16)
    return cast(wa), cast(wb), cast(wc), cast(wd), cast(uq), cast(w_ukv), gq_pad, gkv[None, :]


def kernel(x, c, ctx, c_ctx, w_mod, b_mod, norm_g, w_in, c_q_norm_g, c_kv_norm_g, c_w_uq, c_w_ukv,
           d_q_norm_g, d_k_norm_g, a_sink, b_lambda, b_subln_g, w_out, final_norm_g):
    bsz, s, d = x.shape
    tc = ctx.shape[1]
    depth = w_mod.shape[0]
    assert d == D_MODEL and s % TM_LATENT == 0 and s % GRID_W == 0 and bsz + 1 <= MOD_ROWS
    assert tc % LANES == 0 and TM_LATENT % KV_CHUNK == 0 and tc <= KV_CHUNK

    cc = jnp.concatenate([c, c_ctx[None, :], jnp.zeros((MOD_ROWS - bsz - 1, d), c.dtype)], axis=0)
    mods = _modulation(cc, w_mod, b_mod).reshape(depth, MOD_ROWS, 1, 3 * d)
    lat_row = lambda b: b
    ctx_row = lambda b: bsz

    rows = s // GRID_W
    cos_h, sin_h = _rope_tables(rows, HEAD_DIM)
    cos_b, sin_b = _rope_tables(rows, B_QK_DIM)
    one = jnp.ones((tc, LANES), F32)
    zero = jnp.zeros((tc, LANES), F32)
    tab_lat = {"h": (cos_h, sin_h), "b": (cos_b, sin_b)}
    tab_ctx = {"h": (one, zero), "b": (one, zero)}

    tq2, tq1 = min(N_DENSE // 2, s), min(N_DENSE, s)
    for l in range(depth):
        last = l == depth - 1
        wa, wb, wc, wd, uq, ukv, gq_pad, gkv = _layer_weights(
            w_in[l], c_w_uq[l], c_w_ukv[l], c_q_norm_g[l], c_kv_norm_g[l])
        dq_g, dk_g = d_q_norm_g[l][None, :], d_k_norm_g[l][None, :]
        wo = w_out[l].astype(BF16).reshape(4, GROUP_WIDTH, d)
        lam_init = 0.8 - 0.6 * math.exp(-0.3 * l)
        diff = (b_lambda[l], b_subln_g[l][None, :], lam_init)
        sink2 = a_sink[l].reshape(2, 2)

        def project(stream, row_fn, tabs, tm):
            norm = (norm_g[l][None, :], mods, l, row_fn)
            pa, pd = _project(_proj_ad_kernel, "proj_ad", stream, norm, [wa, wd, dq_g, dk_g], tabs["h"],
                              tm, [(4, 128, 2, 128), (4, 128, 2, 128)])
            pb, = _project(_proj_b_kernel, "proj_b", stream, norm, [wb], tabs["b"], tm, [(8, 128, 4, 128)])
            pc, = _project(_proj_c_kernel, "proj_c", stream, norm, [wc, uq, ukv, gq_pad, gkv], tabs["b"],
                           tm, [(4, 256, 4, 256)])
            return pa, pb, pc, pd

        def sink_rows(tq):
            return jnp.repeat(sink2, tq, axis=1).reshape(2, 1, 2 * tq)

        (qa, ka, vta, gza), (qb, kb, vtb, gzb), (qc, kc, vtc, gzc), (qd, kd, vtd, gzd) = project(
            x, lat_row, tab_lat, TM_LATENT)
        (qa_c, ka_c, vta_c, gza_c), (qb_c, kb_c, vtb_c, gzb_c), (qc_c, kc_c, vtc_c, gzc_c), \
            (qd_c, kd_c, vtd_c, gzd_c) = project(ctx, ctx_row, tab_ctx, tc)

        ua = _band_attention(qa, ka_c, vta_c, ka, vta, gza, sink_rows(WINDOW))
        ub = _dense_attention("attn_b", qb, kb_c, vtb_c, kb, vtb, gzb, 2, tq2, diff=diff)
        uc = _dense_attention("attn_c", qc, kc_c, vtc_c, kc, vtc, gzc, 1, tq1)
        ud = _dense_attention("attn_d", qd, kd_c, vtd_c, kd, vtd, gzd, 2, tq2)
        if not last:
            ua_c = _dense_attention("attn_a_ctx", qa_c, ka_c, vta_c, None, None, gza_c, 2, tc,
                                    sink=sink_rows(tc))
            ub_c = _dense_attention("attn_b_ctx", qb_c, kb_c, vtb_c, None, None, gzb_c, 2, tc, diff=diff)
            uc_c = _dense_attention("attn_c_ctx", qc_c, kc_c, vtc_c, None, None, gzc_c, 1, tc)
            ud_c = _dense_attention("attn_d_ctx", qd_c, kd_c, vtd_c, None, None, gzd_c, 2, tc)
            ctx = _out_project((ua_c, ub_c, uc_c, ud_c), wo, ctx, mods, l, ctx_row, tc)
        x = _out_project((ua, ub, uc, ud), wo, x, mods, l, lat_row, TM_OUT,
                         final_g=final_norm_g[None, :] if last else None)
    return x
```

```python
import functools
import math

import jax
import jax.numpy as jnp
from jax import lax
from jax.experimental import pallas as pl
from jax.experimental.pallas import tpu as pltpu

F32 = jnp.float32
BF16 = jnp.bfloat16

D_MODEL = 2048
HEAD_DIM = 128
GRID_W = 64
GROUP_WIDTH = 512
WINDOW = 128
ROPE_THETA = 10000.0
EPS = 1e-6
NEG_INF = -1e30
C_Q_LORA = 448
C_ROPE = 64
C_QK_DIM = 192
B_QK_DIM = 64
LOG2E = math.log2(math.e)
LANES = 128
ONES_ROWS = 16
DV_AUG = HEAD_DIM + ONES_ROWS
MOD_ROWS = 16
VMEM_LIMIT = 56 * 1024 * 1024

TM_LATENT = 1024
TM_OUT = 1024
KV_CHUNK = 512
N_DENSE = 4096
TQ_BAND = 1024
BAND_AHEAD = 3
MOD_TN = 512
COL_TILE = 256


def _dotf(a, b):
    return jnp.dot(a, b, preferred_element_type=F32)


def _dot_nt(a, b):
    return lax.dot_general(a, b, (((1,), (1,)), ((), ())), preferred_element_type=F32)


def _silu(z):
    return z / (1.0 + jnp.exp(-z))


def _rms(y, inv_n):
    return y * lax.rsqrt(jnp.sum(y * y, axis=-1, keepdims=True) * inv_n + EPS)


def _rope(x, cos, sin_signed, shift):
    lane = lax.broadcasted_iota(jnp.int32, x.shape, 1)
    even = (lane & shift) == 0
    x_up = pltpu.roll(x, LANES - shift, 1)
    x_dn = pltpu.roll(x, shift, 1)
    return x * cos + jnp.where(even, x_up, x_dn) * sin_signed


def _params(n_axes, ordered=False):
    return pltpu.CompilerParams(dimension_semantics=("arbitrary" if ordered else "parallel",) * n_axes,
                                vmem_limit_bytes=VMEM_LIMIT)


def _mod_kernel(c_ref, w_ref, b_ref, o_ref):
    a = _silu(c_ref[...]).astype(BF16)
    o_ref[0] = _dotf(a, w_ref[0].astype(BF16)) + b_ref[0]


def _modulation(cc, w_mod, b_mod):
    depth, d, n = w_mod.shape
    return pl.pallas_call(
        _mod_kernel,
        grid=(depth, n // MOD_TN),
        in_specs=[pl.BlockSpec((MOD_ROWS, d), lambda l, j: (0, 0)),
                  pl.BlockSpec((1, d, MOD_TN), lambda l, j: (l, 0, j)),
                  pl.BlockSpec((1, 1, MOD_TN), lambda l, j: (l, 0, j))],
        out_specs=pl.BlockSpec((1, MOD_ROWS, MOD_TN), lambda l, j: (l, 0, j)),
        out_shape=jax.ShapeDtypeStruct((depth, MOD_ROWS, n), F32),
        compiler_params=_params(2),
        name="modulation",
    )(cc, w_mod, b_mod.reshape(depth, 1, n))


def _prenorm(x_ref, g_ref, shift_ref, scale_ref):
    y = _rms(x_ref[0], 1.0 / D_MODEL) * g_ref[...]
    return (y * (1.0 + scale_ref[0, 0]) + shift_ref[0, 0]).astype(BF16)


def _mod_spec(layer, row_fn, which):
    return pl.BlockSpec((1, 1, 1, D_MODEL), lambda b, i: (layer, row_fn(b), 0, which))


def _store_vt(vt_ref, head, y):
    chunk = vt_ref.shape[-1]
    for j in range(vt_ref.shape[2]):
        vt_ref[0, head, j, 0:HEAD_DIM, :] = y[j * chunk:(j + 1) * chunk, :].T.astype(BF16)
        vt_ref[0, head, j, HEAD_DIM:DV_AUG, :] = jnp.ones((ONES_ROWS, chunk), BF16)


def _proj_a_body(h, w_ref, cos, sin, q_ref, k_ref, vt_ref, gz_ref):
    scale = LOG2E / math.sqrt(HEAD_DIM)
    yq = _dotf(h, w_ref[:, 0:512])
    for hh in range(4):
        q_ref[0, hh] = (_rope(yq[:, hh * 128:(hh + 1) * 128], cos, sin, 32) * scale).astype(BF16)
    ykv = _dotf(h, w_ref[:, 512:1024])
    for hh in range(2):
        k_ref[0, hh] = _rope(ykv[:, hh * 128:(hh + 1) * 128], cos, sin, 32).astype(BF16)
        _store_vt(vt_ref, hh, ykv[:, 256 + hh * 128:256 + (hh + 1) * 128])
    gz_ref[0] = _silu(_dotf(h, w_ref[:, 1024:1536])).astype(BF16)


def _proj_b_kernel(x_ref, g_ref, shift_ref, scale_ref, w_ref, cos_ref, sin_ref,
                   q_ref, k_ref, vt_ref, gz_ref):
    h = _prenorm(x_ref, g_ref, shift_ref, scale_ref)
    cos, sin = cos_ref[...], sin_ref[...]
    scale = LOG2E / math.sqrt(B_QK_DIM)
    yq = _dotf(h, w_ref[:, 0:512])
    lane = lax.broadcasted_iota(jnp.int32, (h.shape[0], LANES), 1)
    first = lane < B_QK_DIM
    for hh in range(4):
        y = _rope(yq[:, hh * 128:(hh + 1) * 128], cos, sin, 16) * scale
        q_ref[0, 2 * hh] = jnp.where(first, y, 0.0).astype(BF16)
        q_ref[0, 2 * hh + 1] = jnp.where(first, 0.0, y).astype(BF16)
    yk = _dotf(h, w_ref[:, 512:1024])
    for hh in range(4):
        k_ref[0, hh] = _rope(yk[:, hh * 128:(hh + 1) * 128], cos, sin, 16).astype(BF16)
    yv = _dotf(h, w_ref[:, 1024:1536])
    for hh in range(4):
        _store_vt(vt_ref, hh, yv[:, hh * 128:(hh + 1) * 128])
    gz_ref[0] = _silu(_dotf(h, w_ref[:, 1536:2048])).astype(BF16)


def _proj_c_kernel(x_ref, g_ref, shift_ref, scale_ref, w_ref, wuq_ref, wukv_ref, gq_ref, gkv_ref,
                   cos_ref, sin_ref, q_ref, k_ref, vt_ref, gz_ref):
    h = _prenorm(x_ref, g_ref, shift_ref, scale_ref)
    cos, sin = cos_ref[...], sin_ref[...]
    scale = LOG2E / math.sqrt(C_QK_DIM)
    ycq = _dotf(h, w_ref[:, 0:512])
    yckv = _dotf(h, w_ref[:, 512:768])
    gz_ref[0] = _silu(_dotf(h, w_ref[:, 768:1280])).astype(BF16)
    cqn = (_rms(ycq, 1.0 / C_Q_LORA) * gq_ref[...]).astype(BF16)
    yq = _dotf(cqn, wuq_ref[...])
    ckvn = (_rms(yckv[:, 0:128], 1.0 / 128) * gkv_ref[...]).astype(BF16)
    ykv = _dotf(ckvn, wukv_ref[...])
    kr_even = _rope(yckv[:, 128:256], cos, sin, 16)
    kr_odd = pltpu.roll(kr_even, C_ROPE, 1)
    qr = [_rope(yq[:, 512 + p * 128:512 + (p + 1) * 128], cos, sin, 16) * scale for p in range(2)]
    for hh in range(4):
        q_ref[0, hh, :, 0:128] = (yq[:, hh * 128:(hh + 1) * 128] * scale).astype(BF16)
        q_ref[0, hh, :, 128:256] = qr[hh // 2].astype(BF16)
        k_ref[0, hh, :, 0:128] = ykv[:, hh * 256:hh * 256 + 128].astype(BF16)
        k_ref[0, hh, :, 128:256] = (kr_even if hh % 2 == 0 else kr_odd).astype(BF16)
        _store_vt(vt_ref, hh, ykv[:, hh * 256 + 128:(hh + 1) * 256])


def _proj_ad_kernel(x_ref, g_ref, shift_ref, scale_ref, wa_ref, wd_ref, gq_ref, gk_ref, cos_ref, sin_ref,
                    qa_ref, ka_ref, vta_ref, gza_ref, qd_ref, kd_ref, vtd_ref, gzd_ref):
    h = _prenorm(x_ref, g_ref, shift_ref, scale_ref)
    cos, sin = cos_ref[...], sin_ref[...]
    scale = LOG2E / math.sqrt(HEAD_DIM)
    yq = _dotf(h, wd_ref[:, 0:512])
    for hh in range(4):
        yn = _rms(yq[:, hh * 128:(hh + 1) * 128], 1.0 / HEAD_DIM) * gq_ref[...]
        qd_ref[0, hh] = (_rope(yn, cos, sin, 32) * scale).astype(BF16)
    yk = _dotf(h, wd_ref[:, 512:768])
    for hh in range(2):
        yn = _rms(yk[:, hh * 128:(hh + 1) * 128], 1.0 / HEAD_DIM) * gk_ref[...]
        kd_ref[0, hh] = _rope(yn, cos, sin, 32).astype(BF16)
    gzd_ref[0] = _silu(_dotf(h, wd_ref[:, 1024:1536])).astype(BF16)
    _proj_a_body(h, wa_ref, cos, sin, qa_ref, ka_ref, vta_ref, gza_ref)
    yv = _dotf(h, wd_ref[:, 768:1024])
    for hh in range(2):
        _store_vt(vtd_ref, hh, yv[:, hh * 128:(hh + 1) * 128])


def _full_spec(a):
    nd = a.ndim
    return pl.BlockSpec(a.shape, lambda b, i: (0,) * nd, pipeline_mode=pl.Buffered(1))


def _project(kern, name, x, norm, weights, tables, tm, mixers):
    bsz, t, d = x.shape
    nt = t // tm
    chunk = min(tm, KV_CHUNK)
    sub = tm // chunk
    gain, mods, layer, row_fn = norm
    tab_specs = [pl.BlockSpec((tm, LANES), (lambda b, i: (i, 0)) if tab.shape[0] == t
                              else (lambda b, i: (0, 0))) for tab in tables]
    out_specs, out_shape = [], []
    for n_q, dq, n_kv, dk in mixers:
        out_specs += [pl.BlockSpec((1, n_q, tm, dq), lambda b, i: (b, 0, i, 0)),
                      pl.BlockSpec((1, n_kv, tm, dk), lambda b, i: (b, 0, i, 0)),
                      pl.BlockSpec((1, n_kv, sub, DV_AUG, chunk), lambda b, i: (b, 0, i, 0, 0)),
                      pl.BlockSpec((1, tm, GROUP_WIDTH), lambda b, i: (b, i, 0))]
        out_shape += [jax.ShapeDtypeStruct((bsz, n_q, t, dq), BF16),
                      jax.ShapeDtypeStruct((bsz, n_kv, t, dk), BF16),
                      jax.ShapeDtypeStruct((bsz, n_kv, nt * sub, DV_AUG, chunk), BF16),
                      jax.ShapeDtypeStruct((bsz, t, GROUP_WIDTH), BF16)]
    outs = pl.pallas_call(
        kern,
        grid=(bsz, nt),
        in_specs=[pl.BlockSpec((1, tm, d), lambda b, i: (b, i, 0)),
                  pl.BlockSpec((1, d), lambda b, i: (0, 0)),
                  _mod_spec(layer, row_fn, 0), _mod_spec(layer, row_fn, 1)]
        + [_full_spec(w) for w in weights] + tab_specs,
        out_specs=out_specs,
        out_shape=out_shape,
        compiler_params=_params(2),
        name=name,
    )(x, gain, mods, mods, *weights, *tables)
    return [tuple(outs[4 * j:4 * j + 4]) for j in range(len(mixers))]


def _finish_plain(acc, l, gate_ref, out_ref, n_sets, tq):
    o_t = acc * (1.0 / l)
    for g in range(n_sets):
        o = o_t[:, g * tq:(g + 1) * tq].astype(BF16).T
        out_ref[0, :, g * 128:(g + 1) * 128] = o * gate_ref[0, :, g * 128:(g + 1) * 128]


def _finish_diff(acc, l, gate_ref, out_ref, lam_ref, subg_ref, lam_init, tq):
    o_t = acc * (1.0 / l)
    lp = lam_ref[...]
    s1 = jnp.sum(lp[0:1] * lp[1:2], axis=1, keepdims=True)
    s2 = jnp.sum(lp[2:3] * lp[3:4], axis=1, keepdims=True)
    lam = jnp.exp(s1) - jnp.exp(s2) + lam_init
    d = o_t[:, 0:tq] - lam * o_t[:, tq:2 * tq]
    dn = d * lax.rsqrt(jnp.mean(d * d, axis=0, keepdims=True) + EPS)
    o = dn.T * subg_ref[...] * (1.0 - lam_init)
    out_ref[0] = (o * gate_ref[0].astype(F32)).astype(BF16)


def _dense_attn_kernel(*refs, n_sets, tq, n_chunks, has_sink, diff, lam_init):
    refs = list(refs)
    q_ref, kc_ref, vtc_ref = refs[:3]
    pos = 3
    if n_chunks:
        kl_ref, vtl_ref, qn_ref, kcn_ref = refs[pos:pos + 4]
        pos += 4
    gate_ref = refs[pos]
    pos += 1
    if has_sink:
        sink_ref = refs[pos]
        pos += 1
    if diff:
        lam_ref, subg_ref = refs[pos:pos + 2]
        pos += 2
    out_ref, acc_ref = refs[pos:pos + 2]
    s_a, s_b = refs[pos + 2:pos + 4]
    if n_chunks:
        s_ctx_ref, cmax_ctx_ref = refs[pos + 4:pos + 6]

    n = n_sets * tq
    n_tiles = n // COL_TILE
    per_set = tq // COL_TILE
    tc = kc_ref.shape[2]
    acc_ref[0:HEAD_DIM, :] = jnp.zeros((HEAD_DIM, n), F32)
    if has_sink:
        m0 = sink_ref[0] * LOG2E
        acc_ref[HEAD_DIM:DV_AUG, :] = jnp.ones((ONES_ROWS, n), F32)
    else:
        m0 = jnp.full((1, n), NEG_INF, F32)
        acc_ref[HEAD_DIM:DV_AUG, :] = jnp.zeros((ONES_ROWS, n), F32)

    def cols(c):
        return slice(c * COL_TILE, (c + 1) * COL_TILE)

    def scores(k, c, s_ref, rows, queries=q_ref):
        qt = queries[0, c // per_set, (c % per_set) * COL_TILE:(c % per_set + 1) * COL_TILE, :]
        s = _dot_nt(k, qt)
        s_ref[0:rows, cols(c)] = s
        return jnp.max(s, axis=0, keepdims=True)

    def softmax_pv(s_ref, rows, cmax, vt, m, c):
        m_new = jnp.maximum(m, cmax)
        alpha = jnp.exp2(m - m_new)
        p = jnp.exp2(s_ref[0:rows, cols(c)] - m_new)
        pv = _dotf(vt, p.astype(BF16))
        acc_ref[:, cols(c)] = alpha * acc_ref[:, cols(c)] + pv
        return m_new

    def stage(k_next, s_next, s_cur, rows_cur, cmax_cur, vt_cur, m, rows_next=KV_CHUNK, queries=q_ref):
        ms, cm = [], []
        for c in range(n_tiles):
            if k_next is not None:
                cm.append(scores(k_next, c, s_next, rows_next, queries))
            if s_cur is not None:
                ms.append(softmax_pv(s_cur, rows_cur, cmax_cur[:, cols(c)], vt_cur, m[:, cols(c)], c))
        cat = lambda xs: jnp.concatenate(xs, axis=1) if xs else None
        return cat(cm), cat(ms)

    if not n_chunks:
        k_ctx = kc_ref[0, 0]
        cmax = jnp.concatenate([scores(k_ctx, c, s_a, tc) for c in range(n_tiles)], axis=1)
        _, m = stage(None, None, s_a, tc, cmax, vtc_ref[0, 0, 0], m0)
    else:
        def k_chunk(j):
            start = pl.multiple_of(j * KV_CHUNK, KV_CHUNK)
            return kl_ref[0, 0, pl.ds(start, KV_CHUNK), :]

        first_step = (pl.program_id(0) == 0) & (pl.program_id(1) == 0) & (pl.program_id(2) == 0)

        @pl.when(first_step)
        def _():
            k_ctx = kc_ref[0, 0]
            cmax_ctx_ref[0:1, :] = jnp.concatenate(
                [scores(k_ctx, c, s_ctx_ref, tc) for c in range(n_tiles)], axis=1)

        cmax = cmax_ctx_ref[0:1, :]
        cmax, m = stage(k_chunk(0), s_b, s_ctx_ref, tc, cmax, vtc_ref[0, 0, 0], m0)

        def body(i, carry):
            cmax, m = carry
            j = 2 * i
            cmax, m = stage(k_chunk(j + 1), s_a, s_b, KV_CHUNK, cmax, vtl_ref[0, 0, j], m)
            return stage(k_chunk(j + 2), s_b, s_a, KV_CHUNK, cmax, vtl_ref[0, 0, j + 1], m)

        cmax, m = lax.fori_loop(0, n_chunks // 2 - 1, body, (cmax, m))
        j = n_chunks - 2
        cmax, m = stage(k_chunk(j + 1), s_a, s_b, KV_CHUNK, cmax, vtl_ref[0, 0, j], m)
        cmax_next, m = stage(kcn_ref[0, 0], s_ctx_ref, s_a, KV_CHUNK, cmax, vtl_ref[0, 0, j + 1], m,
                             rows_next=tc, queries=qn_ref)
        cmax_ctx_ref[0:1, :] = cmax_next

    acc = acc_ref[0:HEAD_DIM, :]
    l = acc_ref[HEAD_DIM:HEAD_DIM + 1, :]
    if diff:
        _finish_diff(acc, l, gate_ref, out_ref, lam_ref, subg_ref, lam_init, tq)
    else:
        _finish_plain(acc, l, gate_ref, out_ref, n_sets, tq)


def _dense_attention(name, q, kc, vtc, kl, vtl, gate, n_sets, tq, sink=None, diff=None):
    bsz, _, t, dk = q.shape
    hkv, tc = kc.shape[1], kc.shape[2]
    dv = vtc.shape[3]
    n_chunks = 0 if kl is None else kl.shape[2] // KV_CHUNK
    assert n_chunks % 2 == 0 and tq % COL_TILE == 0
    out_w = LANES if diff else n_sets * LANES
    n = n_sets * tq
    args = [q, kc, vtc]
    specs = [pl.BlockSpec((1, n_sets, tq, dk), lambda b, h, i: (b, h, i, 0)),
             pl.BlockSpec((1, 1, tc, dk), lambda b, h, i: (b, h, 0, 0)),
             pl.BlockSpec((1, 1, 1, dv, tc), lambda b, h, i: (b, h, 0, 0, 0))]
    n_i = t // tq
    scratch = [pltpu.VMEM((dv, n), F32)] + [pltpu.VMEM((max(KV_CHUNK, tc), n), F32)] * 2
    if n_chunks:
        def next_step(b, h, i):
            lin = jnp.minimum((b * hkv + h) * n_i + i + 1, bsz * hkv * n_i - 1)
            return lin // (hkv * n_i), (lin // n_i) % hkv, lin % n_i

        def q_next_map(b, h, i):
            nb, nh, ni = next_step(b, h, i)
            return nb, nh, ni, 0

        def kc_next_map(b, h, i):
            nb, nh, _ = next_step(b, h, i)
            return nb, nh, 0, 0

        args += [kl, vtl, q, kc]
        specs += [pl.BlockSpec((1, 1, kl.shape[2], dk), lambda b, h, i: (b, h, 0, 0)),
                  pl.BlockSpec((1, 1, n_chunks, dv, KV_CHUNK), lambda b, h, i: (b, h, 0, 0, 0)),
                  pl.BlockSpec((1, n_sets, tq, dk), q_next_map),
                  pl.BlockSpec((1, 1, tc, dk), kc_next_map)]
        scratch += [pltpu.VMEM((tc, n), F32), pltpu.VMEM((8, n), F32)]
    args.append(gate)
    specs.append(pl.BlockSpec((1, tq, out_w), lambda b, h, i: (b, i, h)))
    if sink is not None:
        args.append(sink)
        specs.append(pl.BlockSpec((1, 1, n), lambda b, h, i: (h, 0, 0)))
    lam_init = 0.0
    if diff is not None:
        lam_par, subg, lam_init = diff
        args += [lam_par, subg]
        specs += [pl.BlockSpec(lam_par.shape, lambda b, h, i: (0, 0)),
                  pl.BlockSpec(subg.shape, lambda b, h, i: (0, 0))]
    kern = functools.partial(_dense_attn_kernel, n_sets=n_sets, tq=tq, n_chunks=n_chunks,
                             has_sink=sink is not None, diff=diff is not None, lam_init=lam_init)
    return pl.pallas_call(
        kern,
        grid=(bsz, hkv, n_i),
        in_specs=specs,
        out_specs=pl.BlockSpec((1, tq, out_w), lambda b, h, i: (b, i, h)),
        out_shape=jax.ShapeDtypeStruct((bsz, t, GROUP_WIDTH), BF16),
        scratch_shapes=scratch,
        compiler_params=_params(3, ordered=n_chunks > 0),
        name=name,
    )(*args)


def _band_attn_kernel(q_ref, kc_ref, vtc_ref, kl_ref, vtl_ref, gate_ref, sink_ref, out_ref, *bufs,
                      blocks, n_chunks):
    w = WINDOW
    per_chunk = KV_CHUNK // w
    tc = kc_ref.shape[2]
    i = pl.program_id(2)
    first_step = i == 0
    last_step = i == pl.num_programs(2) - 1
    r1 = lax.broadcasted_iota(jnp.int32, (w, 1), 0)
    qc = lax.broadcasted_iota(jnp.int32, (1, 2 * w), 1) & (w - 1)
    prev_ok = r1 >= qc
    next_ok = r1 <= qc
    sink = sink_ref[0] * LOG2E
    k_ctx = kc_ref[0, 0]

    def pieces(t):
        chunk = i * (blocks // per_chunk) + t // per_chunk
        tt = t % per_chunk
        if tt < per_chunk - 1:
            pair = (chunk * KV_CHUNK + tt * w, chunk, tt * w, [(w, next_ok)], None)
            if tt > 0:
                single = (chunk * KV_CHUNK + (tt - 1) * w, chunk, (tt - 1) * w, [(0, prev_ok)], None)
            else:
                prev_chunk = jnp.maximum(chunk - 1, 0)
                bias = jnp.where(first_step, NEG_INF, 0.0) if t == 0 else None
                single = (prev_chunk * KV_CHUNK + (per_chunk - 1) * w, prev_chunk, (per_chunk - 1) * w,
                          [(0, prev_ok)], bias)
        else:
            pair = (chunk * KV_CHUNK + (tt - 1) * w, chunk, (tt - 1) * w, [(0, prev_ok)], None)
            next_chunk = jnp.minimum(chunk + 1, n_chunks - 1)
            bias = jnp.where(last_step, NEG_INF, 0.0) if t == blocks - 1 else None
            single = (next_chunk * KV_CHUNK, next_chunk, 0, [(0, next_ok)], bias)
        return pair, single

    def q_block(t):
        rows = slice(t * w, (t + 1) * w)
        return jnp.concatenate([q_ref[0, 0, rows, :], q_ref[0, 1, rows, :]], axis=0)

    def scores(t, s_ref):
        qt = q_block(t)
        s_ctx = _dot_nt(k_ctx, qt)
        s_ref[0:tc, :] = s_ctx
        cmax = jnp.max(s_ctx, axis=0, keepdims=True)
        row0 = tc
        for (start, _, _, masked, bias), rows in zip(pieces(t), (2 * w, w)):
            k = kl_ref[0, 0, pl.ds(pl.multiple_of(start, w), rows), :]
            s = _dot_nt(k, qt)
            if bias is not None:
                s = s + bias
            parts = []
            for r0 in range(0, rows, w):
                part = s[r0:r0 + w]
                for off, mask in masked:
                    if off == r0:
                        part = jnp.where(mask, part, NEG_INF)
                parts.append(part)
                cmax = jnp.maximum(cmax, jnp.max(part, axis=0, keepdims=True))
            for j, part in enumerate(parts):
                s_ref[row0 + j * w:row0 + (j + 1) * w, :] = part
            row0 += rows
        return cmax

    def softmax_pv(t, s_ref, cmax):
        m = jnp.maximum(sink, cmax)
        acc = _dotf(vtc_ref[0, 0, 0], jnp.exp2(s_ref[0:tc, :] - m).astype(BF16))
        row0 = tc
        for (_, chunk, lane0, _, _), rows in zip(pieces(t), (2 * w, w)):
            vt = vtl_ref[0, 0, chunk, :, lane0:lane0 + rows]
            acc = acc + _dotf(vt, jnp.exp2(s_ref[row0:row0 + rows, :] - m).astype(BF16))
            row0 += rows
        l = acc[HEAD_DIM:HEAD_DIM + 1, :] + jnp.exp2(sink - m)
        o_t = acc[0:HEAD_DIM, :] * (1.0 / l)
        rows = slice(t * w, (t + 1) * w)
        for g in range(2):
            gcols = slice(g * LANES, (g + 1) * LANES)
            o = o_t[:, g * w:(g + 1) * w].T
            out_ref[0, rows, gcols] = (o * gate_ref[0, rows, gcols].astype(F32)).astype(BF16)

    n_buf = len(bufs)
    ahead = n_buf - 1
    cmaxes = [scores(t, bufs[t % n_buf]) for t in range(min(ahead, blocks))]
    for t in range(blocks):
        if t + ahead < blocks:
            cmaxes.append(scores(t + ahead, bufs[(t + ahead) % n_buf]))
        softmax_pv(t, bufs[t % n_buf], cmaxes[t])


def _band_attention(q, kc, vtc, kl, vtl, gate, sink):
    bsz, _, s, dk = q.shape
    tq = min(TQ_BAND, s)
    tc = kc.shape[2]
    n_chunks = s // KV_CHUNK
    assert tq % KV_CHUNK == 0 and dk == HEAD_DIM
    return pl.pallas_call(
        functools.partial(_band_attn_kernel, blocks=tq // WINDOW, n_chunks=n_chunks),
        grid=(bsz, 2, s // tq),
        in_specs=[
            pl.BlockSpec((1, 2, tq, dk), lambda b, h, i: (b, h, i, 0)),
            pl.BlockSpec((1, 1, tc, dk), lambda b, h, i: (b, h, 0, 0)),
            pl.BlockSpec((1, 1, 1, DV_AUG, tc), lambda b, h, i: (b, h, 0, 0, 0)),
            pl.BlockSpec((1, 1, s, dk), lambda b, h, i: (b, h, 0, 0)),
            pl.BlockSpec((1, 1, n_chunks, DV_AUG, KV_CHUNK), lambda b, h, i: (b, h, 0, 0, 0)),
            pl.BlockSpec((1, tq, 2 * LANES), lambda b, h, i: (b, i, h)),
            pl.BlockSpec((1, 1, 2 * WINDOW), lambda b, h, i: (h, 0, 0)),
        ],
        out_specs=pl.BlockSpec((1, tq, 2 * LANES), lambda b, h, i: (b, i, h)),
        out_shape=jax.ShapeDtypeStruct((bsz, s, GROUP_WIDTH), BF16),
        scratch_shapes=[pltpu.VMEM((tc + 3 * WINDOW, 2 * WINDOW), F32)] * (BAND_AHEAD + 1),
        compiler_params=_params(3),
        name="attn_a_band",
    )(q, kc, vtc, kl, vtl, gate, sink)


def _out_kernel(*refs, final):
    ua, ub, uc, ud, w_ref, x_ref, gate_ref = refs[:7]
    out_ref = refs[-1]
    u = jnp.concatenate([ua[0], ub[0], uc[0], ud[0]], axis=1)
    acc = _dotf(u, w_ref[...].reshape(4 * GROUP_WIDTH, D_MODEL))
    xn = x_ref[0] + gate_ref[0, 0] * acc
    if final:
        xn = _rms(xn, 1.0 / D_MODEL) * refs[7][...]
    out_ref[0] = xn


def _out_project(us, w_out, x, mods, layer, row_fn, tm, final_g=None):
    bsz, t, d = x.shape
    u_spec = pl.BlockSpec((1, tm, GROUP_WIDTH), lambda b, i: (b, i, 0))
    specs = [u_spec] * 4 + [_full_spec(w_out), pl.BlockSpec((1, tm, d), lambda b, i: (b, i, 0)),
                            _mod_spec(layer, row_fn, 2)]
    args = list(us) + [w_out, x, mods]
    if final_g is not None:
        specs.append(pl.BlockSpec((1, d), lambda b, i: (0, 0)))
        args.append(final_g)
    return pl.pallas_call(
        functools.partial(_out_kernel, final=final_g is not None),
        grid=(bsz, t // tm),
        in_specs=specs,
        out_specs=pl.BlockSpec((1, tm, d), lambda b, i: (b, i, 0)),
        out_shape=jax.ShapeDtypeStruct((bsz, t, d), F32),
        compiler_params=_params(2),
        name="out_proj",
    )(*args)


def _rope_tables(rows, rot_dim):
    row = jnp.broadcast_to(jnp.arange(rows)[:, None], (rows, GRID_W)).reshape(-1).astype(F32)
    col = jnp.broadcast_to(jnp.arange(GRID_W)[None, :], (rows, GRID_W)).reshape(-1).astype(F32)
    axis_dim = rot_dim // 2
    inv_freq = ROPE_THETA ** (-jnp.arange(0, axis_dim, 2, dtype=F32) / axis_dim)
    ang_r = row[:, None] * inv_freq[None, :]
    ang_c = col[:, None] * inv_freq[None, :]
    ang = jnp.concatenate([ang_r, ang_r, ang_c, ang_c], axis=-1)
    quarter = rot_dim // 4
    sign = jnp.where((jnp.arange(rot_dim) // quarter) % 2 == 0, -1.0, 1.0).astype(F32)
    cos, sin = jnp.cos(ang), jnp.sin(ang) * sign
    reps = LANES // rot_dim
    return jnp.tile(cos, (1, reps)), jnp.tile(sin, (1, reps))


def _layer_weights(w_in, w_uq, w_ukv, gq, gkv):
    wa = w_in[:, 0:1536]
    wb = w_in[:, 1536:3584]
    zeros64 = jnp.zeros((w_in.shape[0], 64), w_in.dtype)
    wc = jnp.concatenate([w_in[:, 3584:4032], zeros64,
                          w_in[:, 4032:4160],
                          w_in[:, 4160:4224], zeros64,
                          w_in[:, 4224:4736]], axis=1)
    wd = w_in[:, 4736:6272]
    uq = w_uq.reshape(C_Q_LORA, 4, C_QK_DIM)
    uq = jnp.concatenate([uq[:, :, :128].reshape(C_Q_LORA, 512),
                          uq[:, :, 128:].reshape(C_Q_LORA, 256)], axis=1)
    uq = jnp.concatenate([uq, jnp.zeros((64, 768), uq.dtype)], axis=0)
    gq_pad = jnp.concatenate([gq, jnp.zeros((64,), gq.dtype)])[None, :]
    cast = lambda a: a.astype(BF16)
    return cast(wa), cast(wb), cast(wc), cast(wd), cast(uq), cast(w_ukv), gq_pad, gkv[None, :]


def kernel(x, c, ctx, c_ctx, w_mod, b_mod, norm_g, w_in, c_q_norm_g, c_kv_norm_g, c_w_uq, c_w_ukv,
           d_q_norm_g, d_k_norm_g, a_sink, b_lambda, b_subln_g, w_out, final_norm_g):
    bsz, s, d = x.shape
    tc = ctx.shape[1]
    depth = w_mod.shape[0]
    assert d == D_MODEL and s % TM_LATENT == 0 and s % GRID_W == 0 and bsz + 1 <= MOD_ROWS
    assert tc % LANES == 0 and TM_LATENT % KV_CHUNK == 0 and tc <= KV_CHUNK

    cc = jnp.concatenate([c, c_ctx[None, :], jnp.zeros((MOD_ROWS - bsz - 1, d), c.dtype)], axis=0)
    mods = _modulation(cc, w_mod, b_mod).reshape(depth, MOD_ROWS, 1, 3 * d)
    lat_row = lambda b: b
    ctx_row = lambda b: bsz

    rows = s // GRID_W
    cos_h, sin_h = _rope_tables(rows, HEAD_DIM)
    cos_b, sin_b = _rope_tables(rows, B_QK_DIM)
    one = jnp.ones((tc, LANES), F32)
    zero = jnp.zeros((tc, LANES), F32)
    tab_lat = {"h": (cos_h, sin_h), "b": (cos_b, sin_b)}
    tab_ctx = {"h": (one, zero), "b": (one, zero)}

    tq2, tq1 = min(N_DENSE // 2, s), min(N_DENSE, s)
    for l in range(depth):
        last = l == depth - 1
        wa, wb, wc, wd, uq, ukv, gq_pad, gkv = _layer_weights(
            w_in[l], c_w_uq[l], c_w_ukv[l], c_q_norm_g[l], c_kv_norm_g[l])
        dq_g, dk_g = d_q_norm_g[l][None, :], d_k_norm_g[l][None, :]
        wo = w_out[l].astype(BF16).reshape(4, GROUP_WIDTH, d)
        lam_init = 0.8 - 0.6 * math.exp(-0.3 * l)
        diff = (b_lambda[l], b_subln_g[l][None, :], lam_init)
        sink2 = a_sink[l].reshape(2, 2)

        def project(stream, row_fn, tabs, tm):
            norm = (norm_g[l][None, :], mods, l, row_fn)
            pa, pd = _project(_proj_ad_kernel, "proj_ad", stream, norm, [wa, wd, dq_g, dk_g], tabs["h"],
                              tm, [(4, 128, 2, 128), (4, 128, 2, 128)])
            pb, = _project(_proj_b_kernel, "proj_b", stream, norm, [wb], tabs["b"], tm, [(8, 128, 4, 128)])
            pc, = _project(_proj_c_kernel, "proj_c", stream, norm, [wc, uq, ukv, gq_pad, gkv], tabs["b"],
                           tm, [(4, 256, 4, 256)])
            return pa, pb, pc, pd

        def sink_rows(tq):
            return jnp.repeat(sink2, tq, axis=1).reshape(2, 1, 2 * tq)

        (qa, ka, vta, gza), (qb, kb, vtb, gzb), (qc, kc, vtc, gzc), (qd, kd, vtd, gzd) = project(
            x, lat_row, tab_lat, TM_LATENT)
        (qa_c, ka_c, vta_c, gza_c), (qb_c, kb_c, vtb_c, gzb_c), (qc_c, kc_c, vtc_c, gzc_c), \
            (qd_c, kd_c, vtd_c, gzd_c) = project(ctx, ctx_row, tab_ctx, tc)

        ua = _band_attention(qa, ka_c, vta_c, ka, vta, gza, sink_rows(WINDOW))
        ub = _dense_attention("attn_b", qb, kb_c, vtb_c, kb, vtb, gzb, 2, tq2, diff=diff)
        uc = _dense_attention("attn_c", qc, kc_c, vtc_c, kc, vtc, gzc, 1, tq1)
        ud = _dense_attention("attn_d", qd, kd_c, vtd_c, kd, vtd, gzd, 2, tq2)
        if not last:
            ua_c = _dense_attention("attn_a_ctx", qa_c, ka_c, vta_c, None, None, gza_c, 2, tc,
                                    sink=sink_rows(tc))
            ub_c = _dense_attention("attn_b_ctx", qb_c, kb_c, vtb_c, None, None, gzb_c, 2, tc, diff=diff)
            uc_c = _dense_attention("attn_c_ctx", qc_c, kc_c, vtc_c, None, None, gzc_c, 1, tc)
            ud_c = _dense_attention("attn_d_ctx", qd_c, kd_c, vtd_c, None, None, gzd_c, 2, tc)
            ctx = _out_project((ua_c, ub_c, uc_c, ud_c), wo, ctx, mods, l, ctx_row, tc)
        x = _out_project((ua, ub, uc, ud), wo, x, mods, l, lat_row, TM_OUT,
                         final_g=final_norm_g[None, :] if last else None)
    return x
```

```python
import functools
import math

import jax
import jax.numpy as jnp
from jax import lax
from jax.experimental import pallas as pl
from jax.experimental.pallas import tpu as pltpu

F32 = jnp.float32
BF16 = jnp.bfloat16

D_MODEL = 2048
HEAD_DIM = 128
GRID_W = 64
GROUP_WIDTH = 512
WINDOW = 128
ROPE_THETA = 10000.0
EPS = 1e-6
NEG_INF = -1e30
C_Q_LORA = 448
C_ROPE = 64
C_QK_DIM = 192
B_QK_DIM = 64
LOG2E = math.log2(math.e)
LANES = 128
ONES_ROWS = 16
DV_AUG = HEAD_DIM + ONES_ROWS
MOD_ROWS = 16
VMEM_LIMIT = 56 * 1024 * 1024

TM_LATENT = 1024
TM_OUT = 512
KV_CHUNK = 512
N_DENSE = 4096
TQ_BAND = 1024
BAND_AHEAD = 3
MOD_TN = 512
COL_TILE = 256


def _dotf(a, b):
    return jnp.dot(a, b, preferred_element_type=F32)


def _dot_nt(a, b):
    return lax.dot_general(a, b, (((1,), (1,)), ((), ())), preferred_element_type=F32)


def _silu(z):
    return z / (1.0 + jnp.exp(-z))


def _rms(y, inv_n):
    return y * lax.rsqrt(jnp.sum(y * y, axis=-1, keepdims=True) * inv_n + EPS)


def _rope(x, cos, sin_signed, shift):
    lane = lax.broadcasted_iota(jnp.int32, x.shape, 1)
    even = (lane & shift) == 0
    x_up = pltpu.roll(x, LANES - shift, 1)
    x_dn = pltpu.roll(x, shift, 1)
    return x * cos + jnp.where(even, x_up, x_dn) * sin_signed


def _params(n_axes):
    return pltpu.CompilerParams(dimension_semantics=("parallel",) * n_axes,
                                vmem_limit_bytes=VMEM_LIMIT)


def _mod_kernel(c_ref, w_ref, b_ref, o_ref):
    a = _silu(c_ref[...]).astype(BF16)
    o_ref[0] = _dotf(a, w_ref[0].astype(BF16)) + b_ref[0]


def _modulation(cc, w_mod, b_mod):
    depth, d, n = w_mod.shape
    return pl.pallas_call(
        _mod_kernel,
        grid=(depth, n // MOD_TN),
        in_specs=[pl.BlockSpec((MOD_ROWS, d), lambda l, j: (0, 0)),
                  pl.BlockSpec((1, d, MOD_TN), lambda l, j: (l, 0, j)),
                  pl.BlockSpec((1, 1, MOD_TN), lambda l, j: (l, 0, j))],
        out_specs=pl.BlockSpec((1, MOD_ROWS, MOD_TN), lambda l, j: (l, 0, j)),
        out_shape=jax.ShapeDtypeStruct((depth, MOD_ROWS, n), F32),
        compiler_params=_params(2),
        name="modulation",
    )(cc, w_mod, b_mod.reshape(depth, 1, n))


def _prenorm(x_ref, g_ref, shift_ref, scale_ref):
    y = _rms(x_ref[0], 1.0 / D_MODEL) * g_ref[...]
    return (y * (1.0 + scale_ref[0, 0]) + shift_ref[0, 0]).astype(BF16)


def _mod_spec(layer, row_fn, which):
    return pl.BlockSpec((1, 1, 1, D_MODEL), lambda b, i: (layer, row_fn(b), 0, which))


def _store_vt(vt_ref, head, y):
    chunk = vt_ref.shape[-1]
    for j in range(vt_ref.shape[2]):
        vt_ref[0, head, j, 0:HEAD_DIM, :] = y[j * chunk:(j + 1) * chunk, :].T.astype(BF16)
        vt_ref[0, head, j, HEAD_DIM:DV_AUG, :] = jnp.ones((ONES_ROWS, chunk), BF16)


def _proj_a_body(h, w_ref, cos, sin, q_ref, k_ref, vt_ref, gz_ref):
    scale = LOG2E / math.sqrt(HEAD_DIM)
    yq = _dotf(h, w_ref[:, 0:512])
    for hh in range(4):
        q_ref[0, hh] = (_rope(yq[:, hh * 128:(hh + 1) * 128], cos, sin, 32) * scale).astype(BF16)
    ykv = _dotf(h, w_ref[:, 512:1024])
    for hh in range(2):
        k_ref[0, hh] = _rope(ykv[:, hh * 128:(hh + 1) * 128], cos, sin, 32).astype(BF16)
        _store_vt(vt_ref, hh, ykv[:, 256 + hh * 128:256 + (hh + 1) * 128])
    gz_ref[0] = _silu(_dotf(h, w_ref[:, 1024:1536])).astype(BF16)


def _proj_b_kernel(x_ref, g_ref, shift_ref, scale_ref, w_ref, cos_ref, sin_ref,
                   q_ref, k_ref, vt_ref, gz_ref):
    h = _prenorm(x_ref, g_ref, shift_ref, scale_ref)
    cos, sin = cos_ref[...], sin_ref[...]
    scale = LOG2E / math.sqrt(B_QK_DIM)
    yq = _dotf(h, w_ref[:, 0:512])
    lane = lax.broadcasted_iota(jnp.int32, (h.shape[0], LANES), 1)
    first = lane < B_QK_DIM
    for hh in range(4):
        y = _rope(yq[:, hh * 128:(hh + 1) * 128], cos, sin, 16) * scale
        q_ref[0, 2 * hh] = jnp.where(first, y, 0.0).astype(BF16)
        q_ref[0, 2 * hh + 1] = jnp.where(first, 0.0, y).astype(BF16)
    yk = _dotf(h, w_ref[:, 512:1024])
    for hh in range(4):
        k_ref[0, hh] = _rope(yk[:, hh * 128:(hh + 1) * 128], cos, sin, 16).astype(BF16)
    yv = _dotf(h, w_ref[:, 1024:1536])
    for hh in range(4):
        _store_vt(vt_ref, hh, yv[:, hh * 128:(hh + 1) * 128])
    gz_ref[0] = _silu(_dotf(h, w_ref[:, 1536:2048])).astype(BF16)


def _proj_c_kernel(x_ref, g_ref, shift_ref, scale_ref, w_ref, wuq_ref, wukv_ref, gq_ref, gkv_ref,
                   cos_ref, sin_ref, q_ref, k_ref, vt_ref, gz_ref):
    h = _prenorm(x_ref, g_ref, shift_ref, scale_ref)
    cos, sin = cos_ref[...], sin_ref[...]
    scale = LOG2E / math.sqrt(C_QK_DIM)
    ycq = _dotf(h, w_ref[:, 0:512])
    yckv = _dotf(h, w_ref[:, 512:768])
    gz_ref[0] = _silu(_dotf(h, w_ref[:, 768:1280])).astype(BF16)
    cqn = (_rms(ycq, 1.0 / C_Q_LORA) * gq_ref[...]).astype(BF16)
    yq = _dotf(cqn, wuq_ref[...])
    ckvn = (_rms(yckv[:, 0:128], 1.0 / 128) * gkv_ref[...]).astype(BF16)
    ykv = _dotf(ckvn, wukv_ref[...])
    kr_even = _rope(yckv[:, 128:256], cos, sin, 16)
    kr_odd = pltpu.roll(kr_even, C_ROPE, 1)
    qr = [_rope(yq[:, 512 + p * 128:512 + (p + 1) * 128], cos, sin, 16) * scale for p in range(2)]
    for hh in range(4):
        q_ref[0, hh, :, 0:128] = (yq[:, hh * 128:(hh + 1) * 128] * scale).astype(BF16)
        q_ref[0, hh, :, 128:256] = qr[hh // 2].astype(BF16)
        k_ref[0, hh, :, 0:128] = ykv[:, hh * 256:hh * 256 + 128].astype(BF16)
        k_ref[0, hh, :, 128:256] = (kr_even if hh % 2 == 0 else kr_odd).astype(BF16)
        _store_vt(vt_ref, hh, ykv[:, hh * 256 + 128:(hh + 1) * 256])


def _proj_ad_kernel(x_ref, g_ref, shift_ref, scale_ref, wa_ref, wd_ref, gq_ref, gk_ref, cos_ref, sin_ref,
                    qa_ref, ka_ref, vta_ref, gza_ref, qd_ref, kd_ref, vtd_ref, gzd_ref):
    h = _prenorm(x_ref, g_ref, shift_ref, scale_ref)
    cos, sin = cos_ref[...], sin_ref[...]
    scale = LOG2E / math.sqrt(HEAD_DIM)
    yq = _dotf(h, wd_ref[:, 0:512])
    for hh in range(4):
        yn = _rms(yq[:, hh * 128:(hh + 1) * 128], 1.0 / HEAD_DIM) * gq_ref[...]
        qd_ref[0, hh] = (_rope(yn, cos, sin, 32) * scale).astype(BF16)
    yk = _dotf(h, wd_ref[:, 512:768])
    for hh in range(2):
        yn = _rms(yk[:, hh * 128:(hh + 1) * 128], 1.0 / HEAD_DIM) * gk_ref[...]
        kd_ref[0, hh] = _rope(yn, cos, sin, 32).astype(BF16)
    gzd_ref[0] = _silu(_dotf(h, wd_ref[:, 1024:1536])).astype(BF16)
    _proj_a_body(h, wa_ref, cos, sin, qa_ref, ka_ref, vta_ref, gza_ref)
    yv = _dotf(h, wd_ref[:, 768:1024])
    for hh in range(2):
        _store_vt(vtd_ref, hh, yv[:, hh * 128:(hh + 1) * 128])


def _full_spec(a):
    nd = a.ndim
    return pl.BlockSpec(a.shape, lambda b, i: (0,) * nd, pipeline_mode=pl.Buffered(1))


def _project(kern, name, x, norm, weights, tables, tm, mixers):
    bsz, t, d = x.shape
    nt = t // tm
    chunk = min(tm, KV_CHUNK)
    sub = tm // chunk
    gain, mods, layer, row_fn = norm
    tab_specs = [pl.BlockSpec((tm, LANES), (lambda b, i: (i, 0)) if tab.shape[0] == t
                              else (lambda b, i: (0, 0))) for tab in tables]
    out_specs, out_shape = [], []
    for n_q, dq, n_kv, dk in mixers:
        out_specs += [pl.BlockSpec((1, n_q, tm, dq), lambda b, i: (b, 0, i, 0)),
                      pl.BlockSpec((1, n_kv, tm, dk), lambda b, i: (b, 0, i, 0)),
                      pl.BlockSpec((1, n_kv, sub, DV_AUG, chunk), lambda b, i: (b, 0, i, 0, 0)),
                      pl.BlockSpec((1, tm, GROUP_WIDTH), lambda b, i: (b, i, 0))]
        out_shape += [jax.ShapeDtypeStruct((bsz, n_q, t, dq), BF16),
                      jax.ShapeDtypeStruct((bsz, n_kv, t, dk), BF16),
                      jax.ShapeDtypeStruct((bsz, n_kv, nt * sub, DV_AUG, chunk), BF16),
                      jax.ShapeDtypeStruct((bsz, t, GROUP_WIDTH), BF16)]
    outs = pl.pallas_call(
        kern,
        grid=(bsz, nt),
        in_specs=[pl.BlockSpec((1, tm, d), lambda b, i: (b, i, 0)),
                  pl.BlockSpec((1, d), lambda b, i: (0, 0)),
                  _mod_spec(layer, row_fn, 0), _mod_spec(layer, row_fn, 1)]
        + [_full_spec(w) for w in weights] + tab_specs,
        out_specs=out_specs,
        out_shape=out_shape,
        compiler_params=_params(2),
        name=name,
    )(x, gain, mods, mods, *weights, *tables)
    return [tuple(outs[4 * j:4 * j + 4]) for j in range(len(mixers))]


def _finish_plain(acc, l, gate_ref, out_ref, n_sets, tq):
    o_t = acc * (1.0 / l)
    for g in range(n_sets):
        o = o_t[:, g * tq:(g + 1) * tq].astype(BF16).T
        out_ref[0, :, g * 128:(g + 1) * 128] = o


def _finish_diff(acc, l, gate_ref, out_ref, lam_ref, subg_ref, lam_init, tq):
    o_t = acc * (1.0 / l)
    lp = lam_ref[...]
    s1 = jnp.sum(lp[0:1] * lp[1:2], axis=1, keepdims=True)
    s2 = jnp.sum(lp[2:3] * lp[3:4], axis=1, keepdims=True)
    lam = jnp.exp(s1) - jnp.exp(s2) + lam_init
    d = o_t[:, 0:tq] - lam * o_t[:, tq:2 * tq]
    dn = d * lax.rsqrt(jnp.mean(d * d, axis=0, keepdims=True) + EPS)
    o = dn.T * subg_ref[...] * (1.0 - lam_init)
    out_ref[0] = o.astype(BF16)


def _dense_attn_kernel(*refs, n_sets, tq, n_chunks, has_sink, diff, lam_init):
    refs = list(refs)
    q_ref, kc_ref, vtc_ref = refs[:3]
    pos = 3
    if n_chunks:
        kl_ref, vtl_ref = refs[pos:pos + 2]
        pos += 2
    gate_ref = None
    if has_sink:
        sink_ref = refs[pos]
        pos += 1
    if diff:
        lam_ref, subg_ref = refs[pos:pos + 2]
        pos += 2
    out_ref, acc_ref = refs[pos:pos + 2]
    s_refs = refs[pos + 2:pos + 4]

    n = n_sets * tq
    n_tiles = n // COL_TILE
    per_set = tq // COL_TILE
    tc = kc_ref.shape[2]
    acc_ref[0:HEAD_DIM, :] = jnp.zeros((HEAD_DIM, n), F32)
    if has_sink:
        m0 = sink_ref[0] * LOG2E
        acc_ref[HEAD_DIM:DV_AUG, :] = jnp.ones((ONES_ROWS, n), F32)
    else:
        m0 = jnp.full((1, n), NEG_INF, F32)
        acc_ref[HEAD_DIM:DV_AUG, :] = jnp.zeros((ONES_ROWS, n), F32)

    def cols(c):
        return slice(c * COL_TILE, (c + 1) * COL_TILE)

    def scores(k, c, s_ref, rows):
        qt = q_ref[0, c // per_set, (c % per_set) * COL_TILE:(c % per_set + 1) * COL_TILE, :]
        s = _dot_nt(k, qt)
        s_ref[0:rows, cols(c)] = s
        return jnp.max(s, axis=0, keepdims=True)

    def softmax_pv(s_ref, rows, cmax, vt, m, c):
        m_new = jnp.maximum(m, cmax)
        alpha = jnp.exp2(m - m_new)
        p = jnp.exp2(s_ref[0:rows, cols(c)] - m_new)
        pv = _dotf(vt, p.astype(BF16))
        acc_ref[:, cols(c)] = alpha * acc_ref[:, cols(c)] + pv
        return m_new

    def stage(k_next, s_next, s_cur, rows_cur, cmax_cur, vt_cur, m):
        ms, cm = [], []
        for c in range(n_tiles):
            if k_next is not None:
                cm.append(scores(k_next, c, s_next, KV_CHUNK))
            if s_cur is not None:
                ms.append(softmax_pv(s_cur, rows_cur, cmax_cur[:, cols(c)], vt_cur, m[:, cols(c)], c))
        cat = lambda xs: jnp.concatenate(xs, axis=1) if xs else None
        return cat(cm), cat(ms)

    s_a, s_b = s_refs
    k_ctx = kc_ref[0, 0]
    cmax = jnp.concatenate([scores(k_ctx, c, s_a, tc) for c in range(n_tiles)], axis=1)
    if not n_chunks:
        _, m = stage(None, None, s_a, tc, cmax, vtc_ref[0, 0, 0], m0)
    else:
        def k_chunk(j):
            start = pl.multiple_of(j * KV_CHUNK, KV_CHUNK)
            return kl_ref[0, 0, pl.ds(start, KV_CHUNK), :]

        cmax, m = stage(k_chunk(0), s_b, s_a, tc, cmax, vtc_ref[0, 0, 0], m0)

        def body(i, carry):
            cmax, m = carry
            j = 2 * i
            cmax, m = stage(k_chunk(j + 1), s_a, s_b, KV_CHUNK, cmax, vtl_ref[0, 0, j], m)
            return stage(k_chunk(j + 2), s_b, s_a, KV_CHUNK, cmax, vtl_ref[0, 0, j + 1], m)

        cmax, m = lax.fori_loop(0, n_chunks // 2 - 1, body, (cmax, m))
        j = n_chunks - 2
        cmax, m = stage(k_chunk(j + 1), s_a, s_b, KV_CHUNK, cmax, vtl_ref[0, 0, j], m)
        _, m = stage(None, None, s_a, KV_CHUNK, cmax, vtl_ref[0, 0, j + 1], m)

    acc = acc_ref[0:HEAD_DIM, :]
    l = acc_ref[HEAD_DIM:HEAD_DIM + 1, :]
    if diff:
        _finish_diff(acc, l, gate_ref, out_ref, lam_ref, subg_ref, lam_init, tq)
    else:
        _finish_plain(acc, l, gate_ref, out_ref, n_sets, tq)


def _dense_attention(name, q, kc, vtc, kl, vtl, gate, n_sets, tq, sink=None, diff=None):
    bsz, _, t, dk = q.shape
    hkv, tc = kc.shape[1], kc.shape[2]
    dv = vtc.shape[3]
    n_chunks = 0 if kl is None else kl.shape[2] // KV_CHUNK
    assert n_chunks % 2 == 0 and tq % COL_TILE == 0
    out_w = LANES if diff else n_sets * LANES
    n = n_sets * tq
    args = [q, kc, vtc]
    specs = [pl.BlockSpec((1, n_sets, tq, dk), lambda b, h, i: (b, h, i, 0)),
             pl.BlockSpec((1, 1, tc, dk), lambda b, h, i: (b, h, 0, 0)),
             pl.BlockSpec((1, 1, 1, dv, tc), lambda b, h, i: (b, h, 0, 0, 0))]
    if n_chunks:
        args += [kl, vtl]
        specs += [pl.BlockSpec((1, 1, kl.shape[2], dk), lambda b, h, i: (b, h, 0, 0)),
                  pl.BlockSpec((1, 1, n_chunks, dv, KV_CHUNK), lambda b, h, i: (b, h, 0, 0, 0))]
    if sink is not None:
        args.append(sink)
        specs.append(pl.BlockSpec((1, 1, n), lambda b, h, i: (h, 0, 0)))
    lam_init = 0.0
    if diff is not None:
        lam_par, subg, lam_init = diff
        args += [lam_par, subg]
        specs += [pl.BlockSpec(lam_par.shape, lambda b, h, i: (0, 0)),
                  pl.BlockSpec(subg.shape, lambda b, h, i: (0, 0))]
    kern = functools.partial(_dense_attn_kernel, n_sets=n_sets, tq=tq, n_chunks=n_chunks,
                             has_sink=sink is not None, diff=diff is not None, lam_init=lam_init)
    return pl.pallas_call(
        kern,
        grid=(bsz, hkv, t // tq),
        in_specs=specs,
        out_specs=pl.BlockSpec((1, tq, out_w), lambda b, h, i: (b, i, h)),
        out_shape=jax.ShapeDtypeStruct((bsz, t, GROUP_WIDTH), BF16),
        scratch_shapes=[pltpu.VMEM((dv, n), F32)] + [pltpu.VMEM((max(KV_CHUNK, tc), n), F32)] * 2,
        compiler_params=_params(3),
        name=name,
    )(*args)


def _band_attn_kernel(q_ref, kc_ref, vtc_ref, kl_ref, vtl_ref, sink_ref, out_ref, *bufs,
                      blocks, n_chunks):
    w = WINDOW
    per_chunk = KV_CHUNK // w
    tc = kc_ref.shape[2]
    i = pl.program_id(2)
    first_step = i == 0
    last_step = i == pl.num_programs(2) - 1
    r1 = lax.broadcasted_iota(jnp.int32, (w, 1), 0)
    qc = lax.broadcasted_iota(jnp.int32, (1, 2 * w), 1) & (w - 1)
    prev_ok = r1 >= qc
    next_ok = r1 <= qc
    sink = sink_ref[0] * LOG2E
    k_ctx = kc_ref[0, 0]

    def pieces(t):
        chunk = i * (blocks // per_chunk) + t // per_chunk
        tt = t % per_chunk
        if tt < per_chunk - 1:
            pair = (chunk * KV_CHUNK + tt * w, chunk, tt * w, [(w, next_ok)], None)
            if tt > 0:
                single = (chunk * KV_CHUNK + (tt - 1) * w, chunk, (tt - 1) * w, [(0, prev_ok)], None)
            else:
                prev_chunk = jnp.maximum(chunk - 1, 0)
                bias = jnp.where(first_step, NEG_INF, 0.0) if t == 0 else None
                single = (prev_chunk * KV_CHUNK + (per_chunk - 1) * w, prev_chunk, (per_chunk - 1) * w,
                          [(0, prev_ok)], bias)
        else:
            pair = (chunk * KV_CHUNK + (tt - 1) * w, chunk, (tt - 1) * w, [(0, prev_ok)], None)
            next_chunk = jnp.minimum(chunk + 1, n_chunks - 1)
            bias = jnp.where(last_step, NEG_INF, 0.0) if t == blocks - 1 else None
            single = (next_chunk * KV_CHUNK, next_chunk, 0, [(0, next_ok)], bias)
        return pair, single

    def q_block(t):
        rows = slice(t * w, (t + 1) * w)
        return jnp.concatenate([q_ref[0, 0, rows, :], q_ref[0, 1, rows, :]], axis=0)

    def scores(t, s_ref):
        qt = q_block(t)
        s_ctx = _dot_nt(k_ctx, qt)
        s_ref[0:tc, :] = s_ctx
        cmax = jnp.max(s_ctx, axis=0, keepdims=True)
        row0 = tc
        for (start, _, _, masked, bias), rows in zip(pieces(t), (2 * w, w)):
            k = kl_ref[0, 0, pl.ds(pl.multiple_of(start, w), rows), :]
            s = _dot_nt(k, qt)
            if bias is not None:
                s = s + bias
            parts = []
            for r0 in range(0, rows, w):
                part = s[r0:r0 + w]
                for off, mask in masked:
                    if off == r0:
                        part = jnp.where(mask, part, NEG_INF)
                parts.append(part)
                cmax = jnp.maximum(cmax, jnp.max(part, axis=0, keepdims=True))
            for j, part in enumerate(parts):
                s_ref[row0 + j * w:row0 + (j + 1) * w, :] = part
            row0 += rows
        return cmax

    def softmax_pv(t, s_ref, cmax):
        m = jnp.maximum(sink, cmax)
        acc = _dotf(vtc_ref[0, 0, 0], jnp.exp2(s_ref[0:tc, :] - m).astype(BF16))
        row0 = tc
        for (_, chunk, lane0, _, _), rows in zip(pieces(t), (2 * w, w)):
            vt = vtl_ref[0, 0, chunk, :, lane0:lane0 + rows]
            acc = acc + _dotf(vt, jnp.exp2(s_ref[row0:row0 + rows, :] - m).astype(BF16))
            row0 += rows
        l = acc[HEAD_DIM:HEAD_DIM + 1, :] + jnp.exp2(sink - m)
        o_t = acc[0:HEAD_DIM, :] * (1.0 / l)
        rows = slice(t * w, (t + 1) * w)
        for g in range(2):
            gcols = slice(g * LANES, (g + 1) * LANES)
            o = o_t[:, g * w:(g + 1) * w].T
            out_ref[0, rows, gcols] = o.astype(BF16)

    n_buf = len(bufs)
    ahead = n_buf - 1
    cmaxes = [scores(t, bufs[t % n_buf]) for t in range(min(ahead, blocks))]
    for t in range(blocks):
        if t + ahead < blocks:
            cmaxes.append(scores(t + ahead, bufs[(t + ahead) % n_buf]))
        softmax_pv(t, bufs[t % n_buf], cmaxes[t])


def _band_attention(q, kc, vtc, kl, vtl, gate, sink):
    bsz, _, s, dk = q.shape
    tq = min(TQ_BAND, s)
    tc = kc.shape[2]
    n_chunks = s // KV_CHUNK
    assert tq % KV_CHUNK == 0 and dk == HEAD_DIM
    return pl.pallas_call(
        functools.partial(_band_attn_kernel, blocks=tq // WINDOW, n_chunks=n_chunks),
        grid=(bsz, 2, s // tq),
        in_specs=[
            pl.BlockSpec((1, 2, tq, dk), lambda b, h, i: (b, h, i, 0)),
            pl.BlockSpec((1, 1, tc, dk), lambda b, h, i: (b, h, 0, 0)),
            pl.BlockSpec((1, 1, 1, DV_AUG, tc), lambda b, h, i: (b, h, 0, 0, 0)),
            pl.BlockSpec((1, 1, s, dk), lambda b, h, i: (b, h, 0, 0)),
            pl.BlockSpec((1, 1, n_chunks, DV_AUG, KV_CHUNK), lambda b, h, i: (b, h, 0, 0, 0)),
            pl.BlockSpec((1, 1, 2 * WINDOW), lambda b, h, i: (h, 0, 0)),
        ],
        out_specs=pl.BlockSpec((1, tq, 2 * LANES), lambda b, h, i: (b, i, h)),
        out_shape=jax.ShapeDtypeStruct((bsz, s, GROUP_WIDTH), BF16),
        scratch_shapes=[pltpu.VMEM((tc + 3 * WINDOW, 2 * WINDOW), F32)] * (BAND_AHEAD + 1),
        compiler_params=_params(3),
        name="attn_a_band",
    )(q, kc, vtc, kl, vtl, sink)


def _out_kernel(*refs, final):
    os_, gs = refs[0:4], refs[4:8]
    w_ref, x_ref, gate_ref = refs[8:11]
    out_ref = refs[-1]
    u = jnp.concatenate([o[0] * g[0] for o, g in zip(os_, gs)], axis=1)
    acc = _dotf(u, w_ref[...].reshape(4 * GROUP_WIDTH, D_MODEL))
    xn = x_ref[0] + gate_ref[0, 0] * acc
    if final:
        xn = _rms(xn, 1.0 / D_MODEL) * refs[11][...]
    out_ref[0] = xn


def _out_project(us, gzs, w_out, x, mods, layer, row_fn, tm, final_g=None):
    bsz, t, d = x.shape
    u_spec = pl.BlockSpec((1, tm, GROUP_WIDTH), lambda b, i: (b, i, 0))
    specs = [u_spec] * 8 + [_full_spec(w_out), pl.BlockSpec((1, tm, d), lambda b, i: (b, i, 0)),
                            _mod_spec(layer, row_fn, 2)]
    args = list(us) + list(gzs) + [w_out, x, mods]
    if final_g is not None:
        specs.append(pl.BlockSpec((1, d), lambda b, i: (0, 0)))
        args.append(final_g)
    return pl.pallas_call(
        functools.partial(_out_kernel, final=final_g is not None),
        grid=(bsz, t // tm),
        in_specs=specs,
        out_specs=pl.BlockSpec((1, tm, d), lambda b, i: (b, i, 0)),
        out_shape=jax.ShapeDtypeStruct((bsz, t, d), F32),
        compiler_params=_params(2),
        name="out_proj",
    )(*args)


def _rope_tables(rows, rot_dim):
    row = jnp.broadcast_to(jnp.arange(rows)[:, None], (rows, GRID_W)).reshape(-1).astype(F32)
    col = jnp.broadcast_to(jnp.arange(GRID_W)[None, :], (rows, GRID_W)).reshape(-1).astype(F32)
    axis_dim = rot_dim // 2
    inv_freq = ROPE_THETA ** (-jnp.arange(0, axis_dim, 2, dtype=F32) / axis_dim)
    ang_r = row[:, None] * inv_freq[None, :]
    ang_c = col[:, None] * inv_freq[None, :]
    ang = jnp.concatenate([ang_r, ang_r, ang_c, ang_c], axis=-1)
    quarter = rot_dim // 4
    sign = jnp.where((jnp.arange(rot_dim) // quarter) % 2 == 0, -1.0, 1.0).astype(F32)
    cos, sin = jnp.cos(ang), jnp.sin(ang) * sign
    reps = LANES // rot_dim
    return jnp.tile(cos, (1, reps)), jnp.tile(sin, (1, reps))


def _layer_weights(w_in, w_uq, w_ukv, gq, gkv):
    wa = w_in[:, 0:1536]
    wb = w_in[:, 1536:3584]
    zeros64 = jnp.zeros((w_in.shape[0], 64), w_in.dtype)
    wc = jnp.concatenate([w_in[:, 3584:4032], zeros64,
                          w_in[:, 4032:4160],
                          w_in[:, 4160:4224], zeros64,
                          w_in[:, 4224:4736]], axis=1)
    wd = w_in[:, 4736:6272]
    uq = w_uq.reshape(C_Q_LORA, 4, C_QK_DIM)
    uq = jnp.concatenate([uq[:, :, :128].reshape(C_Q_LORA, 512),
                          uq[:, :, 128:].reshape(C_Q_LORA, 256)], axis=1)
    uq = jnp.concatenate([uq, jnp.zeros((64, 768), uq.dtype)], axis=0)
    gq_pad = jnp.concatenate([gq, jnp.zeros((64,), gq.dtype)])[None, :]
    cast = lambda a: a.astype(BF16)
    return cast(wa), cast(wb), cast(wc), cast(wd), cast(uq), cast(w_ukv), gq_pad, gkv[None, :]


def kernel(x, c, ctx, c_ctx, w_mod, b_mod, norm_g, w_in, c_q_norm_g, c_kv_norm_g, c_w_uq, c_w_ukv,
           d_q_norm_g, d_k_norm_g, a_sink, b_lambda, b_subln_g, w_out, final_norm_g):
    bsz, s, d = x.shape
    tc = ctx.shape[1]
    depth = w_mod.shape[0]
    assert d == D_MODEL and s % TM_LATENT == 0 and s % GRID_W == 0 and bsz + 1 <= MOD_ROWS
    assert tc % LANES == 0 and TM_LATENT % KV_CHUNK == 0 and tc <= KV_CHUNK

    cc = jnp.concatenate([c, c_ctx[None, :], jnp.zeros((MOD_ROWS - bsz - 1, d), c.dtype)], axis=0)
    mods = _modulation(cc, w_mod, b_mod).reshape(depth, MOD_ROWS, 1, 3 * d)
    lat_row = lambda b: b
    ctx_row = lambda b: bsz

    rows = s // GRID_W
    cos_h, sin_h = _rope_tables(rows, HEAD_DIM)
    cos_b, sin_b = _rope_tables(rows, B_QK_DIM)
    one = jnp.ones((tc, LANES), F32)
    zero = jnp.zeros((tc, LANES), F32)
    tab_lat = {"h": (cos_h, sin_h), "b": (cos_b, sin_b)}
    tab_ctx = {"h": (one, zero), "b": (one, zero)}

    tq2, tq1 = min(N_DENSE // 2, s), min(N_DENSE, s)
    for l in range(depth):
        last = l == depth - 1
        wa, wb, wc, wd, uq, ukv, gq_pad, gkv = _layer_weights(
            w_in[l], c_w_uq[l], c_w_ukv[l], c_q_norm_g[l], c_kv_norm_g[l])
        dq_g, dk_g = d_q_norm_g[l][None, :], d_k_norm_g[l][None, :]
        wo = w_out[l].astype(BF16).reshape(4, GROUP_WIDTH, d)
        lam_init = 0.8 - 0.6 * math.exp(-0.3 * l)
        diff = (b_lambda[l], b_subln_g[l][None, :], lam_init)
        sink2 = a_sink[l].reshape(2, 2)

        def project(stream, row_fn, tabs, tm):
            norm = (norm_g[l][None, :], mods, l, row_fn)
            pa, pd = _project(_proj_ad_kernel, "proj_ad", stream, norm, [wa, wd, dq_g, dk_g], tabs["h"],
                              tm, [(4, 128, 2, 128), (4, 128, 2, 128)])
            pb, = _project(_proj_b_kernel, "proj_b", stream, norm, [wb], tabs["b"], tm, [(8, 128, 4, 128)])
            pc, = _project(_proj_c_kernel, "proj_c", stream, norm, [wc, uq, ukv, gq_pad, gkv], tabs["b"],
                           tm, [(4, 256, 4, 256)])
            return pa, pb, pc, pd

        def sink_rows(tq):
            return jnp.repeat(sink2, tq, axis=1).reshape(2, 1, 2 * tq)

        (qa, ka, vta, gza), (qb, kb, vtb, gzb), (qc, kc, vtc, gzc), (qd, kd, vtd, gzd) = project(
            x, lat_row, tab_lat, TM_LATENT)
        (qa_c, ka_c, vta_c, gza_c), (qb_c, kb_c, vtb_c, gzb_c), (qc_c, kc_c, vtc_c, gzc_c), \
            (qd_c, kd_c, vtd_c, gzd_c) = project(ctx, ctx_row, tab_ctx, tc)

        ua = _band_attention(qa, ka_c, vta_c, ka, vta, gza, sink_rows(WINDOW))
        ub = _dense_attention("attn_b", qb, kb_c, vtb_c, kb, vtb, gzb, 2, tq2, diff=diff)
        uc = _dense_attention("attn_c", qc, kc_c, vtc_c, kc, vtc, gzc, 1, tq1)
        ud = _dense_attention("attn_d", qd, kd_c, vtd_c, kd, vtd, gzd, 2, tq2)
        if not last:
            ua_c = _dense_attention("attn_a_ctx", qa_c, ka_c, vta_c, None, None, gza_c, 2, tc,
                                    sink=sink_rows(tc))
            ub_c = _dense_attention("attn_b_ctx", qb_c, kb_c, vtb_c, None, None, gzb_c, 2, tc, diff=diff)
            uc_c = _dense_attention("attn_c_ctx", qc_c, kc_c, vtc_c, None, None, gzc_c, 1, tc)
            ud_c = _dense_attention("attn_d_ctx", qd_c, kd_c, vtd_c, None, None, gzd_c, 2, tc)
            ctx = _out_project((ua_c, ub_c, uc_c, ud_c), (gza_c, gzb_c, gzc_c, gzd_c), wo, ctx, mods, l,
                               ctx_row, tc)
        x = _out_project((ua, ub, uc, ud), (gza, gzb, gzc, gzd), wo, x, mods, l, lat_row, TM_OUT,
                         final_g=final_norm_g[None, :] if last else None)
    return x
```

```python
import functools
import math

import jax
import jax.numpy as jnp
from jax import lax
from jax.experimental import pallas as pl
from jax.experimental.pallas import tpu as pltpu

F32 = jnp.float32
BF16 = jnp.bfloat16

D_MODEL = 2048
HEAD_DIM = 128
GRID_W = 64
GROUP_WIDTH = 512
WINDOW = 128
ROPE_THETA = 10000.0
EPS = 1e-6
NEG_INF = -1e30
C_Q_LORA = 448
C_ROPE = 64
C_QK_DIM = 192
B_QK_DIM = 64
LOG2E = math.log2(math.e)
LANES = 128
ONES_ROWS = 16
DV_AUG = HEAD_DIM + ONES_ROWS
MOD_ROWS = 16
VMEM_LIMIT = 56 * 1024 * 1024

TM_LATENT = 1024
TM_OUT = 512
KV_CHUNK = 512
N_DENSE = 4096
TQ_BAND = 1024
BAND_AHEAD = 3
MOD_TN = 512
COL_TILE = 256


def _dotf(a, b):
    return jnp.dot(a, b, preferred_element_type=F32)


def _dot_nt(a, b):
    return lax.dot_general(a, b, (((1,), (1,)), ((), ())), preferred_element_type=F32)


def _silu(z):
    return z / (1.0 + jnp.exp(-z))


def _rms(y, inv_n):
    return y * lax.rsqrt(jnp.sum(y * y, axis=-1, keepdims=True) * inv_n + EPS)


def _rope(x, cos, sin_signed, shift):
    lane = lax.broadcasted_iota(jnp.int32, x.shape, 1)
    even = (lane & shift) == 0
    x_up = pltpu.roll(x, LANES - shift, 1)
    x_dn = pltpu.roll(x, shift, 1)
    return x * cos + jnp.where(even, x_up, x_dn) * sin_signed


def _params(n_axes):
    return pltpu.CompilerParams(dimension_semantics=("parallel",) * n_axes,
                                vmem_limit_bytes=VMEM_LIMIT)


def _mod_kernel(c_ref, w_ref, b_ref, o_ref):
    a = _silu(c_ref[...]).astype(BF16)
    o_ref[0] = _dotf(a, w_ref[0].astype(BF16)) + b_ref[0]


def _modulation(cc, w_mod, b_mod):
    depth, d, n = w_mod.shape
    return pl.pallas_call(
        _mod_kernel,
        grid=(depth, n // MOD_TN),
        in_specs=[pl.BlockSpec((MOD_ROWS, d), lambda l, j: (0, 0)),
                  pl.BlockSpec((1, d, MOD_TN), lambda l, j: (l, 0, j)),
                  pl.BlockSpec((1, 1, MOD_TN), lambda l, j: (l, 0, j))],
        out_specs=pl.BlockSpec((1, MOD_ROWS, MOD_TN), lambda l, j: (l, 0, j)),
        out_shape=jax.ShapeDtypeStruct((depth, MOD_ROWS, n), F32),
        compiler_params=_params(2),
        name="modulation",
    )(cc, w_mod, b_mod.reshape(depth, 1, n))


def _prenorm(x_ref, g_ref, shift_ref, scale_ref):
    y = _rms(x_ref[0], 1.0 / D_MODEL) * g_ref[...]
    return (y * (1.0 + scale_ref[0, 0]) + shift_ref[0, 0]).astype(BF16)


def _mod_spec(layer, row_fn, which):
    return pl.BlockSpec((1, 1, 1, D_MODEL), lambda b, i: (layer, row_fn(b), 0, which))


def _store_vt(vt_ref, head, y):
    chunk = vt_ref.shape[-1]
    for j in range(vt_ref.shape[2]):
        vt_ref[0, head, j, 0:HEAD_DIM, :] = y[j * chunk:(j + 1) * chunk, :].T.astype(BF16)
        vt_ref[0, head, j, HEAD_DIM:DV_AUG, :] = jnp.ones((ONES_ROWS, chunk), BF16)


def _proj_a_body(h, w_ref, cos, sin, q_ref, k_ref, vt_ref, gz_ref):
    scale = LOG2E / math.sqrt(HEAD_DIM)
    yq = _dotf(h, w_ref[:, 0:512])
    for hh in range(4):
        q_ref[0, hh] = (_rope(yq[:, hh * 128:(hh + 1) * 128], cos, sin, 32) * scale).astype(BF16)
    ykv = _dotf(h, w_ref[:, 512:1024])
    for hh in range(2):
        k_ref[0, hh] = _rope(ykv[:, hh * 128:(hh + 1) * 128], cos, sin, 32).astype(BF16)
        _store_vt(vt_ref, hh, ykv[:, 256 + hh * 128:256 + (hh + 1) * 128])
    gz_ref[0] = _silu(_dotf(h, w_ref[:, 1024:1536])).astype(BF16)


def _proj_b_kernel(x_ref, g_ref, shift_ref, scale_ref, w_ref, cos_ref, sin_ref,
                   q_ref, k_ref, vt_ref, gz_ref):
    h = _prenorm(x_ref, g_ref, shift_ref, scale_ref)
    cos, sin = cos_ref[...], sin_ref[...]
    scale = LOG2E / math.sqrt(B_QK_DIM)
    yq = _dotf(h, w_ref[:, 0:512])
    lane = lax.broadcasted_iota(jnp.int32, (h.shape[0], LANES), 1)
    first = lane < B_QK_DIM
    for hh in range(4):
        y = _rope(yq[:, hh * 128:(hh + 1) * 128], cos, sin, 16) * scale
        q_ref[0, 2 * hh] = jnp.where(first, y, 0.0).astype(BF16)
        q_ref[0, 2 * hh + 1] = jnp.where(first, 0.0, y).astype(BF16)
    yk = _dotf(h, w_ref[:, 512:1024])
    for hh in range(4):
        k_ref[0, hh] = _rope(yk[:, hh * 128:(hh + 1) * 128], cos, sin, 16).astype(BF16)
    yv = _dotf(h, w_ref[:, 1024:1536])
    for hh in range(4):
        _store_vt(vt_ref, hh, yv[:, hh * 128:(hh + 1) * 128])
    gz_ref[0] = _silu(_dotf(h, w_ref[:, 1536:2048])).astype(BF16)


def _proj_c_kernel(x_ref, g_ref, shift_ref, scale_ref, w_ref, wuq_ref, wukv_ref, gq_ref, gkv_ref,
                   cos_ref, sin_ref, q_ref, k_ref, vt_ref, gz_ref):
    h = _prenorm(x_ref, g_ref, shift_ref, scale_ref)
    cos, sin = cos_ref[...], sin_ref[...]
    scale = LOG2E / math.sqrt(C_QK_DIM)
    ycq = _dotf(h, w_ref[:, 0:512])
    yckv = _dotf(h, w_ref[:, 512:768])
    gz_ref[0] = _silu(_dotf(h, w_ref[:, 768:1280])).astype(BF16)
    cqn = (_rms(ycq, 1.0 / C_Q_LORA) * gq_ref[...]).astype(BF16)
    yq = _dotf(cqn, wuq_ref[...])
    ckvn = (_rms(yckv[:, 0:128], 1.0 / 128) * gkv_ref[...]).astype(BF16)
    ykv = _dotf(ckvn, wukv_ref[...])
    kr_even = _rope(yckv[:, 128:256], cos, sin, 16)
    kr_odd = pltpu.roll(kr_even, C_ROPE, 1)
    qr = [_rope(yq[:, 512 + p * 128:512 + (p + 1) * 128], cos, sin, 16) * scale for p in range(2)]
    for hh in range(4):
        q_ref[0, hh, :, 0:128] = (yq[:, hh * 128:(hh + 1) * 128] * scale).astype(BF16)
        q_ref[0, hh, :, 128:256] = qr[hh // 2].astype(BF16)
        k_ref[0, hh, :, 0:128] = ykv[:, hh * 256:hh * 256 + 128].astype(BF16)
        k_ref[0, hh, :, 128:256] = (kr_even if hh % 2 == 0 else kr_odd).astype(BF16)
        _store_vt(vt_ref, hh, ykv[:, hh * 256 + 128:(hh + 1) * 256])


def _proj_ad_kernel(x_ref, g_ref, shift_ref, scale_ref, wa_ref, wd_ref, gq_ref, gk_ref, cos_ref, sin_ref,
                    qa_ref, ka_ref, vta_ref, gza_ref, qd_ref, kd_ref, vtd_ref, gzd_ref):
    h = _prenorm(x_ref, g_ref, shift_ref, scale_ref)
    cos, sin = cos_ref[...], sin_ref[...]
    scale = LOG2E / math.sqrt(HEAD_DIM)
    yq = _dotf(h, wd_ref[:, 0:512])
    for hh in range(4):
        yn = _rms(yq[:, hh * 128:(hh + 1) * 128], 1.0 / HEAD_DIM) * gq_ref[...]
        qd_ref[0, hh] = (_rope(yn, cos, sin, 32) * scale).astype(BF16)
    yk = _dotf(h, wd_ref[:, 512:768])
    for hh in range(2):
        yn = _rms(yk[:, hh * 128:(hh + 1) * 128], 1.0 / HEAD_DIM) * gk_ref[...]
        kd_ref[0, hh] = _rope(yn, cos, sin, 32).astype(BF16)
    gzd_ref[0] = _silu(_dotf(h, wd_ref[:, 1024:1536])).astype(BF16)
    _proj_a_body(h, wa_ref, cos, sin, qa_ref, ka_ref, vta_ref, gza_ref)
    yv = _dotf(h, wd_ref[:, 768:1024])
    for hh in range(2):
        _store_vt(vtd_ref, hh, yv[:, hh * 128:(hh + 1) * 128])


def _full_spec(a):
    nd = a.ndim
    return pl.BlockSpec(a.shape, lambda b, i: (0,) * nd, pipeline_mode=pl.Buffered(1))


def _project(kern, name, x, norm, weights, tables, tm, mixers):
    bsz, t, d = x.shape
    nt = t // tm
    chunk = min(tm, KV_CHUNK)
    sub = tm // chunk
    gain, mods, layer, row_fn = norm
    tab_specs = [pl.BlockSpec((tm, LANES), (lambda b, i: (i, 0)) if tab.shape[0] == t
                              else (lambda b, i: (0, 0))) for tab in tables]
    out_specs, out_shape = [], []
    for n_q, dq, n_kv, dk in mixers:
        out_specs += [pl.BlockSpec((1, n_q, tm, dq), lambda b, i: (b, 0, i, 0)),
                      pl.BlockSpec((1, n_kv, tm, dk), lambda b, i: (b, 0, i, 0)),
                      pl.BlockSpec((1, n_kv, sub, DV_AUG, chunk), lambda b, i: (b, 0, i, 0, 0)),
                      pl.BlockSpec((1, tm, GROUP_WIDTH), lambda b, i: (b, i, 0))]
        out_shape += [jax.ShapeDtypeStruct((bsz, n_q, t, dq), BF16),
                      jax.ShapeDtypeStruct((bsz, n_kv, t, dk), BF16),
                      jax.ShapeDtypeStruct((bsz, n_kv, nt * sub, DV_AUG, chunk), BF16),
                      jax.ShapeDtypeStruct((bsz, t, GROUP_WIDTH), BF16)]
    outs = pl.pallas_call(
        kern,
        grid=(bsz, nt),
        in_specs=[pl.BlockSpec((1, tm, d), lambda b, i: (b, i, 0)),
                  pl.BlockSpec((1, d), lambda b, i: (0, 0)),
                  _mod_spec(layer, row_fn, 0), _mod_spec(layer, row_fn, 1)]
        + [_full_spec(w) for w in weights] + tab_specs,
        out_specs=out_specs,
        out_shape=out_shape,
        compiler_params=_params(2),
        name=name,
    )(x, gain, mods, mods, *weights, *tables)
    return [tuple(outs[4 * j:4 * j + 4]) for j in range(len(mixers))]


def _finish_plain(acc, l, gate_ref, out_ref, n_sets, tq):
    o_t = acc * (1.0 / l)
    for g in range(n_sets):
        o = o_t[:, g * tq:(g + 1) * tq].astype(BF16).T
        out_ref[0, :, g * 128:(g + 1) * 128] = o


def _finish_diff(acc, l, gate_ref, out_ref, lam_ref, subg_ref, lam_init, tq):
    o_t = acc * (1.0 / l)
    lp = lam_ref[...]
    s1 = jnp.sum(lp[0:1] * lp[1:2], axis=1, keepdims=True)
    s2 = jnp.sum(lp[2:3] * lp[3:4], axis=1, keepdims=True)
    lam = jnp.exp(s1) - jnp.exp(s2) + lam_init
    d = o_t[:, 0:tq] - lam * o_t[:, tq:2 * tq]
    dn = d * lax.rsqrt(jnp.mean(d * d, axis=0, keepdims=True) + EPS)
    o = dn.astype(BF16).T.astype(F32) * subg_ref[...] * (1.0 - lam_init)
    out_ref[0] = o.astype(BF16)


def _dense_attn_kernel(*refs, n_sets, tq, n_chunks, has_sink, diff, lam_init):
    refs = list(refs)
    q_ref, kc_ref, vtc_ref = refs[:3]
    pos = 3
    if n_chunks:
        kl_ref, vtl_ref = refs[pos:pos + 2]
        pos += 2
    gate_ref = None
    if has_sink:
        sink_ref = refs[pos]
        pos += 1
    if diff:
        lam_ref, subg_ref = refs[pos:pos + 2]
        pos += 2
    out_ref, acc_ref = refs[pos:pos + 2]
    s_refs = refs[pos + 2:pos + 4]

    n = n_sets * tq
    n_tiles = n // COL_TILE
    per_set = tq // COL_TILE
    tc = kc_ref.shape[2]
    acc_ref[0:HEAD_DIM, :] = jnp.zeros((HEAD_DIM, n), F32)
    if has_sink:
        m0 = sink_ref[0] * LOG2E
        acc_ref[HEAD_DIM:DV_AUG, :] = jnp.ones((ONES_ROWS, n), F32)
    else:
        m0 = jnp.full((1, n), NEG_INF, F32)
        acc_ref[HEAD_DIM:DV_AUG, :] = jnp.zeros((ONES_ROWS, n), F32)

    def cols(c):
        return slice(c * COL_TILE, (c + 1) * COL_TILE)

    def scores(k, c, s_ref, rows):
        qt = q_ref[0, c // per_set, (c % per_set) * COL_TILE:(c % per_set + 1) * COL_TILE, :]
        s = _dot_nt(k, qt)
        s_ref[0:rows, cols(c)] = s
        return jnp.max(s, axis=0, keepdims=True)

    def softmax_pv(s_ref, rows, cmax, vt, m, c):
        m_new = jnp.maximum(m, cmax)
        alpha = jnp.exp2(m - m_new)
        p = jnp.exp2(s_ref[0:rows, cols(c)] - m_new)
        pv = _dotf(vt, p.astype(BF16))
        acc_ref[:, cols(c)] = alpha * acc_ref[:, cols(c)] + pv
        return m_new

    def stage(k_next, s_next, s_cur, rows_cur, cmax_cur, vt_cur, m):
        ms, cm = [], []
        for c in range(n_tiles):
            if k_next is not None:
                cm.append(scores(k_next, c, s_next, KV_CHUNK))
            if s_cur is not None:
                ms.append(softmax_pv(s_cur, rows_cur, cmax_cur[:, cols(c)], vt_cur, m[:, cols(c)], c))
        cat = lambda xs: jnp.concatenate(xs, axis=1) if xs else None
        return cat(cm), cat(ms)

    s_a, s_b = s_refs
    k_ctx = kc_ref[0, 0]
    cmax = jnp.concatenate([scores(k_ctx, c, s_a, tc) for c in range(n_tiles)], axis=1)
    if not n_chunks:
        _, m = stage(None, None, s_a, tc, cmax, vtc_ref[0, 0, 0], m0)
    else:
        def k_chunk(j):
            start = pl.multiple_of(j * KV_CHUNK, KV_CHUNK)
            return kl_ref[0, 0, pl.ds(start, KV_CHUNK), :]

        cmax, m = stage(k_chunk(0), s_b, s_a, tc, cmax, vtc_ref[0, 0, 0], m0)

        def body(i, carry):
            cmax, m = carry
            j = 2 * i
            cmax, m = stage(k_chunk(j + 1), s_a, s_b, KV_CHUNK, cmax, vtl_ref[0, 0, j], m)
            return stage(k_chunk(j + 2), s_b, s_a, KV_CHUNK, cmax, vtl_ref[0, 0, j + 1], m)

        cmax, m = lax.fori_loop(0, n_chunks // 2 - 1, body, (cmax, m))
        j = n_chunks - 2
        cmax, m = stage(k_chunk(j + 1), s_a, s_b, KV_CHUNK, cmax, vtl_ref[0, 0, j], m)
        _, m = stage(None, None, s_a, KV_CHUNK, cmax, vtl_ref[0, 0, j + 1], m)

    acc = acc_ref[0:HEAD_DIM, :]
    l = acc_ref[HEAD_DIM:HEAD_DIM + 1, :]
    if diff:
        _finish_diff(acc, l, gate_ref, out_ref, lam_ref, subg_ref, lam_init, tq)
    else:
        _finish_plain(acc, l, gate_ref, out_ref, n_sets, tq)


def _dense_attention(name, q, kc, vtc, kl, vtl, gate, n_sets, tq, sink=None, diff=None):
    bsz, _, t, dk = q.shape
    hkv, tc = kc.shape[1], kc.shape[2]
    dv = vtc.shape[3]
    n_chunks = 0 if kl is None else kl.shape[2] // KV_CHUNK
    assert n_chunks % 2 == 0 and tq % COL_TILE == 0
    out_w = LANES if diff else n_sets * LANES
    n = n_sets * tq
    args = [q, kc, vtc]
    specs = [pl.BlockSpec((1, n_sets, tq, dk), lambda b, h, i: (b, h, i, 0)),
             pl.BlockSpec((1, 1, tc, dk), lambda b, h, i: (b, h, 0, 0)),
             pl.BlockSpec((1, 1, 1, dv, tc), lambda b, h, i: (b, h, 0, 0, 0))]
    if n_chunks:
        args += [kl, vtl]
        specs += [pl.BlockSpec((1, 1, kl.shape[2], dk), lambda b, h, i: (b, h, 0, 0)),
                  pl.BlockSpec((1, 1, n_chunks, dv, KV_CHUNK), lambda b, h, i: (b, h, 0, 0, 0))]
    if sink is not None:
        args.append(sink)
        specs.append(pl.BlockSpec((1, 1, n), lambda b, h, i: (h, 0, 0)))
    lam_init = 0.0
    if diff is not None:
        lam_par, subg, lam_init = diff
        args += [lam_par, subg]
        specs += [pl.BlockSpec(lam_par.shape, lambda b, h, i: (0, 0)),
                  pl.BlockSpec(subg.shape, lambda b, h, i: (0, 0))]
    kern = functools.partial(_dense_attn_kernel, n_sets=n_sets, tq=tq, n_chunks=n_chunks,
                             has_sink=sink is not None, diff=diff is not None, lam_init=lam_init)
    return pl.pallas_call(
        kern,
        grid=(bsz, hkv, t // tq),
        in_specs=specs,
        out_specs=pl.BlockSpec((1, tq, out_w), lambda b, h, i: (b, i, h)),
        out_shape=jax.ShapeDtypeStruct((bsz, t, GROUP_WIDTH), BF16),
        scratch_shapes=[pltpu.VMEM((dv, n), F32)] + [pltpu.VMEM((max(KV_CHUNK, tc), n), F32)] * 2,
        compiler_params=_params(3),
        name=name,
    )(*args)


def _band_attn_kernel(q_ref, kc_ref, vtc_ref, kl_ref, vtl_ref, sink_ref, out_ref, *bufs,
                      blocks, n_chunks):
    w = WINDOW
    per_chunk = KV_CHUNK // w
    tc = kc_ref.shape[2]
    i = pl.program_id(2)
    first_step = i == 0
    last_step = i == pl.num_programs(2) - 1
    r1 = lax.broadcasted_iota(jnp.int32, (w, 1), 0)
    qc = lax.broadcasted_iota(jnp.int32, (1, 2 * w), 1) & (w - 1)
    prev_ok = r1 >= qc
    next_ok = r1 <= qc
    sink = sink_ref[0] * LOG2E
    k_ctx = kc_ref[0, 0]

    def pieces(t):
        chunk = i * (blocks // per_chunk) + t // per_chunk
        tt = t % per_chunk
        if tt < per_chunk - 1:
            pair = (chunk * KV_CHUNK + tt * w, chunk, tt * w, [(w, next_ok)], None)
            if tt > 0:
                single = (chunk * KV_CHUNK + (tt - 1) * w, chunk, (tt - 1) * w, [(0, prev_ok)], None)
            else:
                prev_chunk = jnp.maximum(chunk - 1, 0)
                bias = jnp.where(first_step, NEG_INF, 0.0) if t == 0 else None
                single = (prev_chunk * KV_CHUNK + (per_chunk - 1) * w, prev_chunk, (per_chunk - 1) * w,
                          [(0, prev_ok)], bias)
        else:
            pair = (chunk * KV_CHUNK + (tt - 1) * w, chunk, (tt - 1) * w, [(0, prev_ok)], None)
            next_chunk = jnp.minimum(chunk + 1, n_chunks - 1)
            bias = jnp.where(last_step, NEG_INF, 0.0) if t == blocks - 1 else None
            single = (next_chunk * KV_CHUNK, next_chunk, 0, [(0, next_ok)], bias)
        return pair, single

    def q_block(t):
        rows = slice(t * w, (t + 1) * w)
        return jnp.concatenate([q_ref[0, 0, rows, :], q_ref[0, 1, rows, :]], axis=0)

    def scores(t, s_ref):
        qt = q_block(t)
        s_ctx = _dot_nt(k_ctx, qt)
        s_ref[0:tc, :] = s_ctx
        cmax = jnp.max(s_ctx, axis=0, keepdims=True)
        row0 = tc
        for (start, _, _, masked, bias), rows in zip(pieces(t), (2 * w, w)):
            k = kl_ref[0, 0, pl.ds(pl.multiple_of(start, w), rows), :]
            s = _dot_nt(k, qt)
            if bias is not None:
                s = s + bias
            parts = []
            for r0 in range(0, rows, w):
                part = s[r0:r0 + w]
                for off, mask in masked:
                    if off == r0:
                        part = jnp.where(mask, part, NEG_INF)
                parts.append(part)
                cmax = jnp.maximum(cmax, jnp.max(part, axis=0, keepdims=True))
            for j, part in enumerate(parts):
                s_ref[row0 + j * w:row0 + (j + 1) * w, :] = part
            row0 += rows
        return cmax

    def softmax_pv(t, s_ref, cmax):
        m = jnp.maximum(sink, cmax)
        acc = _dotf(vtc_ref[0, 0, 0], jnp.exp2(s_ref[0:tc, :] - m).astype(BF16))
        row0 = tc
        for (_, chunk, lane0, _, _), rows in zip(pieces(t), (2 * w, w)):
            vt = vtl_ref[0, 0, chunk, :, lane0:lane0 + rows]
            acc = acc + _dotf(vt, jnp.exp2(s_ref[row0:row0 + rows, :] - m).astype(BF16))
            row0 += rows
        l = acc[HEAD_DIM:HEAD_DIM + 1, :] + jnp.exp2(sink - m)
        o_t = acc[0:HEAD_DIM, :] * (1.0 / l)
        rows = slice(t * w, (t + 1) * w)
        for g in range(2):
            gcols = slice(g * LANES, (g + 1) * LANES)
            o = o_t[:, g * w:(g + 1) * w].T
            out_ref[0, rows, gcols] = o.astype(BF16)

    n_buf = len(bufs)
    ahead = n_buf - 1
    cmaxes = [scores(t, bufs[t % n_buf]) for t in range(min(ahead, blocks))]
    for t in range(blocks):
        if t + ahead < blocks:
            cmaxes.append(scores(t + ahead, bufs[(t + ahead) % n_buf]))
        softmax_pv(t, bufs[t % n_buf], cmaxes[t])


def _band_attention(q, kc, vtc, kl, vtl, gate, sink):
    bsz, _, s, dk = q.shape
    tq = min(TQ_BAND, s)
    tc = kc.shape[2]
    n_chunks = s // KV_CHUNK
    assert tq % KV_CHUNK == 0 and dk == HEAD_DIM
    return pl.pallas_call(
        functools.partial(_band_attn_kernel, blocks=tq // WINDOW, n_chunks=n_chunks),
        grid=(bsz, 2, s // tq),
        in_specs=[
            pl.BlockSpec((1, 2, tq, dk), lambda b, h, i: (b, h, i, 0)),
            pl.BlockSpec((1, 1, tc, dk), lambda b, h, i: (b, h, 0, 0)),
            pl.BlockSpec((1, 1, 1, DV_AUG, tc), lambda b, h, i: (b, h, 0, 0, 0)),
            pl.BlockSpec((1, 1, s, dk), lambda b, h, i: (b, h, 0, 0)),
            pl.BlockSpec((1, 1, n_chunks, DV_AUG, KV_CHUNK), lambda b, h, i: (b, h, 0, 0, 0)),
            pl.BlockSpec((1, 1, 2 * WINDOW), lambda b, h, i: (h, 0, 0)),
        ],
        out_specs=pl.BlockSpec((1, tq, 2 * LANES), lambda b, h, i: (b, i, h)),
        out_shape=jax.ShapeDtypeStruct((bsz, s, GROUP_WIDTH), BF16),
        scratch_shapes=[pltpu.VMEM((tc + 3 * WINDOW, 2 * WINDOW), F32)] * (BAND_AHEAD + 1),
        compiler_params=_params(3),
        name="attn_a_band",
    )(q, kc, vtc, kl, vtl, sink)


def _out_kernel(*refs, final):
    os_, gs = refs[0:4], refs[4:8]
    w_ref, x_ref, gate_ref = refs[8:11]
    out_ref = refs[-1]
    u = jnp.concatenate([o[0] * g[0] for o, g in zip(os_, gs)], axis=1)
    acc = _dotf(u, w_ref[...].reshape(4 * GROUP_WIDTH, D_MODEL))
    xn = x_ref[0] + gate_ref[0, 0] * acc
    if final:
        xn = _rms(xn, 1.0 / D_MODEL) * refs[11][...]
    out_ref[0] = xn


def _out_project(us, gzs, w_out, x, mods, layer, row_fn, tm, final_g=None):
    bsz, t, d = x.shape
    u_spec = pl.BlockSpec((1, tm, GROUP_WIDTH), lambda b, i: (b, i, 0))
    specs = [u_spec] * 8 + [_full_spec(w_out), pl.BlockSpec((1, tm, d), lambda b, i: (b, i, 0)),
                            _mod_spec(layer, row_fn, 2)]
    args = list(us) + list(gzs) + [w_out, x, mods]
    if final_g is not None:
        specs.append(pl.BlockSpec((1, d), lambda b, i: (0, 0)))
        args.append(final_g)
    return pl.pallas_call(
        functools.partial(_out_kernel, final=final_g is not None),
        grid=(bsz, t // tm),
        in_specs=specs,
        out_specs=pl.BlockSpec((1, tm, d), lambda b, i: (b, i, 0)),
        out_shape=jax.ShapeDtypeStruct((bsz, t, d), F32),
        compiler_params=_params(2),
        name="out_proj",
    )(*args)


def _rope_tables(rows, rot_dim):
    row = jnp.broadcast_to(jnp.arange(rows)[:, None], (rows, GRID_W)).reshape(-1).astype(F32)
    col = jnp.broadcast_to(jnp.arange(GRID_W)[None, :], (rows, GRID_W)).reshape(-1).astype(F32)
    axis_dim = rot_dim // 2
    inv_freq = ROPE_THETA ** (-jnp.arange(0, axis_dim, 2, dtype=F32) / axis_dim)
    ang_r = row[:, None] * inv_freq[None, :]
    ang_c = col[:, None] * inv_freq[None, :]
    ang = jnp.concatenate([ang_r, ang_r, ang_c, ang_c], axis=-1)
    quarter = rot_dim // 4
    sign = jnp.where((jnp.arange(rot_dim) // quarter) % 2 == 0, -1.0, 1.0).astype(F32)
    cos, sin = jnp.cos(ang), jnp.sin(ang) * sign
    reps = LANES // rot_dim
    return jnp.tile(cos, (1, reps)), jnp.tile(sin, (1, reps))


def _layer_weights(w_in, w_uq, w_ukv, gq, gkv):
    wa = w_in[:, 0:1536]
    wb = w_in[:, 1536:3584]
    zeros64 = jnp.zeros((w_in.shape[0], 64), w_in.dtype)
    wc = jnp.concatenate([w_in[:, 3584:4032], zeros64,
                          w_in[:, 4032:4160],
                          w_in[:, 4160:4224], zeros64,
                          w_in[:, 4224:4736]], axis=1)
    wd = w_in[:, 4736:6272]
    uq = w_uq.reshape(C_Q_LORA, 4, C_QK_DIM)
    uq = jnp.concatenate([uq[:, :, :128].reshape(C_Q_LORA, 512),
                          uq[:, :, 128:].reshape(C_Q_LORA, 256)], axis=1)
    uq = jnp.concatenate([uq, jnp.zeros((64, 768), uq.dtype)], axis=0)
    gq_pad = jnp.concatenate([gq, jnp.zeros((64,), gq.dtype)])[None, :]
    cast = lambda a: a.astype(BF16)
    return cast(wa), cast(wb), cast(wc), cast(wd), cast(uq), cast(w_ukv), gq_pad, gkv[None, :]


def kernel(x, c, ctx, c_ctx, w_mod, b_mod, norm_g, w_in, c_q_norm_g, c_kv_norm_g, c_w_uq, c_w_ukv,
           d_q_norm_g, d_k_norm_g, a_sink, b_lambda, b_subln_g, w_out, final_norm_g):
    bsz, s, d = x.shape
    tc = ctx.shape[1]
    depth = w_mod.shape[0]
    assert d == D_MODEL and s % TM_LATENT == 0 and s % GRID_W == 0 and bsz + 1 <= MOD_ROWS
    assert tc % LANES == 0 and TM_LATENT % KV_CHUNK == 0 and tc <= KV_CHUNK

    cc = jnp.concatenate([c, c_ctx[None, :], jnp.zeros((MOD_ROWS - bsz - 1, d), c.dtype)], axis=0)
    mods = _modulation(cc, w_mod, b_mod).reshape(depth, MOD_ROWS, 1, 3 * d)
    lat_row = lambda b: b
    ctx_row = lambda b: bsz

    rows = s // GRID_W
    cos_h, sin_h = _rope_tables(rows, HEAD_DIM)
    cos_b, sin_b = _rope_tables(rows, B_QK_DIM)
    one = jnp.ones((tc, LANES), F32)
    zero = jnp.zeros((tc, LANES), F32)
    tab_lat = {"h": (cos_h, sin_h), "b": (cos_b, sin_b)}
    tab_ctx = {"h": (one, zero), "b": (one, zero)}

    tq2, tq1 = min(N_DENSE // 2, s), min(N_DENSE, s)
    for l in range(depth):
        last = l == depth - 1
        wa, wb, wc, wd, uq, ukv, gq_pad, gkv = _layer_weights(
            w_in[l], c_w_uq[l], c_w_ukv[l], c_q_norm_g[l], c_kv_norm_g[l])
        dq_g, dk_g = d_q_norm_g[l][None, :], d_k_norm_g[l][None, :]
        wo = w_out[l].astype(BF16).reshape(4, GROUP_WIDTH, d)
        lam_init = 0.8 - 0.6 * math.exp(-0.3 * l)
        diff = (b_lambda[l], b_subln_g[l][None, :], lam_init)
        sink2 = a_sink[l].reshape(2, 2)

        def project(stream, row_fn, tabs, tm):
            norm = (norm_g[l][None, :], mods, l, row_fn)
            pa, pd = _project(_proj_ad_kernel, "proj_ad", stream, norm, [wa, wd, dq_g, dk_g], tabs["h"],
                              tm, [(4, 128, 2, 128), (4, 128, 2, 128)])
            pb, = _project(_proj_b_kernel, "proj_b", stream, norm, [wb], tabs["b"], tm, [(8, 128, 4, 128)])
            pc, = _project(_proj_c_kernel, "proj_c", stream, norm, [wc, uq, ukv, gq_pad, gkv], tabs["b"],
                           tm, [(4, 256, 4, 256)])
            return pa, pb, pc, pd

        def sink_rows(tq):
            return jnp.repeat(sink2, tq, axis=1).reshape(2, 1, 2 * tq)

        (qa, ka, vta, gza), (qb, kb, vtb, gzb), (qc, kc, vtc, gzc), (qd, kd, vtd, gzd) = project(
            x, lat_row, tab_lat, TM_LATENT)
        (qa_c, ka_c, vta_c, gza_c), (qb_c, kb_c, vtb_c, gzb_c), (qc_c, kc_c, vtc_c, gzc_c), \
            (qd_c, kd_c, vtd_c, gzd_c) = project(ctx, ctx_row, tab_ctx, tc)

        ua = _band_attention(qa, ka_c, vta_c, ka, vta, gza, sink_rows(WINDOW))
        ub = _dense_attention("attn_b", qb, kb_c, vtb_c, kb, vtb, gzb, 2, tq2, diff=diff)
        uc = _dense_attention("attn_c", qc, kc_c, vtc_c, kc, vtc, gzc, 1, tq1)
        ud = _dense_attention("attn_d", qd, kd_c, vtd_c, kd, vtd, gzd, 2, tq2)
        if not last:
            ua_c = _dense_attention("attn_a_ctx", qa_c, ka_c, vta_c, None, None, gza_c, 2, tc,
                                    sink=sink_rows(tc))
            ub_c = _dense_attention("attn_b_ctx", qb_c, kb_c, vtb_c, None, None, gzb_c, 2, tc, diff=diff)
            uc_c = _dense_attention("attn_c_ctx", qc_c, kc_c, vtc_c, None, None, gzc_c, 1, tc)
            ud_c = _dense_attention("attn_d_ctx", qd_c, kd_c, vtd_c, None, None, gzd_c, 2, tc)
            ctx = _out_project((ua_c, ub_c, uc_c, ud_c), (gza_c, gzb_c, gzc_c, gzd_c), wo, ctx, mods, l,
                               ctx_row, tc)
        x = _out_project((ua, ub, uc, ud), (gza, gzb, gzc, gzd), wo, x, mods, l, lat_row, TM_OUT,
                         final_g=final_norm_g[None, :] if last else None)
    return x
```
